```python
import math
import jax, jax.numpy as jnp
from jax import lax
import numpy as np

D_MODEL = 1024
BATCH = 8
SEQ = 4096
DEPTH = 4

CHUNK = 64
HEAD_DIM = 64
EPS = 1e-6
ROPE_THETA = 10000.0
A_HEADS = 4
A_LEFT_CHUNKS = 8
A_BAND = (A_LEFT_CHUNKS + 1) * CHUNK
REL_CLIP = 128
B_HEADS = 4
IDX_HEADS = 4
IDX_DIM = 64
TOPK_MAX = 256
Q_BLOCK = 128
C_WIDTH = 512
C_BLOCKS = 8
C_BLOCK_DIM = C_WIDTH // C_BLOCKS
C_CONV = 4
LRU_C = 8.0
N_BRANCH = 3
A_WIDTH = A_HEADS * HEAD_DIM
B_WIDTH = B_HEADS * HEAD_DIM
MIX_WIDTH = A_WIDTH + B_WIDTH + C_WIDTH
D_FF = -(-8 * D_MODEL // (3 * 256)) * 256
IN_SPLITS = (A_WIDTH, A_WIDTH, A_WIDTH,
             B_WIDTH, B_WIDTH, B_WIDTH,
             IDX_HEADS * IDX_DIM, IDX_DIM, IDX_HEADS,
             C_WIDTH, C_WIDTH,
             N_BRANCH * D_MODEL)
D_IN = sum(IN_SPLITS)

kernel_name = "hybrid_chunk_attn_dsa_rglru_gated"


def rms_norm(x, g):
    x32 = x.astype(jnp.float32)
    y = x32 * lax.rsqrt(jnp.mean(x32 * x32, axis=-1, keepdims=True) + EPS)
    return (y * g.astype(jnp.float32)).astype(x.dtype)


def rope_tables(seq):
    pos = jnp.arange(seq, dtype=jnp.float32)
    inv = 1.0 / (ROPE_THETA ** (jnp.arange(0, HEAD_DIM, 2, dtype=jnp.float32) / HEAD_DIM))
    ang = pos[:, None] * inv[None, :]
    ang = jnp.concatenate([ang, ang], axis=-1)
    return jnp.cos(ang), jnp.sin(ang)


def apply_rope(x, cos, sin):
    shape = (cos.shape[0],) + (1,) * (x.ndim - 3) + (cos.shape[1],)
    c, s = cos.reshape(shape), sin.reshape(shape)
    x32 = x.astype(jnp.float32)
    x1, x2 = jnp.split(x32, 2, axis=-1)
    rot = jnp.concatenate([-x2, x1], axis=-1)
    return (x32 * c + rot * s).astype(x.dtype)


def split_cols(z):
    offs, acc = [], 0
    for w in IN_SPLITS[:-1]:
        acc += w
        offs.append(acc)
    return jnp.split(z, offs, axis=-1)


def chunk_relpos_attention(q, k, v, rel_bias, rel_idx):
    b, s, h, d = q.shape
    n_c = s // CHUNK
    qc = q.reshape(b, n_c, CHUNK, h, d)
    pad = ((0, 0), (A_LEFT_CHUNKS, 0), (0, 0), (0, 0), (0, 0))
    kp = jnp.pad(k.reshape(b, n_c, CHUNK, h, d), pad)
    vp = jnp.pad(v.reshape(b, n_c, CHUNK, h, d), pad)
    kb = jnp.concatenate([kp[:, j:j + n_c] for j in range(A_LEFT_CHUNKS + 1)], axis=2)
    vb = jnp.concatenate([vp[:, j:j + n_c] for j in range(A_LEFT_CHUNKS + 1)], axis=2)
    sc = jnp.einsum('bcqhd,bckhd->bhcqk', qc, kb).astype(jnp.float32) * (d ** -0.5)
    bias = rel_bias.astype(jnp.float32)[:, rel_idx]
    sc = sc + bias[:, None]
    key_chunk = jnp.arange(n_c)[:, None] - A_LEFT_CHUNKS + jnp.arange(A_BAND)[None, :] // CHUNK
    valid = key_chunk >= 0
    sc = jnp.where(valid[None, None, :, None, :], sc, -jnp.inf)
    p = jax.nn.softmax(sc, axis=-1).astype(v.dtype)
    o = jnp.einsum('bhcqk,bckhd->bcqhd', p, vb)
    return o.reshape(b, s, h * d)


def indexer_sparse_attention(q, k, v, qi, ki, wi):
    b, s, h, d = q.shape
    topk = min(TOPK_MAX, s // 4)
    n_blk = s // Q_BLOCK
    key_chunk = jnp.arange(s) // CHUNK
    bidx = jnp.arange(b)[:, None, None]
    scale = d ** -0.5

    def to_blocks(t):
        return jnp.moveaxis(t.reshape((b, n_blk, Q_BLOCK) + t.shape[2:]), 1, 0)

    def block(args):
        blk, qb, qib, wib = args
        q_chunk = (blk * Q_BLOCK + jnp.arange(Q_BLOCK)) // CHUNK
        admissible = key_chunk[None, :] <= q_chunk[:, None]
        dots = jnp.einsum('bqhe,bse->bqhs', qib, ki).astype(jnp.float32)
        score = jnp.einsum('bqhs,bqh->bqs', jax.nn.relu(dots), wib.astype(jnp.float32))
        score = jnp.where(admissible[None], score, -jnp.inf)
        _, idx = lax.top_k(score, topk)
        sel_valid = (idx // CHUNK) <= q_chunk[None, :, None]
        ks = k[bidx, idx]
        vs = v[bidx, idx]
        att = jnp.einsum('bqhd,bqkhd->bhqk', qb, ks).astype(jnp.float32) * scale
        att = jnp.where(sel_valid[:, None], att, -jnp.inf)
        p = jax.nn.softmax(att, axis=-1).astype(v.dtype)
        return jnp.einsum('bhqk,bqkhd->bqhd', p, vs)

    out = lax.map(block, (jnp.arange(n_blk), to_blocks(q), to_blocks(qi), to_blocks(wi)))
    return jnp.moveaxis(out, 0, 1).reshape(b, s, h * d)


def rglru_branch(xc, yc, conv_w, conv_b, wa, ba, wx, bx, lam):
    b, s, c = xc.shape
    xp = jnp.pad(xc, ((0, 0), (C_CONV - 1, 0), (0, 0)))
    u = conv_b + sum(xp[:, j:j + s] * conv_w[j] for j in range(C_CONV))
    ub = u.reshape(b, s, C_BLOCKS, C_BLOCK_DIM)
    r = jax.nn.sigmoid(jnp.einsum('bsgi,gij->bsgj', ub, wa).reshape(b, s, c) + ba)
    i = jax.nn.sigmoid(jnp.einsum('bsgi,gij->bsgj', ub, wx).reshape(b, s, c) + bx)
    log_a = -LRU_C * r.astype(jnp.float32) * jax.nn.softplus(-lam.astype(jnp.float32))
    a = jnp.exp(log_a)
    gated_in = jnp.sqrt(-jnp.expm1(2.0 * log_a)) * (i * u).astype(jnp.float32)

    def combine(left, right):
        a1, b1 = left
        a2, b2 = right
        return a1 * a2, a2 * b1 + b2

    _, hs = lax.associative_scan(combine, (a, gated_in), axis=1)
    return hs.astype(xc.dtype) * jax.nn.gelu(yc)


def setup_inputs(seed: int = 0) -> dict:
    key = jax.random.key(seed)
    ks = jax.random.split(key, 20)
    f32 = jnp.float32
    nrm = lambda k, shp: jax.random.normal(k, shp, f32)
    x = nrm(ks[0], (BATCH, SEQ, D_MODEL))
    g_mix = 1.0 + 0.02 * nrm(ks[1], (DEPTH, D_MODEL))
    w_in = nrm(ks[2], (DEPTH, D_MODEL, D_IN)) * D_MODEL ** -0.5
    qk_gain_a = 1.0 + 0.02 * nrm(ks[3], (DEPTH, 2, HEAD_DIM))
    rel_bias = 0.1 * nrm(ks[4], (DEPTH, A_HEADS, 2 * REL_CLIP + 1))
    qk_gain_b = 1.0 + 0.02 * nrm(ks[5], (DEPTH, 2, HEAD_DIM))
    g_idx_k = 1.0 + 0.02 * nrm(ks[6], (DEPTH, IDX_DIM))
    conv_w = nrm(ks[7], (DEPTH, C_CONV, C_WIDTH)) * C_CONV ** -0.5
    conv_b = 0.01 * nrm(ks[8], (DEPTH, C_WIDTH))
    lru_wa = nrm(ks[9], (DEPTH, C_BLOCKS, C_BLOCK_DIM, C_BLOCK_DIM)) * C_BLOCK_DIM ** -0.5
    lru_ba = 0.01 * nrm(ks[10], (DEPTH, C_WIDTH))
    lru_wx = nrm(ks[11], (DEPTH, C_BLOCKS, C_BLOCK_DIM, C_BLOCK_DIM)) * C_BLOCK_DIM ** -0.5
    lru_bx = 0.01 * nrm(ks[12], (DEPTH, C_WIDTH))
    a_c = jax.random.uniform(ks[13], (DEPTH, C_WIDTH), f32, 0.9, 0.999)
    a0 = a_c ** (1.0 / LRU_C)
    lru_lambda = jnp.log(a0) - jnp.log1p(-a0)
    b_gate = 0.01 * nrm(ks[14], (DEPTH, N_BRANCH * D_MODEL))
    row_scale = jnp.concatenate([jnp.full((A_WIDTH,), A_WIDTH ** -0.5, f32),
                                 jnp.full((B_WIDTH,), B_WIDTH ** -0.5, f32),
                                 jnp.full((C_WIDTH,), C_WIDTH ** -0.5, f32)])
    w_branch = nrm(ks[15], (DEPTH, MIX_WIDTH, D_MODEL)) * row_scale[None, :, None]
    w_out = nrm(ks[16], (DEPTH, D_MODEL, D_MODEL)) * D_MODEL ** -0.5
    g_ffn = 1.0 + 0.02 * nrm(ks[17], (DEPTH, D_MODEL))
    w_ffn_in = nrm(ks[18], (DEPTH, D_MODEL, 2 * D_FF)) * D_MODEL ** -0.5
    w_ffn_out = nrm(ks[19], (DEPTH, D_FF, D_MODEL)) * D_FF ** -0.5
    return {"x": x, "g_mix": g_mix, "w_in": w_in, "qk_gain_a": qk_gain_a, "rel_bias": rel_bias,
            "qk_gain_b": qk_gain_b, "g_idx_k": g_idx_k, "conv_w": conv_w, "conv_b": conv_b,
            "lru_wa": lru_wa, "lru_ba": lru_ba, "lru_wx": lru_wx, "lru_bx": lru_bx,
            "lru_lambda": lru_lambda, "b_gate": b_gate, "w_branch": w_branch, "w_out": w_out,
            "g_ffn": g_ffn, "w_ffn_in": w_ffn_in, "w_ffn_out": w_ffn_out}


def reference(x, g_mix, w_in, qk_gain_a, rel_bias, qk_gain_b, g_idx_k, conv_w, conv_b,
              lru_wa, lru_ba, lru_wx, lru_bx, lru_lambda, b_gate, w_branch, w_out,
              g_ffn, w_ffn_in, w_ffn_out):
    b, s, _ = x.shape
    cos, sin = rope_tables(s)
    q_in_band = A_LEFT_CHUNKS * CHUNK + jnp.arange(CHUNK)
    rel_idx = jnp.clip(q_in_band[:, None] - jnp.arange(A_BAND)[None, :], -REL_CLIP, REL_CLIP) + REL_CLIP
    for l in range(DEPTH):
        h = rms_norm(x, g_mix[l])
        z = h @ w_in[l]
        aq, ak, av, bq, bk, bv, iq, ik, iw, cx, cy, gt = split_cols(z)
        hd = lambda t, n: t.reshape(b, s, n, HEAD_DIM)
        y_a = chunk_relpos_attention(rms_norm(hd(aq, A_HEADS), qk_gain_a[l, 0]),
                                     rms_norm(hd(ak, A_HEADS), qk_gain_a[l, 1]),
                                     hd(av, A_HEADS), rel_bias[l], rel_idx)
        q_b = apply_rope(rms_norm(hd(bq, B_HEADS), qk_gain_b[l, 0]), cos, sin)
        k_b = apply_rope(rms_norm(hd(bk, B_HEADS), qk_gain_b[l, 1]), cos, sin)
        q_i = apply_rope(iq.reshape(b, s, IDX_HEADS, IDX_DIM), cos, sin)
        k_i = apply_rope(rms_norm(ik, g_idx_k[l]), cos, sin)
        w_i = iw * (IDX_HEADS ** -0.5 * IDX_DIM ** -0.5)
        y_b = indexer_sparse_attention(q_b, k_b, hd(bv, B_HEADS), q_i, k_i, w_i)
        y_c = rglru_branch(cx, cy, conv_w[l], conv_b[l], lru_wa[l], lru_ba[l],
                           lru_wx[l], lru_bx[l], lru_lambda[l])
        gates = jax.nn.sigmoid(gt + b_gate[l]).reshape(b, s, N_BRANCH, D_MODEL)
        wb = w_branch[l]
        merged = (gates[:, :, 0] * (y_a @ wb[:A_WIDTH])
                  + gates[:, :, 1] * (y_b @ wb[A_WIDTH:A_WIDTH + B_WIDTH])
                  + gates[:, :, 2] * (y_c @ wb[A_WIDTH + B_WIDTH:]))
        x = x + merged @ w_out[l]
        gu = rms_norm(x, g_ffn[l]) @ w_ffn_in[l]
        g_, u_ = jnp.split(gu, 2, axis=-1)
        x = x + (jax.nn.silu(g_) * u_) @ w_ffn_out[l]
    return x
```

```python
import functools
import math

import jax
import jax.numpy as jnp
from jax import lax
from jax.experimental import pallas as pl
from jax.experimental.pallas import tpu as pltpu

F32 = jnp.float32
BF16 = jnp.bfloat16
I32 = jnp.int32

CHUNK = 64
HEAD_DIM = 64
EPS = 1e-6
ROPE_THETA = 10000.0
A_HEADS = 4
A_LEFT_CHUNKS = 8
REL_CLIP = 128
B_HEADS = 4
IDX_HEADS = 4
IDX_DIM = 64
TOPK_MAX = 256
C_WIDTH = 512
C_BLOCKS = 8
C_CONV = 4
LRU_C = 8.0
N_BRANCH = 3
A_WIDTH = A_HEADS * HEAD_DIM
B_WIDTH = B_HEADS * HEAD_DIM

LANES = 128
SUBLANES = 8
Q_TILE = 128
A_KEY_TILES = A_LEFT_CHUNKS * CHUNK // Q_TILE + 1
B_KEY_TILE = 512
INT_MIN = -2 ** 31
NO_TIE_LIMIT = 2 ** 30
VMEM_LIMIT = 56 * 1024 * 1024

NT_DIMS = (((1,), (1,)), ((), ()))


def _cparams(*sem):
    return pltpu.CompilerParams(dimension_semantics=sem, vmem_limit_bytes=VMEM_LIMIT)


def _const_spec(shape):
    nd = len(shape)
    return pl.BlockSpec(shape, lambda *_: (0,) * nd, pipeline_mode=pl.Buffered(1))


def _rms(x, gain):
    ms = jnp.mean(x * x, axis=-1, keepdims=True)
    return x * lax.rsqrt(ms + EPS) * gain


def _proj_kernel(x_ref, g_ref, wrm_ref, wt_ref, bd_ref, gka_ref, gkb_ref, gki_ref, gqaT_ref, gqbT_ref,
                 cos_ref, sina_ref, sinb_ref, cosT_ref, sinT_ref,
                 ka_ref, kb_ref, ki_ref, cx_ref, cy_ref,
                 qaT_ref, vaT_ref, qbT_ref, vbT_ref, qiT_ref, iwT_ref):
    tm = x_ref.shape[1]
    h = _rms(x_ref[0], g_ref[...]).astype(BF16)

    def rm(c0, c1):
        return jnp.dot(h, wrm_ref[:, c0:c1], preferred_element_type=F32)

    def head_rms_rm(z, gain):
        sq = z * z
        hi = sq.astype(BF16)
        lo = (sq - hi.astype(F32)).astype(BF16)
        ssq = (jnp.dot(hi, bd_ref[...], preferred_element_type=F32)
               + jnp.dot(lo, bd_ref[...], preferred_element_type=F32))
        return z * lax.rsqrt(ssq * (1.0 / HEAD_DIM) + EPS) * gain

    def rope_rm(z):
        w = z.shape[1]
        zm = pltpu.roll(z, w - HEAD_DIM // 2, axis=1)
        zp = pltpu.roll(z, HEAD_DIM // 2, axis=1)
        return z * cos_ref[:, :w] + zm * sina_ref[:, :w] + zp * sinb_ref[:, :w]

    ka_ref[0] = head_rms_rm(rm(0, 256), gka_ref[...]).astype(BF16)
    kb_ref[0] = rope_rm(head_rms_rm(rm(256, 512), gkb_ref[...])).astype(BF16)
    zi = rm(512, 640)
    ms = jnp.sum(zi * zi, axis=-1, keepdims=True) * (1.0 / IDX_DIM)
    ki_ref[0] = rope_rm(zi * lax.rsqrt(ms + EPS) * gki_ref[...]).astype(BF16)
    cx_ref[0] = rm(640, 1152)
    cy_ref[0] = rm(1152, 1664)

    def tr(r0, r1):
        return lax.dot_general(wt_ref[r0:r1, :], h, NT_DIMS, preferred_element_type=F32)

    def head_rms_t(z, gain):
        z4 = z.reshape(4, HEAD_DIM, tm)
        ms4 = jnp.mean(z4 * z4, axis=1, keepdims=True)
        return (z4 * lax.rsqrt(ms4 + EPS)).reshape(4 * HEAD_DIM, tm) * gain

    def rope_t(z):
        z4 = z.reshape(4, HEAD_DIM, tm)
        rot = jnp.concatenate([-z4[:, HEAD_DIM // 2:, :], z4[:, :HEAD_DIM // 2, :]], axis=1)
        return (z4 * cosT_ref[...][None] + rot * sinT_ref[...][None]).reshape(4 * HEAD_DIM, tm)

    scale = HEAD_DIM ** -0.5
    qaT_ref[0] = (head_rms_t(tr(0, 256), gqaT_ref[...]) * scale).astype(BF16)
    vaT_ref[0] = tr(256, 512).astype(BF16)
    qbT_ref[0] = (rope_t(head_rms_t(tr(512, 768), gqbT_ref[...])) * scale).astype(BF16)
    vbT_ref[0] = tr(768, 1024).astype(BF16)
    qiT_ref[0] = rope_t(tr(1024, 1280)).astype(BF16)
    iwT_ref[0] = tr(1280, 1296)[0:SUBLANES] * (IDX_HEADS ** -0.5 * IDX_DIM ** -0.5)


def _project(x, g, wrm, wt, bd, gka, gkb, gki, gqaT, gqbT, cos, sina, sinb, cosT, sinT, tm):
    b, s, d = x.shape
    grid = (b, s // tm)
    row = lambda w: pl.BlockSpec((1, tm, w), lambda bi, si: (bi, si, 0))
    col = lambda r: pl.BlockSpec((1, r, tm), lambda bi, si: (bi, 0, si))
    tab_rm = pl.BlockSpec((tm, 256), lambda bi, si: (si, 0))
    tab_t = pl.BlockSpec((HEAD_DIM, tm), lambda bi, si: (0, si))
    in_specs = [row(d), _const_spec(g.shape), _const_spec(wrm.shape), _const_spec(wt.shape), _const_spec(bd.shape),
                _const_spec(gka.shape), _const_spec(gkb.shape), _const_spec(gki.shape),
                _const_spec(gqaT.shape), _const_spec(gqbT.shape),
                tab_rm, tab_rm, tab_rm, tab_t, tab_t]
    out_shape = (jax.ShapeDtypeStruct((b, s, 256), BF16), jax.ShapeDtypeStruct((b, s, 256), BF16),
                 jax.ShapeDtypeStruct((b, s, 128), BF16),
                 jax.ShapeDtypeStruct((b, s, C_WIDTH), F32), jax.ShapeDtypeStruct((b, s, C_WIDTH), F32),
                 jax.ShapeDtypeStruct((b, 256, s), BF16), jax.ShapeDtypeStruct((b, 256, s), BF16),
                 jax.ShapeDtypeStruct((b, 256, s), BF16), jax.ShapeDtypeStruct((b, 256, s), BF16),
                 jax.ShapeDtypeStruct((b, 256, s), BF16), jax.ShapeDtypeStruct((b, 8, s), F32))
    out_specs = (row(256), row(256), row(128), row(C_WIDTH), row(C_WIDTH),
                 col(256), col(256), col(256), col(256), col(256), col(8))
    return pl.pallas_call(_proj_kernel, grid=grid, in_specs=in_specs, out_specs=out_specs, out_shape=out_shape,
                          compiler_params=_cparams("parallel", "parallel"), name="proj")(
        x, g, wrm, wt, bd, gka, gkb, gki, gqaT, gqbT, cos, sina, sinb, cosT, sinT)


def _block_diag_q(qT):
    rows = lax.broadcasted_iota(I32, qT.shape, 0) // HEAD_DIM
    zero = jnp.zeros_like(qT)
    return jnp.concatenate([jnp.where(rows == hh, qT, zero) for hh in range(4)], axis=1)


def _chunk_attn_kernel(qT_ref, k_ref, vT_ref, bias_ref, o_ref):
    i = pl.program_id(1)
    qbd = _block_diag_q(qT_ref[0])
    ks, vs, pens = [], [], []
    for t in range(A_KEY_TILES):
        kt = i - (A_KEY_TILES - 1) + t
        r0 = pl.multiple_of(jnp.maximum(kt, 0) * Q_TILE, Q_TILE)
        ks.append(k_ref[0, pl.ds(r0, Q_TILE), :])
        vs.append(vT_ref[0, :, pl.ds(r0, Q_TILE)])
        pens.append(jnp.where(kt >= 0, 0.0, -jnp.inf).astype(F32))
    kwin = jnp.concatenate(ks, axis=0)
    vwin = jnp.concatenate(vs, axis=1)
    s = jnp.dot(kwin, qbd, preferred_element_type=F32) + bias_ref[...]
    s = jnp.concatenate([s[t * Q_TILE:(t + 1) * Q_TILE] + pens[t] for t in range(A_KEY_TILES)], axis=0)
    m = jnp.max(s, axis=0, keepdims=True)
    p = jnp.exp(s - m)
    l = jnp.sum(p, axis=0, keepdims=True)
    pb = p.astype(BF16)
    outs = []
    for hh in range(A_HEADS):
        lo, hi = hh * Q_TILE, (hh + 1) * Q_TILE
        o = jnp.dot(vwin[hh * HEAD_DIM:(hh + 1) * HEAD_DIM, :], pb[:, lo:hi], preferred_element_type=F32)
        outs.append(o / l[:, lo:hi])
    o_ref[0] = jnp.concatenate(outs, axis=0).T.astype(BF16)


def _chunk_attention(qaT, ka, vaT, biasT):
    b, _, s = qaT.shape
    grid = (b, s // Q_TILE)
    return pl.pallas_call(
        _chunk_attn_kernel, grid=grid,
        in_specs=[pl.BlockSpec((1, 256, Q_TILE), lambda bi, i: (bi, 0, i)),
                  pl.BlockSpec((1, s, 256), lambda bi, i: (bi, 0, 0)),
                  pl.BlockSpec((1, 256, s), lambda bi, i: (bi, 0, 0)),
                  _const_spec(biasT.shape)],
        out_specs=pl.BlockSpec((1, Q_TILE, 256), lambda bi, i: (bi, i, 0)),
        out_shape=jax.ShapeDtypeStruct((b, s, 256), BF16),
        compiler_params=_cparams("parallel", "parallel"), name="chunk_attn")(qaT, ka, vaT, biasT)


def _dsa_kernel(qiT_ref, iwT_ref, qbT_ref, ki_ref, kb_ref, vbT_ref, o_ref, keys_ref, tie_ref, *, topk, idx_bits):
    tk = B_KEY_TILE
    i = pl.program_id(1)
    nt = ((i + 1) * Q_TILE + tk - 1) // tk
    qiT = qiT_ref[0]
    qi = jnp.concatenate([qiT[hh * IDX_DIM:(hh + 1) * IDX_DIM] for hh in range(IDX_HEADS)], axis=1)
    qi = jnp.concatenate([qi, jnp.zeros_like(qi)], axis=0)
    w = iwT_ref[0]
    lane = lax.broadcasted_iota(I32, (1, Q_TILE), 1)
    key_limit = i * Q_TILE + CHUNK + jnp.where(lane >= CHUNK, CHUNK, 0)
    row_iota = lax.broadcasted_iota(I32, (tk, Q_TILE), 0)

    def tile_start(t):
        return pl.multiple_of(t * tk, tk)

    def score_tile(t, c):
        r0 = tile_start(t)
        dots = jnp.dot(ki_ref[0, pl.ds(r0, tk), :], qi, preferred_element_type=F32)
        sc = jnp.maximum(dots[:, 0:Q_TILE], 0.0) * w[0:1, :]
        for hh in range(1, IDX_HEADS):
            sc = sc + jnp.maximum(dots[:, hh * Q_TILE:(hh + 1) * Q_TILE], 0.0) * w[hh:hh + 1, :]
        sc = jnp.where(sc == 0.0, 0.0, sc)
        bits = lax.bitcast_convert_type(sc, I32)
        key = jnp.where(bits < 0, bits ^ jnp.int32(0x7FFFFFFF), bits)
        keys_ref[pl.ds(r0, tk), :] = jnp.where(r0 + row_iota < key_limit, key, jnp.int32(INT_MIN))
        return c

    lax.fori_loop(0, nt, score_tile, 0)

    def count(pred):
        def body(t, acc):
            r0 = tile_start(t)
            hit = jnp.where(pred(keys_ref[pl.ds(r0, tk), :], r0 + row_iota), 1, 0)
            return acc + jnp.sum(hit.reshape(tk // SUBLANES, SUBLANES, Q_TILE), axis=0)
        acc = lax.fori_loop(0, nt, body, jnp.zeros((SUBLANES, Q_TILE), I32))
        return jnp.sum(acc, axis=0, keepdims=True)

    zero = jnp.zeros((1, Q_TILE), I32)
    thr = jnp.where(count(lambda k, p: k >= zero) >= topk, zero, jnp.int32(INT_MIN))

    def thr_bit(it, thr):
        cand = thr + (jnp.int32(1) << (30 - it))
        return jnp.where(count(lambda k, p: k >= cand) >= topk, cand, thr)

    thr = lax.fori_loop(0, 31, thr_bit, thr)

    n_gt = count(lambda k, p: k > thr)
    n_eq = count(lambda k, p: k == thr)
    need = topk - n_gt
    short = thr == jnp.int32(INT_MIN)
    excess = jnp.logical_and(n_eq > need, jnp.logical_not(short))
    tie0 = jnp.where(short, -1, NO_TIE_LIMIT).astype(I32)
    tie_ref[...] = jnp.broadcast_to(tie0, tie_ref.shape)

    @pl.when(jnp.max(excess.astype(I32)) > 0)
    def _():
        def u_bit(it, u):
            cand = u + (jnp.int32(1) << (idx_bits - 1 - it))
            c = count(lambda k, p: jnp.logical_and(k == thr, p < cand))
            return jnp.where(c < need, cand, u)
        u = lax.fori_loop(0, idx_bits, u_bit, zero)
        tie_ref[...] = jnp.broadcast_to(jnp.where(excess, u, tie0), tie_ref.shape)

    tie = tie_ref[0:1, :]

    qbd = _block_diag_q(qbT_ref[0])

    def attn_tile(t, carry):
        ms, ls, accs = carry
        r0 = tile_start(t)
        key = keys_ref[pl.ds(r0, tk), :]
        pos = r0 + row_iota
        sel = jnp.logical_or(key > thr, jnp.logical_and(key == thr, pos <= tie))
        s_all = jnp.dot(kb_ref[0, pl.ds(r0, tk), :], qbd, preferred_element_type=F32)
        new_ms, new_ls, new_accs = [], [], []
        for hh in range(B_HEADS):
            s = jnp.where(sel, s_all[:, hh * Q_TILE:(hh + 1) * Q_TILE], -jnp.inf)
            m_new = jnp.maximum(ms[hh], jnp.max(s, axis=0, keepdims=True))
            alpha = jnp.exp(ms[hh] - m_new)
            p = jnp.exp(s - m_new)
            new_ls.append(alpha * ls[hh] + jnp.sum(p, axis=0, keepdims=True))
            pv = jnp.dot(vbT_ref[0, hh * HEAD_DIM:(hh + 1) * HEAD_DIM, pl.ds(r0, tk)], p.astype(BF16),
                         preferred_element_type=F32)
            new_accs.append(alpha * accs[hh] + pv)
            new_ms.append(m_new)
        return tuple(new_ms), tuple(new_ls), tuple(new_accs)

    init = (tuple(jnp.full((1, Q_TILE), -1e30, F32) for _ in range(B_HEADS)),
            tuple(jnp.zeros((1, Q_TILE), F32) for _ in range(B_HEADS)),
            tuple(jnp.zeros((HEAD_DIM, Q_TILE), F32) for _ in range(B_HEADS)))
    _, ls, accs = lax.fori_loop(0, nt, attn_tile, init)
    out = jnp.concatenate([accs[hh] / ls[hh] for hh in range(B_HEADS)], axis=0)
    o_ref[0] = out.T.astype(BF16)


def _sparse_attention(qiT, iwT, qbT, ki, kb, vbT):
    b, _, s = qbT.shape
    s_pad = -(-s // B_KEY_TILE) * B_KEY_TILE
    topk = min(TOPK_MAX, s // 4)
    idx_bits = int(s).bit_length()
    grid = (b, s // Q_TILE)
    qspec = lambda r: pl.BlockSpec((1, r, Q_TILE), lambda bi, i: (bi, 0, i))
    kernel = functools.partial(_dsa_kernel, topk=topk, idx_bits=idx_bits)
    return pl.pallas_call(
        kernel, grid=grid,
        in_specs=[qspec(256), qspec(8), qspec(256),
                  pl.BlockSpec((1, s, 128), lambda bi, i: (bi, 0, 0)),
                  pl.BlockSpec((1, s, 256), lambda bi, i: (bi, 0, 0)),
                  pl.BlockSpec((1, 256, s), lambda bi, i: (bi, 0, 0))],
        out_specs=pl.BlockSpec((1, Q_TILE, 256), lambda bi, i: (bi, i, 0)),
        out_shape=jax.ShapeDtypeStruct((b, s, 256), BF16),
        scratch_shapes=[pltpu.VMEM((s_pad, Q_TILE), I32), pltpu.VMEM((SUBLANES, Q_TILE), I32)],
        compiler_params=_cparams("parallel", "parallel"), name="sparse_attn")(qiT, iwT, qbT, ki, kb, vbT)


def _rglru_kernel(cx_ref, cy_ref, cw_ref, cb_ref, wax_ref, ba_ref, bx_ref, lam_ref, o_ref,
                  tail_ref, h_ref, a_ref, g_ref):
    ts = cx_ref.shape[1]

    @pl.when(pl.program_id(1) == 0)
    def _():
        tail_ref[...] = jnp.zeros_like(tail_ref)
        h_ref[...] = jnp.zeros_like(h_ref)

    x = cx_ref[0]
    tail = tail_ref[...]
    row8 = lax.broadcasted_iota(I32, (SUBLANES, C_WIDTH), 0)
    u = cb_ref[...] + x * cw_ref[C_CONV - 1:C_CONV, :]
    for k in range(1, C_CONV):
        xr = pltpu.roll(x, k, axis=0)
        head = jnp.where(row8 < k, pltpu.roll(tail, k, axis=0), xr[0:SUBLANES])
        xk = jnp.concatenate([head, xr[SUBLANES:]], axis=0)
        u = u + xk * cw_ref[C_CONV - 1 - k:C_CONV - k, :]
    tail_ref[...] = x[ts - SUBLANES:ts]

    gates = jnp.dot(u.astype(BF16), wax_ref[...], preferred_element_type=F32)
    r = jax.nn.sigmoid(gates[:, :C_WIDTH] + ba_ref[...])
    ig = jax.nn.sigmoid(gates[:, C_WIDTH:] + bx_ref[...])
    nlam = -lam_ref[...]
    softplus = jnp.maximum(nlam, 0.0) + jnp.log1p(jnp.exp(-jnp.abs(nlam)))
    a = jnp.exp(-LRU_C * r * softplus)
    a_ref[...] = a
    g_ref[...] = jnp.sqrt(1.0 - a * a) * (ig * u)

    def group(gi, hprev):
        r0 = pl.multiple_of(gi * SUBLANES, SUBLANES)
        av = a_ref[pl.ds(r0, SUBLANES), :]
        bv = g_ref[pl.ds(r0, SUBLANES), :]
        for d in (1, 2, 4):
            keep = row8 >= d
            a_sh = jnp.where(keep, pltpu.roll(av, d, axis=0), 1.0)
            b_sh = jnp.where(keep, pltpu.roll(bv, d, axis=0), 0.0)
            bv = av * b_sh + bv
            av = av * a_sh
        hs = av * hprev + bv
        g_ref[pl.ds(r0, SUBLANES), :] = hs
        return jnp.broadcast_to(hs[SUBLANES - 1:SUBLANES, :], hs.shape)

    h_ref[...] = lax.fori_loop(0, ts // SUBLANES, group, h_ref[...], unroll=2)
    o_ref[0] = (g_ref[...] * jax.nn.gelu(cy_ref[0])).astype(BF16)


def _rglru(cx, cy, cw, cb, wax, ba, bx, lam, ts):
    b, s, c = cx.shape
    blk = pl.BlockSpec((1, ts, c), lambda bi, si: (bi, si, 0))
    return pl.pallas_call(
        _rglru_kernel, grid=(b, s // ts),
        in_specs=[blk, blk, _const_spec(cw.shape), _const_spec(cb.shape), _const_spec(wax.shape),
                  _const_spec(ba.shape), _const_spec(bx.shape), _const_spec(lam.shape)],
        out_specs=blk, out_shape=jax.ShapeDtypeStruct((b, s, c), BF16),
        scratch_shapes=[pltpu.VMEM((SUBLANES, c), F32), pltpu.VMEM((SUBLANES, c), F32),
                        pltpu.VMEM((ts, c), F32), pltpu.VMEM((ts, c), F32)],
        compiler_params=_cparams("parallel", "arbitrary"), name="rglru")(cx, cy, cw, cb, wax, ba, bx, lam)


def _merge_kernel(x_ref, ya_ref, yb_ref, yc_ref, g_ref, wg_ref, bg_ref, wb_ref, wo_ref, o_ref):
    x = x_ref[...]
    d = x.shape[1]
    h = _rms(x, g_ref[...]).astype(BF16)
    ys = (ya_ref[...], yb_ref[...], yc_ref[...])
    offs = (0, A_WIDTH, A_WIDTH + B_WIDTH, A_WIDTH + B_WIDTH + C_WIDTH)
    merged = None
    for j in range(N_BRANCH):
        gate = jax.nn.sigmoid(jnp.dot(h, wg_ref[:, j * d:(j + 1) * d], preferred_element_type=F32)
                              + bg_ref[:, j * d:(j + 1) * d])
        term = gate * jnp.dot(ys[j], wb_ref[offs[j]:offs[j + 1], :], preferred_element_type=F32)
        merged = term if merged is None else merged + term
    o_ref[...] = x + jnp.dot(merged.astype(BF16), wo_ref[...], preferred_element_type=F32)


def _merge(x2, ya, yb, yc, g, wg, bg, wb, wo, tm):
    n, d = x2.shape
    row = lambda w: pl.BlockSpec((tm, w), lambda i: (i, 0))
    return pl.pallas_call(
        _merge_kernel, grid=(n // tm,),
        in_specs=[row(d), row(A_WIDTH), row(B_WIDTH), row(C_WIDTH), _const_spec(g.shape), _const_spec(wg.shape),
                  _const_spec(bg.shape), _const_spec(wb.shape), _const_spec(wo.shape)],
        out_specs=row(d), out_shape=jax.ShapeDtypeStruct((n, d), F32),
        compiler_params=_cparams("parallel"), name="merge")(x2, ya, yb, yc, g, wg, bg, wb, wo)


def _ffn_kernel(x_ref, g_ref, wi_ref, wo_ref, o_ref, *, d_ff, chunk):
    x = x_ref[...]
    h = _rms(x, g_ref[...]).astype(BF16)
    acc = x
    for c0 in range(0, d_ff, chunk):
        c1 = min(c0 + chunk, d_ff)
        gp = jnp.dot(h, wi_ref[:, c0:c1], preferred_element_type=F32)
        up = jnp.dot(h, wi_ref[:, d_ff + c0:d_ff + c1], preferred_element_type=F32)
        act = (jax.nn.silu(gp) * up).astype(BF16)
        acc = acc + jnp.dot(act, wo_ref[c0:c1, :], preferred_element_type=F32)
    o_ref[...] = acc


def _ffn(x2, g, wi, wo, tm):
    n, d = x2.shape
    d_ff = wo.shape[0]
    row = pl.BlockSpec((tm, d), lambda i: (i, 0))
    kernel = functools.partial(_ffn_kernel, d_ff=d_ff, chunk=1024)
    return pl.pallas_call(
        kernel, grid=(n // tm,),
        in_specs=[row, _const_spec(g.shape), _const_spec(wi.shape), _const_spec(wo.shape)],
        out_specs=row, out_shape=jax.ShapeDtypeStruct((n, d), F32),
        compiler_params=_cparams("parallel"), name="ffn")(x2, g, wi, wo)


def _rope_tables(s):
    pos = jnp.arange(s, dtype=F32)
    inv = 1.0 / (ROPE_THETA ** (jnp.arange(0, HEAD_DIM, 2, dtype=F32) / HEAD_DIM))
    ang = pos[:, None] * inv[None, :]
    ang = jnp.concatenate([ang, ang], axis=-1)
    return jnp.cos(ang), jnp.sin(ang)


def _rel_bias_table(rel_bias_l):
    nk = A_KEY_TILES * Q_TILE
    j = jnp.arange(nk)[:, None]
    q = jnp.arange(Q_TILE)[None, :]
    rel = jnp.clip((nk - Q_TILE) + q - j, -REL_CLIP, REL_CLIP) + REL_CLIP
    dchunk = A_LEFT_CHUNKS + q // CHUNK - j // CHUNK
    valid = jnp.logical_and(dchunk >= 0, dchunk <= A_LEFT_CHUNKS)
    tab = jnp.where(valid[None], rel_bias_l.astype(F32)[:, rel], -jnp.inf)
    return jnp.transpose(tab, (1, 0, 2)).reshape(nk, A_HEADS * Q_TILE)


def kernel(x, g_mix, w_in, qk_gain_a, rel_bias, qk_gain_b, g_idx_k, conv_w, conv_b, lru_wa, lru_ba, lru_wx,
           lru_bx, lru_lambda, b_gate, w_branch, w_out, g_ffn, w_ffn_in, w_ffn_out):
    b, s, d = x.shape
    depth = g_mix.shape[0]
    assert s % B_KEY_TILE == 0 and d % LANES == 0
    tm = 512
    cos, sin = _rope_tables(s)
    half = (jnp.arange(HEAD_DIM) < HEAD_DIM // 2).astype(F32)
    cos_rm = jnp.tile(cos, (1, 4))
    sina_rm = jnp.tile(-sin * half[None, :], (1, 4))
    sinb_rm = jnp.tile(sin * (1.0 - half)[None, :], (1, 4))
    cosT, sinT = cos.T, sin.T
    hid = jnp.arange(256) // HEAD_DIM
    bd = (hid[:, None] == hid[None, :]).astype(BF16)
    row1 = lambda v: v.reshape(1, -1).astype(F32)
    o_aq, o_ak, o_av, o_bq, o_bk, o_bv = 0, 256, 512, 768, 1024, 1280
    o_iq, o_ik, o_iw, o_cx, o_cy, o_gt = 1536, 1792, 1856, 1860, 2372, 2884
    blocks = jnp.arange(C_WIDTH) // (C_WIDTH // C_BLOCKS)
    bmask = (blocks[:, None] == blocks[None, :])

    def dense_blocks(wblk):
        return jnp.where(bmask, jnp.tile(wblk.reshape(C_WIDTH, C_WIDTH // C_BLOCKS), (1, C_BLOCKS)), 0.0)

    for l in range(depth):
        w = w_in[l]
        zpad = jnp.zeros((d, 64), F32)
        wrm = jnp.concatenate([w[:, o_ak:o_ak + 256], w[:, o_bk:o_bk + 256], w[:, o_ik:o_ik + 64], zpad,
                               w[:, o_cx:o_cx + 512], w[:, o_cy:o_cy + 512]], axis=1).astype(BF16)
        wt = jnp.concatenate([w[:, o_aq:o_aq + 256], w[:, o_av:o_av + 256], w[:, o_bq:o_bq + 256],
                              w[:, o_bv:o_bv + 256], w[:, o_iq:o_iq + 256], w[:, o_iw:o_iw + 4],
                              jnp.zeros((d, 12), F32)], axis=1).T.astype(BF16)
        gki = jnp.concatenate([g_idx_k[l], jnp.zeros((64,), F32)]).reshape(1, 128)
        ka, kb, ki, cx, cy, qaT, vaT, qbT, vbT, qiT, iwT = _project(
            x, row1(g_mix[l]), wrm, wt, bd,
            row1(jnp.tile(qk_gain_a[l, 1], 4)), row1(jnp.tile(qk_gain_b[l, 1], 4)), gki,
            jnp.tile(qk_gain_a[l, 0], 4).reshape(256, 1), jnp.tile(qk_gain_b[l, 0], 4).reshape(256, 1),
            cos_rm, sina_rm, sinb_rm, cosT, sinT, tm)
        ya = _chunk_attention(qaT, ka, vaT, _rel_bias_table(rel_bias[l]))
        yb = _sparse_attention(qiT, iwT, qbT, ki, kb, vbT)
        wax = jnp.concatenate([dense_blocks(lru_wa[l]), dense_blocks(lru_wx[l])], axis=1).astype(BF16)
        yc = _rglru(cx, cy, conv_w[l], row1(conv_b[l]), wax, row1(lru_ba[l]), row1(lru_bx[l]),
                    row1(lru_lambda[l]), 512)
        n = b * s
        x1 = _merge(x.reshape(n, d), ya.reshape(n, A_WIDTH), yb.reshape(n, B_WIDTH), yc.reshape(n, C_WIDTH),
                    row1(g_mix[l]), w[:, o_gt:].astype(BF16), row1(b_gate[l]), w_branch[l].astype(BF16),
                    w_out[l].astype(BF16), tm)
        x = _ffn(x1, row1(g_ffn[l]), w_ffn_in[l].astype(BF16), w_ffn_out[l].astype(BF16), tm).reshape(b, s, d)
    return x
```

```python
import functools
import math

import jax
import jax.numpy as jnp
from jax import lax
from jax.experimental import pallas as pl
from jax.experimental.pallas import tpu as pltpu

F32 = jnp.float32
BF16 = jnp.bfloat16
I32 = jnp.int32

CHUNK = 64
HEAD_DIM = 64
EPS = 1e-6
ROPE_THETA = 10000.0
A_HEADS = 4
A_LEFT_CHUNKS = 8
REL_CLIP = 128
B_HEADS = 4
IDX_HEADS = 4
IDX_DIM = 64
TOPK_MAX = 256
C_WIDTH = 512
C_BLOCKS = 8
C_CONV = 4
LRU_C = 8.0
N_BRANCH = 3
A_WIDTH = A_HEADS * HEAD_DIM
B_WIDTH = B_HEADS * HEAD_DIM

LANES = 128
SUBLANES = 8
Q_TILE = 128
A_KEY_TILES = A_LEFT_CHUNKS * CHUNK // Q_TILE + 1
B_KEY_TILE = 512
INT_MIN = -2 ** 31
VMEM_LIMIT = 56 * 1024 * 1024

NT_DIMS = (((1,), (1,)), ((), ()))


def _cparams(*sem):
    return pltpu.CompilerParams(dimension_semantics=sem, vmem_limit_bytes=VMEM_LIMIT)


def _const_spec(shape):
    nd = len(shape)
    return pl.BlockSpec(shape, lambda *_: (0,) * nd, pipeline_mode=pl.Buffered(1))


def _rms(x, gain):
    ms = jnp.mean(x * x, axis=-1, keepdims=True)
    return x * lax.rsqrt(ms + EPS) * gain


def _proj_kernel(x_ref, g_ref, wrm_ref, wt_ref, bd_ref, gka_ref, gkb_ref, gki_ref, gqaT_ref, gqbT_ref,
                 cos_ref, sina_ref, sinb_ref, cosT_ref, sinT_ref,
                 ka_ref, kb_ref, ki_ref, cx_ref, cy_ref,
                 qaT_ref, vaT_ref, qbT_ref, vbT_ref, qiT_ref, iwT_ref):
    tm = x_ref.shape[1]
    h = _rms(x_ref[0], g_ref[...]).astype(BF16)

    def rm(c0, c1):
        return jnp.dot(h, wrm_ref[:, c0:c1], preferred_element_type=F32)

    def head_rms_rm(z, gain):
        sq = z * z
        hi = sq.astype(BF16)
        lo = (sq - hi.astype(F32)).astype(BF16)
        ssq = (jnp.dot(hi, bd_ref[...], preferred_element_type=F32)
               + jnp.dot(lo, bd_ref[...], preferred_element_type=F32))
        return z * lax.rsqrt(ssq * (1.0 / HEAD_DIM) + EPS) * gain

    def rope_rm(z):
        w = z.shape[1]
        zm = pltpu.roll(z, w - HEAD_DIM // 2, axis=1)
        zp = pltpu.roll(z, HEAD_DIM // 2, axis=1)
        return z * cos_ref[:, :w] + zm * sina_ref[:, :w] + zp * sinb_ref[:, :w]

    ka_ref[0] = head_rms_rm(rm(0, 256), gka_ref[...]).astype(BF16)
    kb_ref[0] = rope_rm(head_rms_rm(rm(256, 512), gkb_ref[...])).astype(BF16)
    zi = rm(512, 640)
    ms = jnp.sum(zi * zi, axis=-1, keepdims=True) * (1.0 / IDX_DIM)
    ki_ref[0] = rope_rm(zi * lax.rsqrt(ms + EPS) * gki_ref[...]).astype(BF16)
    cx_ref[0] = rm(640, 1152)
    cy_ref[0] = rm(1152, 1664)

    def tr(r0, r1):
        return lax.dot_general(wt_ref[r0:r1, :], h, NT_DIMS, preferred_element_type=F32)

    def head_rms_t(z, gain):
        z4 = z.reshape(4, HEAD_DIM, tm)
        ms4 = jnp.mean(z4 * z4, axis=1, keepdims=True)
        return (z4 * lax.rsqrt(ms4 + EPS)).reshape(4 * HEAD_DIM, tm) * gain

    def rope_t(z):
        z4 = z.reshape(4, HEAD_DIM, tm)
        rot = jnp.concatenate([-z4[:, HEAD_DIM // 2:, :], z4[:, :HEAD_DIM // 2, :]], axis=1)
        return (z4 * cosT_ref[...][None] + rot * sinT_ref[...][None]).reshape(4 * HEAD_DIM, tm)

    scale = HEAD_DIM ** -0.5
    qaT_ref[0] = (head_rms_t(tr(0, 256), gqaT_ref[...]) * scale).astype(BF16)
    vaT_ref[0] = tr(256, 512).astype(BF16)
    qbT_ref[0] = (rope_t(head_rms_t(tr(512, 768), gqbT_ref[...])) * scale).astype(BF16)
    vbT_ref[0] = tr(768, 1024).astype(BF16)
    qiT_ref[0] = rope_t(tr(1024, 1280)).astype(BF16)
    iwT_ref[0] = tr(1280, 1296)[0:SUBLANES] * (IDX_HEADS ** -0.5 * IDX_DIM ** -0.5)


def _project(x, g, wrm, wt, bd, gka, gkb, gki, gqaT, gqbT, cos, sina, sinb, cosT, sinT, tm):
    b, s, d = x.shape
    grid = (b, s // tm)
    row = lambda w: pl.BlockSpec((1, tm, w), lambda bi, si: (bi, si, 0))
    col = lambda r: pl.BlockSpec((1, r, tm), lambda bi, si: (bi, 0, si))
    tab_rm = pl.BlockSpec((tm, 256), lambda bi, si: (si, 0))
    tab_t = pl.BlockSpec((HEAD_DIM, tm), lambda bi, si: (0, si))
    in_specs = [row(d), _const_spec(g.shape), _const_spec(wrm.shape), _const_spec(wt.shape), _const_spec(bd.shape),
                _const_spec(gka.shape), _const_spec(gkb.shape), _const_spec(gki.shape),
                _const_spec(gqaT.shape), _const_spec(gqbT.shape),
                tab_rm, tab_rm, tab_rm, tab_t, tab_t]
    out_shape = (jax.ShapeDtypeStruct((b, s, 256), BF16), jax.ShapeDtypeStruct((b, s, 256), BF16),
                 jax.ShapeDtypeStruct((b, s, 128), BF16),
                 jax.ShapeDtypeStruct((b, s, C_WIDTH), F32), jax.ShapeDtypeStruct((b, s, C_WIDTH), F32),
                 jax.ShapeDtypeStruct((b, 256, s), BF16), jax.ShapeDtypeStruct((b, 256, s), BF16),
                 jax.ShapeDtypeStruct((b, 256, s), BF16), jax.ShapeDtypeStruct((b, 256, s), BF16),
                 jax.ShapeDtypeStruct((b, 256, s), BF16), jax.ShapeDtypeStruct((b, 8, s), F32))
    out_specs = (row(256), row(256), row(128), row(C_WIDTH), row(C_WIDTH),
                 col(256), col(256), col(256), col(256), col(256), col(8))
    return pl.pallas_call(_proj_kernel, grid=grid, in_specs=in_specs, out_specs=out_specs, out_shape=out_shape,
                          compiler_params=_cparams("parallel", "parallel"), name="proj")(
        x, g, wrm, wt, bd, gka, gkb, gki, gqaT, gqbT, cos, sina, sinb, cosT, sinT)


def _transpose_cast_kernel(w_ref, o_ref):
    o_ref[...] = w_ref[...].T.astype(BF16)


def _transpose_cast(w):
    d, n = w.shape
    return pl.pallas_call(
        _transpose_cast_kernel, grid=(n // LANES,),
        in_specs=[pl.BlockSpec((d, LANES), lambda i: (0, i))],
        out_specs=pl.BlockSpec((LANES, d), lambda i: (i, 0)),
        out_shape=jax.ShapeDtypeStruct((n, d), BF16),
        compiler_params=_cparams("parallel"), name="transpose_cast")(w)


def _block_diag_q(qT):
    rows = lax.broadcasted_iota(I32, qT.shape, 0) // HEAD_DIM
    zero = jnp.zeros_like(qT)
    return jnp.concatenate([jnp.where(rows == hh, qT, zero) for hh in range(4)], axis=1)


def _chunk_attn_kernel(qT_ref, k_ref, vT_ref, bias_ref, o_ref):
    i = pl.program_id(1)
    qbd = _block_diag_q(qT_ref[0])
    ks, vs, pens = [], [], []
    for t in range(A_KEY_TILES):
        kt = i - (A_KEY_TILES - 1) + t
        r0 = pl.multiple_of(jnp.maximum(kt, 0) * Q_TILE, Q_TILE)
        ks.append(k_ref[0, pl.ds(r0, Q_TILE), :])
        vs.append(vT_ref[0, :, pl.ds(r0, Q_TILE)])
        pens.append(jnp.where(kt >= 0, 0.0, -jnp.inf).astype(F32))
    kwin = jnp.concatenate(ks, axis=0)
    vwin = jnp.concatenate(vs, axis=1)
    s = jnp.dot(kwin, qbd, preferred_element_type=F32) + bias_ref[...]
    s = jnp.concatenate([s[t * Q_TILE:(t + 1) * Q_TILE] + pens[t] for t in range(A_KEY_TILES)], axis=0)
    m = jnp.max(s, axis=0, keepdims=True)
    p = jnp.exp(s - m)
    l = jnp.sum(p, axis=0, keepdims=True)
    pb = p.astype(BF16)
    outs = []
    for hh in range(A_HEADS):
        lo, hi = hh * Q_TILE, (hh + 1) * Q_TILE
        o = jnp.dot(vwin[hh * HEAD_DIM:(hh + 1) * HEAD_DIM, :], pb[:, lo:hi], preferred_element_type=F32)
        outs.append(o / l[:, lo:hi])
    o_ref[0] = jnp.concatenate(outs, axis=0).T.astype(BF16)


def _chunk_attention(qaT, ka, vaT, biasT):
    b, _, s = qaT.shape
    grid = (b, s // Q_TILE)
    return pl.pallas_call(
        _chunk_attn_kernel, grid=grid,
        in_specs=[pl.BlockSpec((1, 256, Q_TILE), lambda bi, i: (bi, 0, i)),
                  pl.BlockSpec((1, s, 256), lambda bi, i: (bi, 0, 0)),
                  pl.BlockSpec((1, 256, s), lambda bi, i: (bi, 0, 0)),
                  _const_spec(biasT.shape)],
        out_specs=pl.BlockSpec((1, Q_TILE, 256), lambda bi, i: (bi, i, 0)),
        out_shape=jax.ShapeDtypeStruct((b, s, 256), BF16),
        compiler_params=_cparams("parallel", "parallel"), name="chunk_attn")(qaT, ka, vaT, biasT)


def _dsa_kernel(qiT_ref, iwT_ref, qbT_ref, ki_ref, kb_ref, vbT_ref, o_ref, keys_ref, *, topk):
    tk = B_KEY_TILE
    i = pl.program_id(1)
    nt = ((i + 1) * Q_TILE + tk - 1) // tk
    qiT = qiT_ref[0]
    qi = jnp.concatenate([qiT[hh * IDX_DIM:(hh + 1) * IDX_DIM] for hh in range(IDX_HEADS)], axis=1)
    qi = jnp.concatenate([qi, jnp.zeros_like(qi)], axis=0)
    w = iwT_ref[0]
    lane = lax.broadcasted_iota(I32, (1, Q_TILE), 1)
    key_limit = i * Q_TILE + CHUNK + jnp.where(lane >= CHUNK, CHUNK, 0)
    row_iota = lax.broadcasted_iota(I32, (tk, Q_TILE), 0)

    def tile_start(t):
        return pl.multiple_of(t * tk, tk)

    def score_tile(t, c):
        r0 = tile_start(t)
        dots = jnp.dot(ki_ref[0, pl.ds(r0, tk), :], qi, preferred_element_type=F32)
        sc = jnp.maximum(dots[:, 0:Q_TILE], 0.0) * w[0:1, :]
        for hh in range(1, IDX_HEADS):
            sc = sc + jnp.maximum(dots[:, hh * Q_TILE:(hh + 1) * Q_TILE], 0.0) * w[hh:hh + 1, :]
        sc = jnp.where(sc == 0.0, 0.0, sc)
        bits = lax.bitcast_convert_type(sc, I32)
        key = jnp.where(bits < 0, bits ^ jnp.int32(0x7FFFFFFF), bits)
        keys_ref[pl.ds(r0, tk), :] = jnp.where(r0 + row_iota < key_limit, key, jnp.int32(INT_MIN))
        return c

    lax.fori_loop(0, nt, score_tile, 0)

    def count(pred):
        def body(t, acc):
            hit = jnp.where(pred(keys_ref[pl.ds(tile_start(t), tk), :]), 1, 0)
            return acc + jnp.sum(hit.reshape(tk // SUBLANES, SUBLANES, Q_TILE), axis=0)
        acc = lax.fori_loop(0, nt, body, jnp.zeros((SUBLANES, Q_TILE), I32))
        return jnp.sum(acc, axis=0, keepdims=True)

    zero = jnp.zeros((1, Q_TILE), I32)
    thr = jnp.where(count(lambda k: k >= zero) >= topk, zero, jnp.int32(INT_MIN))

    def thr_bit(it, thr):
        cand = thr + (jnp.int32(1) << (30 - it))
        return jnp.where(count(lambda k: k >= cand) >= topk, cand, thr)

    thr = lax.fori_loop(0, 31, thr_bit, thr)

    n_gt = count(lambda k: k > thr)
    short = thr == jnp.int32(INT_MIN)
    need = jnp.where(short, 0, topk - n_gt).astype(F32)
    hk = tk // 2
    tril = (lax.broadcasted_iota(I32, (hk, hk), 0) >= lax.broadcasted_iota(I32, (hk, hk), 1)).astype(BF16)

    qbd = _block_diag_q(qbT_ref[0])

    def attn_tile(t, carry):
        ms, ls, accs, seen = carry
        r0 = tile_start(t)
        key = keys_ref[pl.ds(r0, tk), :]
        tied = key == thr
        tied_b = jnp.where(tied, 1.0, 0.0).astype(BF16)
        rank0 = jnp.dot(tril, tied_b[:hk], preferred_element_type=F32) + seen
        rank1 = jnp.dot(tril, tied_b[hk:], preferred_element_type=F32) + rank0[hk - 1:hk, :]
        rank = jnp.concatenate([rank0, rank1], axis=0)
        sel = jnp.logical_or(key > thr, jnp.logical_and(tied, rank <= need))
        s_all = jnp.dot(kb_ref[0, pl.ds(r0, tk), :], qbd, preferred_element_type=F32)
        new_ms, new_ls, new_accs = [], [], []
        for hh in range(B_HEADS):
            s = jnp.where(sel, s_all[:, hh * Q_TILE:(hh + 1) * Q_TILE], -jnp.inf)
            m_new = jnp.maximum(ms[hh], jnp.max(s, axis=0, keepdims=True))
            alpha = jnp.exp(ms[hh] - m_new)
            p = jnp.exp(s - m_new)
            new_ls.append(alpha * ls[hh] + jnp.sum(p, axis=0, keepdims=True))
            pv = jnp.dot(vbT_ref[0, hh * HEAD_DIM:(hh + 1) * HEAD_DIM, pl.ds(r0, tk)], p.astype(BF16),
                         preferred_element_type=F32)
            new_accs.append(alpha * accs[hh] + pv)
            new_ms.append(m_new)
        return tuple(new_ms), tuple(new_ls), tuple(new_accs), rank1[hk - 1:hk, :]

    init = (tuple(jnp.full((1, Q_TILE), -1e30, F32) for _ in range(B_HEADS)),
            tuple(jnp.zeros((1, Q_TILE), F32) for _ in range(B_HEADS)),
            tuple(jnp.zeros((HEAD_DIM, Q_TILE), F32) for _ in range(B_HEADS)),
            jnp.zeros((1, Q_TILE), F32))
    _, ls, accs, _ = lax.fori_loop(0, nt, attn_tile, init)
    out = jnp.concatenate([accs[hh] / ls[hh] for hh in range(B_HEADS)], axis=0)
    o_ref[0] = out.T.astype(BF16)


def _sparse_attention(qiT, iwT, qbT, ki, kb, vbT):
    b, _, s = qbT.shape
    s_pad = -(-s // B_KEY_TILE) * B_KEY_TILE
    topk = min(TOPK_MAX, s // 4)
    grid = (b, s // Q_TILE)
    qspec = lambda r: pl.BlockSpec((1, r, Q_TILE), lambda bi, i: (bi, 0, i))
    kernel = functools.partial(_dsa_kernel, topk=topk)
    return pl.pallas_call(
        kernel, grid=grid,
        in_specs=[qspec(256), qspec(8), qspec(256),
                  pl.BlockSpec((1, s, 128), lambda bi, i: (bi, 0, 0)),
                  pl.BlockSpec((1, s, 256), lambda bi, i: (bi, 0, 0)),
                  pl.BlockSpec((1, 256, s), lambda bi, i: (bi, 0, 0))],
        out_specs=pl.BlockSpec((1, Q_TILE, 256), lambda bi, i: (bi, i, 0)),
        out_shape=jax.ShapeDtypeStruct((b, s, 256), BF16),
        scratch_shapes=[pltpu.VMEM((s_pad, Q_TILE), I32)],
        compiler_params=_cparams("parallel", "parallel"), name="sparse_attn")(qiT, iwT, qbT, ki, kb, vbT)


def _rglru_kernel(cx_ref, cy_ref, cw_ref, cb_ref, wax_ref, ba_ref, bx_ref, lam_ref, o_ref,
                  tail_ref, h_ref, a_ref, g_ref):
    ts = cx_ref.shape[1]

    @pl.when(pl.program_id(1) == 0)
    def _():
        tail_ref[...] = jnp.zeros_like(tail_ref)
        h_ref[...] = jnp.zeros_like(h_ref)

    x = cx_ref[0]
    tail = tail_ref[...]
    row8 = lax.broadcasted_iota(I32, (SUBLANES, C_WIDTH), 0)
    u = cb_ref[...] + x * cw_ref[C_CONV - 1:C_CONV, :]
    for k in range(1, C_CONV):
        xr = pltpu.roll(x, k, axis=0)
        head = jnp.where(row8 < k, pltpu.roll(tail, k, axis=0), xr[0:SUBLANES])
        xk = jnp.concatenate([head, xr[SUBLANES:]], axis=0)
        u = u + xk * cw_ref[C_CONV - 1 - k:C_CONV - k, :]
    tail_ref[...] = x[ts - SUBLANES:ts]

    gates = jnp.dot(u.astype(BF16), wax_ref[...], preferred_element_type=F32)
    r = jax.nn.sigmoid(gates[:, :C_WIDTH] + ba_ref[...])
    ig = jax.nn.sigmoid(gates[:, C_WIDTH:] + bx_ref[...])
    nlam = -lam_ref[...]
    softplus = jnp.maximum(nlam, 0.0) + jnp.log1p(jnp.exp(-jnp.abs(nlam)))
    a = jnp.exp(-LRU_C * r * softplus)
    a_ref[...] = a
    g_ref[...] = jnp.sqrt(1.0 - a * a) * (ig * u)

    def group(gi, hprev):
        r0 = pl.multiple_of(gi * SUBLANES, SUBLANES)
        av = a_ref[pl.ds(r0, SUBLANES), :]
        bv = g_ref[pl.ds(r0, SUBLANES), :]
        for d in (1, 2, 4):
            keep = row8 >= d
            a_sh = jnp.where(keep, pltpu.roll(av, d, axis=0), 1.0)
            b_sh = jnp.where(keep, pltpu.roll(bv, d, axis=0), 0.0)
            bv = av * b_sh + bv
            av = av * a_sh
        hs = av * hprev + bv
        g_ref[pl.ds(r0, SUBLANES), :] = hs
        return jnp.broadcast_to(hs[SUBLANES - 1:SUBLANES, :], hs.shape)

    h_ref[...] = lax.fori_loop(0, ts // SUBLANES, group, h_ref[...], unroll=2)
    o_ref[0] = (g_ref[...] * jax.nn.gelu(cy_ref[0])).astype(BF16)


def _rglru(cx, cy, cw, cb, wax, ba, bx, lam, ts):
    b, s, c = cx.shape
    blk = pl.BlockSpec((1, ts, c), lambda bi, si: (bi, si, 0))
    return pl.pallas_call(
        _rglru_kernel, grid=(b, s // ts),
        in_specs=[blk, blk, _const_spec(cw.shape), _const_spec(cb.shape), _const_spec(wax.shape),
                  _const_spec(ba.shape), _const_spec(bx.shape), _const_spec(lam.shape)],
        out_specs=blk, out_shape=jax.ShapeDtypeStruct((b, s, c), BF16),
        scratch_shapes=[pltpu.VMEM((SUBLANES, c), F32), pltpu.VMEM((SUBLANES, c), F32),
                        pltpu.VMEM((ts, c), F32), pltpu.VMEM((ts, c), F32)],
        compiler_params=_cparams("parallel", "arbitrary"), name="rglru")(cx, cy, cw, cb, wax, ba, bx, lam)


def _merge_kernel(x_ref, ya_ref, yb_ref, yc_ref, g_ref, wg_ref, bg_ref, wb_ref, wo_ref, o_ref):
    x = x_ref[...]
    d = x.shape[1]
    h = _rms(x, g_ref[...]).astype(BF16)
    ys = (ya_ref[...], yb_ref[...], yc_ref[...])
    offs = (0, A_WIDTH, A_WIDTH + B_WIDTH, A_WIDTH + B_WIDTH + C_WIDTH)
    merged = None
    for j in range(N_BRANCH):
        gate = jax.nn.sigmoid(jnp.dot(h, wg_ref[:, j * d:(j + 1) * d], preferred_element_type=F32)
                              + bg_ref[:, j * d:(j + 1) * d])
        term = gate * jnp.dot(ys[j], wb_ref[offs[j]:offs[j + 1], :], preferred_element_type=F32)
        merged = term if merged is None else merged + term
    o_ref[...] = x + jnp.dot(merged.astype(BF16), wo_ref[...], preferred_element_type=F32)


def _merge(x2, ya, yb, yc, g, wg, bg, wb, wo, tm):
    n, d = x2.shape
    row = lambda w: pl.BlockSpec((tm, w), lambda i: (i, 0))
    return pl.pallas_call(
        _merge_kernel, grid=(n // tm,),
        in_specs=[row(d), row(A_WIDTH), row(B_WIDTH), row(C_WIDTH), _const_spec(g.shape), _const_spec(wg.shape),
                  _const_spec(bg.shape), _const_spec(wb.shape), _const_spec(wo.shape)],
        out_specs=row(d), out_shape=jax.ShapeDtypeStruct((n, d), F32),
        compiler_params=_cparams("parallel"), name="merge")(x2, ya, yb, yc, g, wg, bg, wb, wo)


def _ffn_kernel(x_ref, g_ref, wi_ref, wo_ref, o_ref, *, d_ff, chunk):
    x = x_ref[...]
    h = _rms(x, g_ref[...]).astype(BF16)
    acc = x
    for c0 in range(0, d_ff, chunk):
        c1 = min(c0 + chunk, d_ff)
        gp = jnp.dot(h, wi_ref[:, c0:c1], preferred_element_type=F32)
        up = jnp.dot(h, wi_ref[:, d_ff + c0:d_ff + c1], preferred_element_type=F32)
        act = (jax.nn.silu(gp) * up).astype(BF16)
        acc = acc + jnp.dot(act, wo_ref[c0:c1, :], preferred_element_type=F32)
    o_ref[...] = acc


def _ffn(x2, g, wi, wo, tm):
    n, d = x2.shape
    d_ff = wo.shape[0]
    row = pl.BlockSpec((tm, d), lambda i: (i, 0))
    kernel = functools.partial(_ffn_kernel, d_ff=d_ff, chunk=1024)
    return pl.pallas_call(
        kernel, grid=(n // tm,),
        in_specs=[row, _const_spec(g.shape), _const_spec(wi.shape), _const_spec(wo.shape)],
        out_specs=row, out_shape=jax.ShapeDtypeStruct((n, d), F32),
        compiler_params=_cparams("parallel"), name="ffn")(x2, g, wi, wo)


def _rope_tables(s):
    pos = jnp.arange(s, dtype=F32)
    inv = 1.0 / (ROPE_THETA ** (jnp.arange(0, HEAD_DIM, 2, dtype=F32) / HEAD_DIM))
    ang = pos[:, None] * inv[None, :]
    ang = jnp.concatenate([ang, ang], axis=-1)
    return jnp.cos(ang), jnp.sin(ang)


def _rel_bias_table(rel_bias_l):
    nk = A_KEY_TILES * Q_TILE
    assert Q_TILE - 1 < REL_CLIP
    rb = rel_bias_l.astype(F32)
    n_vec = nk + Q_TILE
    vec = jnp.concatenate([rb[:, 1:], jnp.tile(rb[:, -1:], (1, n_vec - 2 * REL_CLIP))], axis=1)
    skew = jnp.tile(vec, (1, nk))[:, :nk * (n_vec - 1)].reshape(A_HEADS, nk, n_vec - 1)
    toep = skew[:, :, nk - 1:nk - 1 + Q_TILE]
    j = jnp.arange(nk)[:, None]
    q = jnp.arange(Q_TILE)[None, :]
    dchunk = A_LEFT_CHUNKS + q // CHUNK - j // CHUNK
    valid = jnp.logical_and(dchunk >= 0, dchunk <= A_LEFT_CHUNKS)
    tab = jnp.where(valid[None], toep, -jnp.inf)
    return jnp.transpose(tab, (1, 0, 2)).reshape(nk, A_HEADS * Q_TILE)


def kernel(x, g_mix, w_in, qk_gain_a, rel_bias, qk_gain_b, g_idx_k, conv_w, conv_b, lru_wa, lru_ba, lru_wx,
           lru_bx, lru_lambda, b_gate, w_branch, w_out, g_ffn, w_ffn_in, w_ffn_out):
    b, s, d = x.shape
    depth = g_mix.shape[0]
    assert s % B_KEY_TILE == 0 and d % LANES == 0
    tm = 512
    cos, sin = _rope_tables(s)
    half = (jnp.arange(HEAD_DIM) < HEAD_DIM // 2).astype(F32)
    cos_rm = jnp.tile(cos, (1, 4))
    sina_rm = jnp.tile(-sin * half[None, :], (1, 4))
    sinb_rm = jnp.tile(sin * (1.0 - half)[None, :], (1, 4))
    cosT, sinT = cos.T, sin.T
    hid = jnp.arange(256) // HEAD_DIM
    bd = (hid[:, None] == hid[None, :]).astype(BF16)
    row1 = lambda v: v.reshape(1, -1).astype(F32)
    o_aq, o_ak, o_av, o_bq, o_bk, o_bv = 0, 256, 512, 768, 1024, 1280
    o_iq, o_ik, o_iw, o_cx, o_cy, o_gt = 1536, 1792, 1856, 1860, 2372, 2884
    blocks = jnp.arange(C_WIDTH) // (C_WIDTH // C_BLOCKS)
    bmask = (blocks[:, None] == blocks[None, :])

    def dense_blocks(wblk):
        return jnp.where(bmask, jnp.tile(wblk.reshape(C_WIDTH, C_WIDTH // C_BLOCKS), (1, C_BLOCKS)), 0.0)

    for l in range(depth):
        w = w_in[l]
        zpad = jnp.zeros((d, 64), F32)
        wrm = jnp.concatenate([w[:, o_ak:o_ak + 256], w[:, o_bk:o_bk + 256], w[:, o_ik:o_ik + 64], zpad,
                               w[:, o_cx:o_cx + 512], w[:, o_cy:o_cy + 512]], axis=1).astype(BF16)
        wt = _transpose_cast(jnp.concatenate(
            [w[:, o_aq:o_aq + 256], w[:, o_av:o_av + 256], w[:, o_bq:o_bq + 256], w[:, o_bv:o_bv + 256],
             w[:, o_iq:o_iq + 256], w[:, o_iw:o_iw + 4], jnp.zeros((d, LANES - 4), F32)], axis=1))
        gki = jnp.concatenate([g_idx_k[l], jnp.zeros((64,), F32)]).reshape(1, 128)
        ka, kb, ki, cx, cy, qaT, vaT, qbT, vbT, qiT, iwT = _project(
            x, row1(g_mix[l]), wrm, wt, bd,
            row1(jnp.tile(qk_gain_a[l, 1], 4)), row1(jnp.tile(qk_gain_b[l, 1], 4)), gki,
            jnp.tile(qk_gain_a[l, 0], 4).reshape(256, 1), jnp.tile(qk_gain_b[l, 0], 4).reshape(256, 1),
            cos_rm, sina_rm, sinb_rm, cosT, sinT, tm)
        ya = _chunk_attention(qaT, ka, vaT, _rel_bias_table(rel_bias[l]))
        yb = _sparse_attention(qiT, iwT, qbT, ki, kb, vbT)
        wax = jnp.concatenate([dense_blocks(lru_wa[l]), dense_blocks(lru_wx[l])], axis=1).astype(BF16)
        yc = _rglru(cx, cy, conv_w[l], row1(conv_b[l]), wax, row1(lru_ba[l]), row1(lru_bx[l]),
                    row1(lru_lambda[l]), 512)
        n = b * s
        x1 = _merge(x.reshape(n, d), ya.reshape(n, A_WIDTH), yb.reshape(n, B_WIDTH), yc.reshape(n, C_WIDTH),
                    row1(g_mix[l]), w[:, o_gt:].astype(BF16), row1(b_gate[l]), w_branch[l].astype(BF16),
                    w_out[l].astype(BF16), tm)
        x = _ffn(x1, row1(g_ffn[l]), w_ffn_in[l].astype(BF16), w_ffn_out[l].astype(BF16), tm).reshape(b, s, d)
    return x
```

```python
import functools
import math

import jax
import jax.numpy as jnp
from jax import lax
from jax.experimental import pallas as pl
from jax.experimental.pallas import tpu as pltpu

F32 = jnp.float32
BF16 = jnp.bfloat16
I32 = jnp.int32
I16 = jnp.int16

CHUNK = 64
HEAD_DIM = 64
EPS = 1e-6
ROPE_THETA = 10000.0
A_HEADS = 4
A_LEFT_CHUNKS = 8
REL_CLIP = 128
B_HEADS = 4
IDX_HEADS = 4
IDX_DIM = 64
TOPK_MAX = 256
C_WIDTH = 512
C_BLOCKS = 8
C_CONV = 4
LRU_C = 8.0
N_BRANCH = 3
A_WIDTH = A_HEADS * HEAD_DIM
B_WIDTH = B_HEADS * HEAD_DIM

LANES = 128
SUBLANES = 8
Q_TILE = 128
A_KEY_TILES = A_LEFT_CHUNKS * CHUNK // Q_TILE + 1
B_KEY_TILE = 512
LOG2_E = math.log2(math.e)
INT_MIN = -2 ** 31
I16_MIN = -2 ** 15
PACKED_ROWS = 2 * SUBLANES
VMEM_LIMIT = 56 * 1024 * 1024

NT_DIMS = (((1,), (1,)), ((), ()))


def _cparams(*sem):
    return pltpu.CompilerParams(dimension_semantics=sem, vmem_limit_bytes=VMEM_LIMIT)


def _const_spec(shape):
    nd = len(shape)
    return pl.BlockSpec(shape, lambda *_: (0,) * nd, pipeline_mode=pl.Buffered(1))


def _rms(x, gain):
    ms = jnp.mean(x * x, axis=-1, keepdims=True)
    return x * lax.rsqrt(ms + EPS) * gain


def _proj_kernel(x_ref, g_ref, wrm_ref, wt_ref, bd_ref, gka_ref, gkb_ref, gki_ref, gqaT_ref, gqbT_ref,
                 cos_ref, sina_ref, sinb_ref, cosT_ref, sinT_ref,
                 ka_ref, kb_ref, ki_ref, cx_ref, cy_ref,
                 qaT_ref, vaT_ref, qbT_ref, vbT_ref, qiT_ref, iwT_ref):
    tm = x_ref.shape[1]
    h = _rms(x_ref[0], g_ref[...]).astype(BF16)

    def rm(c0, c1):
        return jnp.dot(h, wrm_ref[:, c0:c1], preferred_element_type=F32)

    def head_rms_rm(z, gain):
        sq = z * z
        hi = sq.astype(BF16)
        lo = (sq - hi.astype(F32)).astype(BF16)
        ssq = (jnp.dot(hi, bd_ref[...], preferred_element_type=F32)
               + jnp.dot(lo, bd_ref[...], preferred_element_type=F32))
        return z * lax.rsqrt(ssq * (1.0 / HEAD_DIM) + EPS) * gain

    def rope_rm(z):
        w = z.shape[1]
        zm = pltpu.roll(z, w - HEAD_DIM // 2, axis=1)
        zp = pltpu.roll(z, HEAD_DIM // 2, axis=1)
        return z * cos_ref[:, :w] + zm * sina_ref[:, :w] + zp * sinb_ref[:, :w]

    ka_ref[0] = head_rms_rm(rm(0, 256), gka_ref[...]).astype(BF16)
    kb_ref[0] = rope_rm(head_rms_rm(rm(256, 512), gkb_ref[...])).astype(BF16)
    zi = rm(512, 640)
    ms = jnp.sum(zi * zi, axis=-1, keepdims=True) * (1.0 / IDX_DIM)
    ki_ref[0] = rope_rm(zi * lax.rsqrt(ms + EPS) * gki_ref[...]).astype(BF16)
    cx_ref[0] = rm(640, 1152)
    cy_ref[0] = rm(1152, 1664)

    def tr(r0, r1):
        return lax.dot_general(wt_ref[r0:r1, :], h, NT_DIMS, preferred_element_type=F32)

    def head_rms_t(z, gain):
        z4 = z.reshape(4, HEAD_DIM, tm)
        ms4 = jnp.mean(z4 * z4, axis=1, keepdims=True)
        return (z4 * lax.rsqrt(ms4 + EPS)).reshape(4 * HEAD_DIM, tm) * gain

    def rope_t(z):
        z4 = z.reshape(4, HEAD_DIM, tm)
        rot = jnp.concatenate([-z4[:, HEAD_DIM // 2:, :], z4[:, :HEAD_DIM // 2, :]], axis=1)
        return (z4 * cosT_ref[...][None] + rot * sinT_ref[...][None]).reshape(4 * HEAD_DIM, tm)

    scale = HEAD_DIM ** -0.5 * LOG2_E
    qaT_ref[0] = (head_rms_t(tr(0, 256), gqaT_ref[...]) * scale).astype(BF16)
    vaT_ref[0] = tr(256, 512).astype(BF16)
    qbT_ref[0] = (rope_t(head_rms_t(tr(512, 768), gqbT_ref[...])) * scale).astype(BF16)
    vbT_ref[0] = tr(768, 1024).astype(BF16)
    qiT_ref[0] = rope_t(tr(1024, 1280)).astype(BF16)
    iwT_ref[0] = tr(1280, 1296)[0:SUBLANES] * (IDX_HEADS ** -0.5 * IDX_DIM ** -0.5)


def _project(x, g, wrm, wt, bd, gka, gkb, gki, gqaT, gqbT, cos, sina, sinb, cosT, sinT, tm):
    b, s, d = x.shape
    grid = (b, s // tm)
    row = lambda w: pl.BlockSpec((1, tm, w), lambda bi, si: (bi, si, 0))
    col = lambda r: pl.BlockSpec((1, r, tm), lambda bi, si: (bi, 0, si))
    tab_rm = pl.BlockSpec((tm, 256), lambda bi, si: (si, 0))
    tab_t = pl.BlockSpec((HEAD_DIM, tm), lambda bi, si: (0, si))
    in_specs = [row(d), _const_spec(g.shape), _const_spec(wrm.shape), _const_spec(wt.shape), _const_spec(bd.shape),
                _const_spec(gka.shape), _const_spec(gkb.shape), _const_spec(gki.shape),
                _const_spec(gqaT.shape), _const_spec(gqbT.shape),
                tab_rm, tab_rm, tab_rm, tab_t, tab_t]
    out_shape = (jax.ShapeDtypeStruct((b, s, 256), BF16), jax.ShapeDtypeStruct((b, s, 256), BF16),
                 jax.ShapeDtypeStruct((b, s, 128), BF16),
                 jax.ShapeDtypeStruct((b, s, C_WIDTH), F32), jax.ShapeDtypeStruct((b, s, C_WIDTH), F32),
                 jax.ShapeDtypeStruct((b, 256, s), BF16), jax.ShapeDtypeStruct((b, 256, s), BF16),
                 jax.ShapeDtypeStruct((b, 256, s), BF16), jax.ShapeDtypeStruct((b, 256, s), BF16),
                 jax.ShapeDtypeStruct((b, 256, s), BF16), jax.ShapeDtypeStruct((b, 8, s), F32))
    out_specs = (row(256), row(256), row(128), row(C_WIDTH), row(C_WIDTH),
                 col(256), col(256), col(256), col(256), col(256), col(8))
    return pl.pallas_call(_proj_kernel, grid=grid, in_specs=in_specs, out_specs=out_specs, out_shape=out_shape,
                          compiler_params=_cparams("parallel", "parallel"), name="proj")(
        x, g, wrm, wt, bd, gka, gkb, gki, gqaT, gqbT, cos, sina, sinb, cosT, sinT)


def _transpose_cast_kernel(w_ref, o_ref):
    o_ref[...] = w_ref[...].T.astype(BF16)


def _transpose_cast(w):
    d, n = w.shape
    return pl.pallas_call(
        _transpose_cast_kernel, grid=(n // LANES,),
        in_specs=[pl.BlockSpec((d, LANES), lambda i: (0, i))],
        out_specs=pl.BlockSpec((LANES, d), lambda i: (i, 0)),
        out_shape=jax.ShapeDtypeStruct((n, d), BF16),
        compiler_params=_cparams("parallel"), name="transpose_cast")(w)


def _block_diag_q(qT):
    rows = lax.broadcasted_iota(I32, qT.shape, 0) // HEAD_DIM
    zero = jnp.zeros_like(qT)
    return jnp.concatenate([jnp.where(rows == hh, qT, zero) for hh in range(4)], axis=1)


def _chunk_attn_kernel(qT_ref, k_ref, vT_ref, bias_ref, o_ref):
    i = pl.program_id(1)
    qbd = _block_diag_q(qT_ref[0])
    ks, vs, pens = [], [], []
    for t in range(A_KEY_TILES):
        kt = i - (A_KEY_TILES - 1) + t
        r0 = pl.multiple_of(jnp.maximum(kt, 0) * Q_TILE, Q_TILE)
        ks.append(k_ref[0, pl.ds(r0, Q_TILE), :])
        vs.append(vT_ref[0, :, pl.ds(r0, Q_TILE)])
        pens.append(jnp.where(kt >= 0, 0.0, -jnp.inf).astype(F32))
    kwin = jnp.concatenate(ks, axis=0)
    vwin = jnp.concatenate(vs, axis=1)
    s = jnp.dot(kwin, qbd, preferred_element_type=F32) + bias_ref[...]
    s = jnp.concatenate([s[t * Q_TILE:(t + 1) * Q_TILE] + pens[t] for t in range(A_KEY_TILES)], axis=0)
    m = jnp.max(s, axis=0, keepdims=True)
    p = jnp.exp2(s - m)
    l = jnp.sum(p, axis=0, keepdims=True)
    pb = p.astype(BF16)
    outs = []
    for hh in range(A_HEADS):
        lo, hi = hh * Q_TILE, (hh + 1) * Q_TILE
        o = jnp.dot(vwin[hh * HEAD_DIM:(hh + 1) * HEAD_DIM, :], pb[:, lo:hi], preferred_element_type=F32)
        outs.append(o / l[:, lo:hi])
    o_ref[0] = jnp.concatenate(outs, axis=0).T.astype(BF16)


def _chunk_attention(qaT, ka, vaT, biasT):
    b, _, s = qaT.shape
    grid = (b, s // Q_TILE)
    return pl.pallas_call(
        _chunk_attn_kernel, grid=grid,
        in_specs=[pl.BlockSpec((1, 256, Q_TILE), lambda bi, i: (bi, 0, i)),
                  pl.BlockSpec((1, s, 256), lambda bi, i: (bi, 0, 0)),
                  pl.BlockSpec((1, 256, s), lambda bi, i: (bi, 0, 0)),
                  _const_spec(biasT.shape)],
        out_specs=pl.BlockSpec((1, Q_TILE, 256), lambda bi, i: (bi, i, 0)),
        out_shape=jax.ShapeDtypeStruct((b, s, 256), BF16),
        compiler_params=_cparams("parallel", "parallel"), name="chunk_attn")(qaT, ka, vaT, biasT)


def _dsa_kernel(qiT_ref, iwT_ref, qbT_ref, ki_ref, kb_ref, vbT_ref, o_ref, keys_ref, half_ref, *, topk):
    tk = B_KEY_TILE
    i = pl.program_id(1)
    nt = ((i + 1) * Q_TILE + tk - 1) // tk
    qiT = qiT_ref[0]
    qi = jnp.concatenate([qiT[hh * IDX_DIM:(hh + 1) * IDX_DIM] for hh in range(IDX_HEADS)], axis=1)
    qi = jnp.concatenate([qi, jnp.zeros_like(qi)], axis=0)
    w = iwT_ref[0]
    lane = lax.broadcasted_iota(I32, (1, Q_TILE), 1)
    key_limit = i * Q_TILE + CHUNK + jnp.where(lane >= CHUNK, CHUNK, 0)
    row_iota = lax.broadcasted_iota(I32, (tk, Q_TILE), 0)

    def tile_start(t):
        return pl.multiple_of(t * tk, tk)

    def score_tile(t, c):
        r0 = tile_start(t)
        dots = jnp.dot(ki_ref[0, pl.ds(r0, tk), :], qi, preferred_element_type=F32)
        sc = jnp.maximum(dots[:, 0:Q_TILE], 0.0) * w[0:1, :]
        for hh in range(1, IDX_HEADS):
            sc = sc + jnp.maximum(dots[:, hh * Q_TILE:(hh + 1) * Q_TILE], 0.0) * w[hh:hh + 1, :]
        sc = jnp.where(sc == 0.0, 0.0, sc)
        bits = lax.bitcast_convert_type(sc, I32)
        key = jnp.where(bits < 0, bits ^ jnp.int32(0x7FFFFFFF), bits)
        key = jnp.where(r0 + row_iota < key_limit, key, jnp.int32(INT_MIN))
        keys_ref[pl.ds(r0, tk), :] = key
        half_ref[pl.ds(r0, tk), :] = (key >> 16).astype(I16)
        return c

    lax.fori_loop(0, nt, score_tile, 0)

    def count16(cand, strict):
        c16 = jnp.broadcast_to(cand, (PACKED_ROWS, Q_TILE)).astype(I16)
        one, zero = jnp.int16(1), jnp.int16(0)

        def body(t, acc):
            k = half_ref[pl.ds(tile_start(t), tk), :].reshape(tk // PACKED_ROWS, PACKED_ROWS, Q_TILE)
            hit = jnp.where((k > c16[None]) if strict else (k >= c16[None]), one, zero)
            parts = [hit[j] for j in range(tk // PACKED_ROWS)]
            while len(parts) > 1:
                parts = [parts[j] + parts[j + 1] for j in range(0, len(parts), 2)]
            return acc + parts[0]
        acc = lax.fori_loop(0, nt, body, jnp.zeros((PACKED_ROWS, Q_TILE), I16))
        return jnp.sum(acc.astype(I32), axis=0, keepdims=True)

    def kth_largest16(k_want):
        v = jnp.where(count16(jnp.zeros((1, Q_TILE), I32), False) >= k_want, 0, I16_MIN).astype(I32)

        def bit(it, v):
            cand = v + (jnp.int32(1) << (14 - it))
            return jnp.where(count16(cand, False) >= k_want, cand, v)
        return lax.fori_loop(0, 15, bit, v)

    thr_hi = kth_largest16(topk)
    n_hi_gt = count16(thr_hi, True)

    def low_half_tile(t, c):
        r0 = tile_start(t)
        key = keys_ref[pl.ds(r0, tk), :]
        low = (key & jnp.int32(0xFFFF)) + I16_MIN
        half_ref[pl.ds(r0, tk), :] = jnp.where((key >> 16) == thr_hi, low, I16_MIN).astype(I16)
        return c

    lax.fori_loop(0, nt, low_half_tile, 0)
    thr_lo = kth_largest16(topk - n_hi_gt)
    thr = thr_hi * 65536 + (thr_lo - I16_MIN)

    n_gt = n_hi_gt + count16(thr_lo, True)
    short = thr == jnp.int32(INT_MIN)
    need = jnp.where(short, 0, topk - n_gt).astype(F32)
    hk = tk // 2
    tril = (lax.broadcasted_iota(I32, (hk, hk), 0) >= lax.broadcasted_iota(I32, (hk, hk), 1)).astype(BF16)

    qbd = _block_diag_q(qbT_ref[0])

    def attn_tile(t, carry):
        ms, ls, accs, seen = carry
        r0 = tile_start(t)
        key = keys_ref[pl.ds(r0, tk), :]
        tied = key == thr
        tied_b = jnp.where(tied, 1.0, 0.0).astype(BF16)
        rank0 = jnp.dot(tril, tied_b[:hk], preferred_element_type=F32) + seen
        rank1 = jnp.dot(tril, tied_b[hk:], preferred_element_type=F32) + rank0[hk - 1:hk, :]
        rank = jnp.concatenate([rank0, rank1], axis=0)
        sel = jnp.logical_or(key > thr, jnp.logical_and(tied, rank <= need))
        s_all = jnp.dot(kb_ref[0, pl.ds(r0, tk), :], qbd, preferred_element_type=F32)
        new_ms, new_ls, new_accs = [], [], []
        for hh in range(B_HEADS):
            s = jnp.where(sel, s_all[:, hh * Q_TILE:(hh + 1) * Q_TILE], -jnp.inf)
            m_new = jnp.maximum(ms[hh], jnp.max(s, axis=0, keepdims=True))
            alpha = jnp.exp2(ms[hh] - m_new)
            p = jnp.exp2(s - m_new)
            new_ls.append(alpha * ls[hh] + jnp.sum(p, axis=0, keepdims=True))
            pv = jnp.dot(vbT_ref[0, hh * HEAD_DIM:(hh + 1) * HEAD_DIM, pl.ds(r0, tk)], p.astype(BF16),
                         preferred_element_type=F32)
            new_accs.append(alpha * accs[hh] + pv)
            new_ms.append(m_new)
        return tuple(new_ms), tuple(new_ls), tuple(new_accs), rank1[hk - 1:hk, :]

    init = (tuple(jnp.full((1, Q_TILE), -1e30, F32) for _ in range(B_HEADS)),
            tuple(jnp.zeros((1, Q_TILE), F32) for _ in range(B_HEADS)),
            tuple(jnp.zeros((HEAD_DIM, Q_TILE), F32) for _ in range(B_HEADS)),
            jnp.zeros((1, Q_TILE), F32))
    _, ls, accs, _ = lax.fori_loop(0, nt, attn_tile, init)
    out = jnp.concatenate([accs[hh] / ls[hh] for hh in range(B_HEADS)], axis=0)
    o_ref[0] = out.T.astype(BF16)


def _sparse_attention(qiT, iwT, qbT, ki, kb, vbT):
    b, _, s = qbT.shape
    s_pad = -(-s // B_KEY_TILE) * B_KEY_TILE
    topk = min(TOPK_MAX, s // 4)
    grid = (b, s // Q_TILE)
    qspec = lambda r: pl.BlockSpec((1, r, Q_TILE), lambda bi, i: (bi, 0, i))
    kernel = functools.partial(_dsa_kernel, topk=topk)
    return pl.pallas_call(
        kernel, grid=grid,
        in_specs=[qspec(256), qspec(8), qspec(256),
                  pl.BlockSpec((1, s, 128), lambda bi, i: (bi, 0, 0)),
                  pl.BlockSpec((1, s, 256), lambda bi, i: (bi, 0, 0)),
                  pl.BlockSpec((1, 256, s), lambda bi, i: (bi, 0, 0))],
        out_specs=pl.BlockSpec((1, Q_TILE, 256), lambda bi, i: (bi, i, 0)),
        out_shape=jax.ShapeDtypeStruct((b, s, 256), BF16),
        scratch_shapes=[pltpu.VMEM((s_pad, Q_TILE), I32), pltpu.VMEM((s_pad, Q_TILE), I16)],
        compiler_params=_cparams("parallel", "parallel"), name="sparse_attn")(qiT, iwT, qbT, ki, kb, vbT)


def _rglru_kernel(cx_ref, cy_ref, cw_ref, cb_ref, wax_ref, ba_ref, bx_ref, lam_ref, o_ref,
                  tail_ref, h_ref, a_ref, g_ref):
    ts = cx_ref.shape[1]

    @pl.when(pl.program_id(1) == 0)
    def _():
        tail_ref[...] = jnp.zeros_like(tail_ref)
        h_ref[...] = jnp.zeros_like(h_ref)

    x = cx_ref[0]
    tail = tail_ref[...]
    row8 = lax.broadcasted_iota(I32, (SUBLANES, C_WIDTH), 0)
    u = cb_ref[...] + x * cw_ref[C_CONV - 1:C_CONV, :]
    for k in range(1, C_CONV):
        xr = pltpu.roll(x, k, axis=0)
        head = jnp.where(row8 < k, pltpu.roll(tail, k, axis=0), xr[0:SUBLANES])
        xk = jnp.concatenate([head, xr[SUBLANES:]], axis=0)
        u = u + xk * cw_ref[C_CONV - 1 - k:C_CONV - k, :]
    tail_ref[...] = x[ts - SUBLANES:ts]

    gates = jnp.dot(u.astype(BF16), wax_ref[...], preferred_element_type=F32)
    r = jax.nn.sigmoid(gates[:, :C_WIDTH] + ba_ref[...])
    ig = jax.nn.sigmoid(gates[:, C_WIDTH:] + bx_ref[...])
    nlam = -lam_ref[...]
    softplus = jnp.maximum(nlam, 0.0) + jnp.log1p(jnp.exp(-jnp.abs(nlam)))
    a = jnp.exp(-LRU_C * r * softplus)
    a_ref[...] = a
    g_ref[...] = jnp.sqrt(1.0 - a * a) * (ig * u)

    def group(gi, hprev):
        r0 = pl.multiple_of(gi * SUBLANES, SUBLANES)
        av = a_ref[pl.ds(r0, SUBLANES), :]
        bv = g_ref[pl.ds(r0, SUBLANES), :]
        for d in (1, 2, 4):
            keep = row8 >= d
            a_sh = jnp.where(keep, pltpu.roll(av, d, axis=0), 1.0)
            b_sh = jnp.where(keep, pltpu.roll(bv, d, axis=0), 0.0)
            bv = av * b_sh + bv
            av = av * a_sh
        hs = av * hprev + bv
        g_ref[pl.ds(r0, SUBLANES), :] = hs
        return jnp.broadcast_to(hs[SUBLANES - 1:SUBLANES, :], hs.shape)

    h_ref[...] = lax.fori_loop(0, ts // SUBLANES, group, h_ref[...], unroll=2)
    o_ref[0] = (g_ref[...] * jax.nn.gelu(cy_ref[0])).astype(BF16)


def _rglru(cx, cy, cw, cb, wax, ba, bx, lam, ts):
    b, s, c = cx.shape
    blk = pl.BlockSpec((1, ts, c), lambda bi, si: (bi, si, 0))
    return pl.pallas_call(
        _rglru_kernel, grid=(b, s // ts),
        in_specs=[blk, blk, _const_spec(cw.shape), _const_spec(cb.shape), _const_spec(wax.shape),
                  _const_spec(ba.shape), _const_spec(bx.shape), _const_spec(lam.shape)],
        out_specs=blk, out_shape=jax.ShapeDtypeStruct((b, s, c), BF16),
        scratch_shapes=[pltpu.VMEM((SUBLANES, c), F32), pltpu.VMEM((SUBLANES, c), F32),
                        pltpu.VMEM((ts, c), F32), pltpu.VMEM((ts, c), F32)],
        compiler_params=_cparams("parallel", "arbitrary"), name="rglru")(cx, cy, cw, cb, wax, ba, bx, lam)


def _merge_kernel(x_ref, ya_ref, yb_ref, yc_ref, g_ref, wg_ref, bg_ref, wb_ref, wo_ref, o_ref):
    x = x_ref[...]
    d = x.shape[1]
    h = _rms(x, g_ref[...]).astype(BF16)
    ys = (ya_ref[...], yb_ref[...], yc_ref[...])
    offs = (0, A_WIDTH, A_WIDTH + B_WIDTH, A_WIDTH + B_WIDTH + C_WIDTH)
    merged = None
    for j in range(N_BRANCH):
        gate = jax.nn.sigmoid(jnp.dot(h, wg_ref[:, j * d:(j + 1) * d], preferred_element_type=F32)
                              + bg_ref[:, j * d:(j + 1) * d])
        term = gate * jnp.dot(ys[j], wb_ref[offs[j]:offs[j + 1], :], preferred_element_type=F32)
        merged = term if merged is None else merged + term
    o_ref[...] = x + jnp.dot(merged.astype(BF16), wo_ref[...], preferred_element_type=F32)


def _merge(x2, ya, yb, yc, g, wg, bg, wb, wo, tm):
    n, d = x2.shape
    row = lambda w: pl.BlockSpec((tm, w), lambda i: (i, 0))
    return pl.pallas_call(
        _merge_kernel, grid=(n // tm,),
        in_specs=[row(d), row(A_WIDTH), row(B_WIDTH), row(C_WIDTH), _const_spec(g.shape), _const_spec(wg.shape),
                  _const_spec(bg.shape), _const_spec(wb.shape), _const_spec(wo.shape)],
        out_specs=row(d), out_shape=jax.ShapeDtypeStruct((n, d), F32),
        compiler_params=_cparams("parallel"), name="merge")(x2, ya, yb, yc, g, wg, bg, wb, wo)


def _ffn_kernel(x_ref, g_ref, wi_ref, wo_ref, o_ref, *, d_ff, chunk):
    x = x_ref[...]
    h = _rms(x, g_ref[...]).astype(BF16)
    acc = x
    for c0 in range(0, d_ff, chunk):
        c1 = min(c0 + chunk, d_ff)
        gp = jnp.dot(h, wi_ref[:, c0:c1], preferred_element_type=F32)
        up = jnp.dot(h, wi_ref[:, d_ff + c0:d_ff + c1], preferred_element_type=F32)
        act = (jax.nn.silu(gp) * up).astype(BF16)
        acc = acc + jnp.dot(act, wo_ref[c0:c1, :], preferred_element_type=F32)
    o_ref[...] = acc


def _ffn(x2, g, wi, wo, tm):
    n, d = x2.shape
    d_ff = wo.shape[0]
    row = pl.BlockSpec((tm, d), lambda i: (i, 0))
    kernel = functools.partial(_ffn_kernel, d_ff=d_ff, chunk=1024)
    return pl.pallas_call(
        kernel, grid=(n // tm,),
        in_specs=[row, _const_spec(g.shape), _const_spec(wi.shape), _const_spec(wo.shape)],
        out_specs=row, out_shape=jax.ShapeDtypeStruct((n, d), F32),
        compiler_params=_cparams("parallel"), name="ffn")(x2, g, wi, wo)


def _rope_tables(s):
    pos = jnp.arange(s, dtype=F32)
    inv = 1.0 / (ROPE_THETA ** (jnp.arange(0, HEAD_DIM, 2, dtype=F32) / HEAD_DIM))
    ang = pos[:, None] * inv[None, :]
    ang = jnp.concatenate([ang, ang], axis=-1)
    return jnp.cos(ang), jnp.sin(ang)


def _rel_bias_table(rel_bias_l):
    nk = A_KEY_TILES * Q_TILE
    assert Q_TILE - 1 < REL_CLIP
    rb = rel_bias_l.astype(F32)
    n_vec = nk + Q_TILE
    vec = jnp.concatenate([rb[:, 1:], jnp.tile(rb[:, -1:], (1, n_vec - 2 * REL_CLIP))], axis=1)
    skew = jnp.tile(vec, (1, nk))[:, :nk * (n_vec - 1)].reshape(A_HEADS, nk, n_vec - 1)
    toep = skew[:, :, nk - 1:nk - 1 + Q_TILE]
    j = jnp.arange(nk)[:, None]
    q = jnp.arange(Q_TILE)[None, :]
    dchunk = A_LEFT_CHUNKS + q // CHUNK - j // CHUNK
    valid = jnp.logical_and(dchunk >= 0, dchunk <= A_LEFT_CHUNKS)
    tab = jnp.where(valid[None], toep * LOG2_E, -jnp.inf)
    return jnp.transpose(tab, (1, 0, 2)).reshape(nk, A_HEADS * Q_TILE)


def kernel(x, g_mix, w_in, qk_gain_a, rel_bias, qk_gain_b, g_idx_k, conv_w, conv_b, lru_wa, lru_ba, lru_wx,
           lru_bx, lru_lambda, b_gate, w_branch, w_out, g_ffn, w_ffn_in, w_ffn_out):
    b, s, d = x.shape
    depth = g_mix.shape[0]
    assert s % B_KEY_TILE == 0 and d % LANES == 0
    tm = 512
    cos, sin = _rope_tables(s)
    half = (jnp.arange(HEAD_DIM) < HEAD_DIM // 2).astype(F32)
    cos_rm = jnp.tile(cos, (1, 4))
    sina_rm = jnp.tile(-sin * half[None, :], (1, 4))
    sinb_rm = jnp.tile(sin * (1.0 - half)[None, :], (1, 4))
    cosT, sinT = cos.T, sin.T
    hid = jnp.arange(256) // HEAD_DIM
    bd = (hid[:, None] == hid[None, :]).astype(BF16)
    row1 = lambda v: v.reshape(1, -1).astype(F32)
    o_aq, o_ak, o_av, o_bq, o_bk, o_bv = 0, 256, 512, 768, 1024, 1280
    o_iq, o_ik, o_iw, o_cx, o_cy, o_gt = 1536, 1792, 1856, 1860, 2372, 2884
    blocks = jnp.arange(C_WIDTH) // (C_WIDTH // C_BLOCKS)
    bmask = (blocks[:, None] == blocks[None, :])

    def dense_blocks(wblk):
        return jnp.where(bmask, jnp.tile(wblk.reshape(C_WIDTH, C_WIDTH // C_BLOCKS), (1, C_BLOCKS)), 0.0)

    for l in range(depth):
        w = w_in[l]
        zpad = jnp.zeros((d, 64), F32)
        wrm = jnp.concatenate([w[:, o_ak:o_ak + 256], w[:, o_bk:o_bk + 256], w[:, o_ik:o_ik + 64], zpad,
                               w[:, o_cx:o_cx + 512], w[:, o_cy:o_cy + 512]], axis=1).astype(BF16)
        wt = _transpose_cast(jnp.concatenate(
            [w[:, o_aq:o_aq + 256], w[:, o_av:o_av + 256], w[:, o_bq:o_bq + 256], w[:, o_bv:o_bv + 256],
             w[:, o_iq:o_iq + 256], w[:, o_iw:o_iw + 4], jnp.zeros((d, LANES - 4), F32)], axis=1))
        gki = jnp.concatenate([g_idx_k[l], jnp.zeros((64,), F32)]).reshape(1, 128)
        ka, kb, ki, cx, cy, qaT, vaT, qbT, vbT, qiT, iwT = _project(
            x, row1(g_mix[l]), wrm, wt, bd,
            row1(jnp.tile(qk_gain_a[l, 1], 4)), row1(jnp.tile(qk_gain_b[l, 1], 4)), gki,
            jnp.tile(qk_gain_a[l, 0], 4).reshape(256, 1), jnp.tile(qk_gain_b[l, 0], 4).reshape(256, 1),
            cos_rm, sina_rm, sinb_rm, cosT, sinT, tm)
        ya = _chunk_attention(qaT, ka, vaT, _rel_bias_table(rel_bias[l]))
        yb = _sparse_attention(qiT, iwT, qbT, ki, kb, vbT)
        wax = jnp.concatenate([dense_blocks(lru_wa[l]), dense_blocks(lru_wx[l])], axis=1).astype(BF16)
        yc = _rglru(cx, cy, conv_w[l], row1(conv_b[l]), wax, row1(lru_ba[l]), row1(lru_bx[l]),
                    row1(lru_lambda[l]), 512)
        n = b * s
        x1 = _merge(x.reshape(n, d), ya.reshape(n, A_WIDTH), yb.reshape(n, B_WIDTH), yc.reshape(n, C_WIDTH),
                    row1(g_mix[l]), w[:, o_gt:].astype(BF16), row1(b_gate[l]), w_branch[l].astype(BF16),
                    w_out[l].astype(BF16), tm)
        x = _ffn(x1, row1(g_ffn[l]), w_ffn_in[l].astype(BF16), w_ffn_out[l].astype(BF16), tm).reshape(b, s, d)
    return x
```

```python
import functools
import math

import jax
import jax.numpy as jnp
from jax import lax
from jax.experimental import pallas as pl
from jax.experimental.pallas import tpu as pltpu

F32 = jnp.float32
BF16 = jnp.bfloat16
I32 = jnp.int32

CHUNK = 64
HEAD_DIM = 64
EPS = 1e-6
ROPE_THETA = 10000.0
A_HEADS = 4
A_LEFT_CHUNKS = 8
REL_CLIP = 128
B_HEADS = 4
IDX_HEADS = 4
IDX_DIM = 64
TOPK_MAX = 256
C_WIDTH = 512
C_BLOCKS = 8
C_CONV = 4
LRU_C = 8.0
N_BRANCH = 3
A_WIDTH = A_HEADS * HEAD_DIM
B_WIDTH = B_HEADS * HEAD_DIM

LANES = 128
SUBLANES = 8
Q_TILE = 128
A_KEY_TILES = A_LEFT_CHUNKS * CHUNK // Q_TILE + 1
B_KEY_TILE = 512
COUNT_ROWS = 4 * SUBLANES
LOG2_E = math.log2(math.e)
INT_MIN = -2 ** 31
VMEM_LIMIT = 56 * 1024 * 1024

NT_DIMS = (((1,), (1,)), ((), ()))


def _cparams(*sem):
    return pltpu.CompilerParams(dimension_semantics=sem, vmem_limit_bytes=VMEM_LIMIT)


def _const_spec(shape):
    nd = len(shape)
    return pl.BlockSpec(shape, lambda *_: (0,) * nd, pipeline_mode=pl.Buffered(1))


def _rms(x, gain):
    ms = jnp.mean(x * x, axis=-1, keepdims=True)
    return x * lax.rsqrt(ms + EPS) * gain


def _proj_kernel(x_ref, g_ref, wrm_ref, wt_ref, bd_ref, gka_ref, gkb_ref, gki_ref, gqaT_ref, gqbT_ref,
                 cos_ref, sina_ref, sinb_ref, cosT_ref, sinT_ref,
                 ka_ref, kb_ref, ki_ref, cx_ref, cy_ref,
                 qaT_ref, vaT_ref, qbT_ref, vbT_ref, qiT_ref, iwT_ref):
    tm = x_ref.shape[1]
    h = _rms(x_ref[0], g_ref[...]).astype(BF16)

    def rm(c0, c1):
        return jnp.dot(h, wrm_ref[:, c0:c1], preferred_element_type=F32)

    def head_rms_rm(z, gain):
        sq = z * z
        hi = sq.astype(BF16)
        lo = (sq - hi.astype(F32)).astype(BF16)
        ssq = (jnp.dot(hi, bd_ref[...], preferred_element_type=F32)
               + jnp.dot(lo, bd_ref[...], preferred_element_type=F32))
        return z * lax.rsqrt(ssq * (1.0 / HEAD_DIM) + EPS) * gain

    def rope_rm(z):
        w = z.shape[1]
        zm = pltpu.roll(z, w - HEAD_DIM // 2, axis=1)
        zp = pltpu.roll(z, HEAD_DIM // 2, axis=1)
        return z * cos_ref[:, :w] + zm * sina_ref[:, :w] + zp * sinb_ref[:, :w]

    ka_ref[0] = head_rms_rm(rm(0, 256), gka_ref[...]).astype(BF16)
    kb_ref[0] = rope_rm(head_rms_rm(rm(256, 512), gkb_ref[...])).astype(BF16)
    zi = rm(512, 640)
    ms = jnp.sum(zi * zi, axis=-1, keepdims=True) * (1.0 / IDX_DIM)
    ki_ref[0] = rope_rm(zi * lax.rsqrt(ms + EPS) * gki_ref[...]).astype(BF16)
    cx_ref[0] = rm(640, 1152)
    cy_ref[0] = rm(1152, 1664)

    def tr(r0, r1):
        return lax.dot_general(wt_ref[r0:r1, :], h, NT_DIMS, preferred_element_type=F32)

    def head_rms_t(z, gain):
        z4 = z.reshape(4, HEAD_DIM, tm)
        ms4 = jnp.mean(z4 * z4, axis=1, keepdims=True)
        return (z4 * lax.rsqrt(ms4 + EPS)).reshape(4 * HEAD_DIM, tm) * gain

    def rope_t(z):
        z4 = z.reshape(4, HEAD_DIM, tm)
        rot = jnp.concatenate([-z4[:, HEAD_DIM // 2:, :], z4[:, :HEAD_DIM // 2, :]], axis=1)
        return (z4 * cosT_ref[...][None] + rot * sinT_ref[...][None]).reshape(4 * HEAD_DIM, tm)

    scale = HEAD_DIM ** -0.5 * LOG2_E
    qaT_ref[0] = (head_rms_t(tr(0, 256), gqaT_ref[...]) * scale).astype(BF16)
    vaT_ref[0] = tr(256, 512).astype(BF16)
    qbT_ref[0] = (rope_t(head_rms_t(tr(512, 768), gqbT_ref[...])) * scale).astype(BF16)
    vbT_ref[0] = tr(768, 1024).astype(BF16)
    qiT_ref[0] = rope_t(tr(1024, 1280)).astype(BF16)
    iwT_ref[0] = tr(1280, 1296)[0:SUBLANES] * (IDX_HEADS ** -0.5 * IDX_DIM ** -0.5)


def _project(x, g, wrm, wt, bd, gka, gkb, gki, gqaT, gqbT, cos, sina, sinb, cosT, sinT, tm):
    b, s, d = x.shape
    grid = (b, s // tm)
    row = lambda w: pl.BlockSpec((1, tm, w), lambda bi, si: (bi, si, 0))
    col = lambda r: pl.BlockSpec((1, r, tm), lambda bi, si: (bi, 0, si))
    tab_rm = pl.BlockSpec((tm, 256), lambda bi, si: (si, 0))
    tab_t = pl.BlockSpec((HEAD_DIM, tm), lambda bi, si: (0, si))
    in_specs = [row(d), _const_spec(g.shape), _const_spec(wrm.shape), _const_spec(wt.shape), _const_spec(bd.shape),
                _const_spec(gka.shape), _const_spec(gkb.shape), _const_spec(gki.shape),
                _const_spec(gqaT.shape), _const_spec(gqbT.shape),
                tab_rm, tab_rm, tab_rm, tab_t, tab_t]
    out_shape = (jax.ShapeDtypeStruct((b, s, 256), BF16), jax.ShapeDtypeStruct((b, s, 256), BF16),
                 jax.ShapeDtypeStruct((b, s, 128), BF16),
                 jax.ShapeDtypeStruct((b, s, C_WIDTH), F32), jax.ShapeDtypeStruct((b, s, C_WIDTH), F32),
                 jax.ShapeDtypeStruct((b, 256, s), BF16), jax.ShapeDtypeStruct((b, 256, s), BF16),
                 jax.ShapeDtypeStruct((b, 256, s), BF16), jax.ShapeDtypeStruct((b, 256, s), BF16),
                 jax.ShapeDtypeStruct((b, 256, s), BF16), jax.ShapeDtypeStruct((b, 8, s), F32))
    out_specs = (row(256), row(256), row(128), row(C_WIDTH), row(C_WIDTH),
                 col(256), col(256), col(256), col(256), col(256), col(8))
    return pl.pallas_call(_proj_kernel, grid=grid, in_specs=in_specs, out_specs=out_specs, out_shape=out_shape,
                          compiler_params=_cparams("parallel", "parallel"), name="proj")(
        x, g, wrm, wt, bd, gka, gkb, gki, gqaT, gqbT, cos, sina, sinb, cosT, sinT)


def _transpose_cast_kernel(w_ref, o_ref):
    o_ref[...] = w_ref[...].T.astype(BF16)


def _transpose_cast(w):
    d, n = w.shape
    return pl.pallas_call(
        _transpose_cast_kernel, grid=(n // LANES,),
        in_specs=[pl.BlockSpec((d, LANES), lambda i: (0, i))],
        out_specs=pl.BlockSpec((LANES, d), lambda i: (i, 0)),
        out_shape=jax.ShapeDtypeStruct((n, d), BF16),
        compiler_params=_cparams("parallel"), name="transpose_cast")(w)


def _block_diag_q(qT):
    rows = lax.broadcasted_iota(I32, qT.shape, 0) // HEAD_DIM
    zero = jnp.zeros_like(qT)
    return jnp.concatenate([jnp.where(rows == hh, qT, zero) for hh in range(4)], axis=1)


def _chunk_attn_kernel(qT_ref, k_ref, vT_ref, bias_ref, o_ref):
    i = pl.program_id(1)
    qbd = _block_diag_q(qT_ref[0])
    ks, vs, pens = [], [], []
    for t in range(A_KEY_TILES):
        kt = i - (A_KEY_TILES - 1) + t
        r0 = pl.multiple_of(jnp.maximum(kt, 0) * Q_TILE, Q_TILE)
        ks.append(k_ref[0, pl.ds(r0, Q_TILE), :])
        vs.append(vT_ref[0, :, pl.ds(r0, Q_TILE)])
        pens.append(jnp.where(kt >= 0, 0.0, -jnp.inf).astype(F32))
    kwin = jnp.concatenate(ks, axis=0)
    vwin = jnp.concatenate(vs, axis=1)
    s = jnp.dot(kwin, qbd, preferred_element_type=F32) + bias_ref[...]
    s = jnp.concatenate([s[t * Q_TILE:(t + 1) * Q_TILE] + pens[t] for t in range(A_KEY_TILES)], axis=0)
    m = jnp.max(s, axis=0, keepdims=True)
    p = jnp.exp2(s - m)
    l = jnp.sum(p, axis=0, keepdims=True)
    pb = p.astype(BF16)
    outs = []
    for hh in range(A_HEADS):
        lo, hi = hh * Q_TILE, (hh + 1) * Q_TILE
        o = jnp.dot(vwin[hh * HEAD_DIM:(hh + 1) * HEAD_DIM, :], pb[:, lo:hi], preferred_element_type=F32)
        outs.append(o / l[:, lo:hi])
    o_ref[0] = jnp.concatenate(outs, axis=0).T.astype(BF16)


def _chunk_attention(qaT, ka, vaT, biasT):
    b, _, s = qaT.shape
    grid = (b, s // Q_TILE)
    return pl.pallas_call(
        _chunk_attn_kernel, grid=grid,
        in_specs=[pl.BlockSpec((1, 256, Q_TILE), lambda bi, i: (bi, 0, i)),
                  pl.BlockSpec((1, s, 256), lambda bi, i: (bi, 0, 0)),
                  pl.BlockSpec((1, 256, s), lambda bi, i: (bi, 0, 0)),
                  _const_spec(biasT.shape)],
        out_specs=pl.BlockSpec((1, Q_TILE, 256), lambda bi, i: (bi, i, 0)),
        out_shape=jax.ShapeDtypeStruct((b, s, 256), BF16),
        compiler_params=_cparams("parallel", "parallel"), name="chunk_attn")(qaT, ka, vaT, biasT)


def _dsa_kernel(qiT_ref, iwT_ref, qbT_ref, ki_ref, kb_ref, vbT_ref, o_ref, keys_ref, *, topk):
    tk = B_KEY_TILE
    i = pl.program_id(1)
    nt = ((i + 1) * Q_TILE + tk - 1) // tk
    qiT = qiT_ref[0]
    qi = jnp.concatenate([qiT[hh * IDX_DIM:(hh + 1) * IDX_DIM] for hh in range(IDX_HEADS)], axis=1)
    qi = jnp.concatenate([qi, jnp.zeros_like(qi)], axis=0)
    w = iwT_ref[0]
    lane = lax.broadcasted_iota(I32, (1, Q_TILE), 1)
    key_limit = i * Q_TILE + CHUNK + jnp.where(lane >= CHUNK, CHUNK, 0)
    row_iota = lax.broadcasted_iota(I32, (tk, Q_TILE), 0)

    def tile_start(t):
        return pl.multiple_of(t * tk, tk)

    def score_tile(t, c):
        r0 = tile_start(t)
        dots = jnp.dot(ki_ref[0, pl.ds(r0, tk), :], qi, preferred_element_type=F32)
        sc = jnp.maximum(dots[:, 0:Q_TILE], 0.0) * w[0:1, :]
        for hh in range(1, IDX_HEADS):
            sc = sc + jnp.maximum(dots[:, hh * Q_TILE:(hh + 1) * Q_TILE], 0.0) * w[hh:hh + 1, :]
        sc = jnp.where(sc == 0.0, 0.0, sc)
        bits = lax.bitcast_convert_type(sc, I32)
        key = jnp.where(bits < 0, bits ^ jnp.int32(0x7FFFFFFF), bits)
        keys_ref[pl.ds(r0, tk), :] = jnp.where(r0 + row_iota < key_limit, key, jnp.int32(INT_MIN))
        return c

    lax.fori_loop(0, nt, score_tile, 0)

    def over_key_tiles(span, carry):
        wide = 2 * tk
        carry = lax.fori_loop(0, lax.shift_right_logical(nt, 1),
                              lambda u, c: span(pl.multiple_of(u * wide, wide), wide, c), carry)
        return lax.cond((nt & 1) == 1, lambda c: span(tile_start(nt - 1), tk, c), lambda c: c, carry)

    def count(pred):
        def span(r0, rows, acc):
            hit = jnp.where(pred(keys_ref[pl.ds(r0, rows), :]), 1, 0)
            return acc + jnp.sum(hit.reshape(rows // COUNT_ROWS, COUNT_ROWS, Q_TILE), axis=0)
        acc = over_key_tiles(span, jnp.zeros((COUNT_ROWS, Q_TILE), I32))
        return jnp.sum(acc, axis=0, keepdims=True)

    zero = jnp.zeros((1, Q_TILE), I32)
    thr = jnp.where(count(lambda k: k >= zero) >= topk, zero, jnp.int32(INT_MIN))

    def thr_bit(it, thr):
        cand = thr + (jnp.int32(1) << (30 - it))
        return jnp.where(count(lambda k: k >= cand) >= topk, cand, thr)

    thr = lax.fori_loop(0, 31, thr_bit, thr)

    n_gt = count(lambda k: k > thr)
    short = thr == jnp.int32(INT_MIN)
    need = jnp.where(short, 0, topk - n_gt).astype(F32)
    hk = tk // 2
    tril = (lax.broadcasted_iota(I32, (hk, hk), 0) >= lax.broadcasted_iota(I32, (hk, hk), 1)).astype(BF16)

    qbd = _block_diag_q(qbT_ref[0])

    def attn_span(r0, rows, carry):
        ms, ls, accs, seen = carry
        key = keys_ref[pl.ds(r0, rows), :]
        tied = key == thr
        tied_b = jnp.where(tied, 1.0, 0.0).astype(BF16)
        ranks = []
        for c0 in range(0, rows, hk):
            ranks.append(jnp.dot(tril, tied_b[c0:c0 + hk], preferred_element_type=F32) + seen)
            seen = ranks[-1][hk - 1:hk, :]
        sel = jnp.logical_or(key > thr, jnp.logical_and(tied, jnp.concatenate(ranks, axis=0) <= need))
        s_all = jnp.dot(kb_ref[0, pl.ds(r0, rows), :], qbd, preferred_element_type=F32)
        new_ms, new_ls, new_accs = [], [], []
        for hh in range(B_HEADS):
            s = jnp.where(sel, s_all[:, hh * Q_TILE:(hh + 1) * Q_TILE], -jnp.inf)
            m_new = jnp.maximum(ms[hh], jnp.max(s, axis=0, keepdims=True))
            alpha = jnp.exp2(ms[hh] - m_new)
            p = jnp.exp2(s - m_new)
            new_ls.append(alpha * ls[hh] + jnp.sum(p, axis=0, keepdims=True))
            pv = jnp.dot(vbT_ref[0, hh * HEAD_DIM:(hh + 1) * HEAD_DIM, pl.ds(r0, rows)], p.astype(BF16),
                         preferred_element_type=F32)
            new_accs.append(alpha * accs[hh] + pv)
            new_ms.append(m_new)
        return tuple(new_ms), tuple(new_ls), tuple(new_accs), seen

    init = (tuple(jnp.full((1, Q_TILE), -1e30, F32) for _ in range(B_HEADS)),
            tuple(jnp.zeros((1, Q_TILE), F32) for _ in range(B_HEADS)),
            tuple(jnp.zeros((HEAD_DIM, Q_TILE), F32) for _ in range(B_HEADS)),
            jnp.zeros((1, Q_TILE), F32))
    _, ls, accs, _ = over_key_tiles(attn_span, init)
    out = jnp.concatenate([accs[hh] / ls[hh] for hh in range(B_HEADS)], axis=0)
    o_ref[0] = out.T.astype(BF16)


def _sparse_attention(qiT, iwT, qbT, ki, kb, vbT):
    b, _, s = qbT.shape
    s_pad = -(-s // B_KEY_TILE) * B_KEY_TILE
    topk = min(TOPK_MAX, s // 4)
    grid = (b, s // Q_TILE)
    qspec = lambda r: pl.BlockSpec((1, r, Q_TILE), lambda bi, i: (bi, 0, i))
    kernel = functools.partial(_dsa_kernel, topk=topk)
    return pl.pallas_call(
        kernel, grid=grid,
        in_specs=[qspec(256), qspec(8), qspec(256),
                  pl.BlockSpec((1, s, 128), lambda bi, i: (bi, 0, 0)),
                  pl.BlockSpec((1, s, 256), lambda bi, i: (bi, 0, 0)),
                  pl.BlockSpec((1, 256, s), lambda bi, i: (bi, 0, 0))],
        out_specs=pl.BlockSpec((1, Q_TILE, 256), lambda bi, i: (bi, i, 0)),
        out_shape=jax.ShapeDtypeStruct((b, s, 256), BF16),
        scratch_shapes=[pltpu.VMEM((s_pad, Q_TILE), I32)],
        compiler_params=_cparams("parallel", "parallel"), name="sparse_attn")(qiT, iwT, qbT, ki, kb, vbT)


def _rglru_kernel(cx_ref, cy_ref, cw_ref, cb_ref, wax_ref, ba_ref, bx_ref, lam_ref, o_ref,
                  tail_ref, h_ref, a_ref, g_ref):
    ts = cx_ref.shape[1]

    @pl.when(pl.program_id(1) == 0)
    def _():
        tail_ref[...] = jnp.zeros_like(tail_ref)
        h_ref[...] = jnp.zeros_like(h_ref)

    x = cx_ref[0]
    tail = tail_ref[...]
    row8 = lax.broadcasted_iota(I32, (SUBLANES, C_WIDTH), 0)
    u = cb_ref[...] + x * cw_ref[C_CONV - 1:C_CONV, :]
    for k in range(1, C_CONV):
        xr = pltpu.roll(x, k, axis=0)
        head = jnp.where(row8 < k, pltpu.roll(tail, k, axis=0), xr[0:SUBLANES])
        xk = jnp.concatenate([head, xr[SUBLANES:]], axis=0)
        u = u + xk * cw_ref[C_CONV - 1 - k:C_CONV - k, :]
    tail_ref[...] = x[ts - SUBLANES:ts]

    gates = jnp.dot(u.astype(BF16), wax_ref[...], preferred_element_type=F32)
    r = jax.nn.sigmoid(gates[:, :C_WIDTH] + ba_ref[...])
    ig = jax.nn.sigmoid(gates[:, C_WIDTH:] + bx_ref[...])
    nlam = -lam_ref[...]
    softplus = jnp.maximum(nlam, 0.0) + jnp.log1p(jnp.exp(-jnp.abs(nlam)))
    a = jnp.exp(-LRU_C * r * softplus)
    a_ref[...] = a
    g_ref[...] = jnp.sqrt(1.0 - a * a) * (ig * u)

    def group(gi, hprev):
        r0 = pl.multiple_of(gi * SUBLANES, SUBLANES)
        av = a_ref[pl.ds(r0, SUBLANES), :]
        bv = g_ref[pl.ds(r0, SUBLANES), :]
        for d in (1, 2, 4):
            keep = row8 >= d
            a_sh = jnp.where(keep, pltpu.roll(av, d, axis=0), 1.0)
            b_sh = jnp.where(keep, pltpu.roll(bv, d, axis=0), 0.0)
            bv = av * b_sh + bv
            av = av * a_sh
        hs = av * hprev + bv
        g_ref[pl.ds(r0, SUBLANES), :] = hs
        return jnp.broadcast_to(hs[SUBLANES - 1:SUBLANES, :], hs.shape)

    h_ref[...] = lax.fori_loop(0, ts // SUBLANES, group, h_ref[...], unroll=2)
    o_ref[0] = (g_ref[...] * jax.nn.gelu(cy_ref[0])).astype(BF16)


def _rglru(cx, cy, cw, cb, wax, ba, bx, lam, ts):
    b, s, c = cx.shape
    blk = pl.BlockSpec((1, ts, c), lambda bi, si: (bi, si, 0))
    return pl.pallas_call(
        _rglru_kernel, grid=(b, s // ts),
        in_specs=[blk, blk, _const_spec(cw.shape), _const_spec(cb.shape), _const_spec(wax.shape),
                  _const_spec(ba.shape), _const_spec(bx.shape), _const_spec(lam.shape)],
        out_specs=blk, out_shape=jax.ShapeDtypeStruct((b, s, c), BF16),
        scratch_shapes=[pltpu.VMEM((SUBLANES, c), F32), pltpu.VMEM((SUBLANES, c), F32),
                        pltpu.VMEM((ts, c), F32), pltpu.VMEM((ts, c), F32)],
        compiler_params=_cparams("parallel", "arbitrary"), name="rglru")(cx, cy, cw, cb, wax, ba, bx, lam)


def _merge_kernel(x_ref, ya_ref, yb_ref, yc_ref, g_ref, wg_ref, bg_ref, wb_ref, wo_ref, o_ref):
    x = x_ref[...]
    d = x.shape[1]
    h = _rms(x, g_ref[...]).astype(BF16)
    ys = (ya_ref[...], yb_ref[...], yc_ref[...])
    offs = (0, A_WIDTH, A_WIDTH + B_WIDTH, A_WIDTH + B_WIDTH + C_WIDTH)
    merged = None
    for j in range(N_BRANCH):
        gate = jax.nn.sigmoid(jnp.dot(h, wg_ref[:, j * d:(j + 1) * d], preferred_element_type=F32)
                              + bg_ref[:, j * d:(j + 1) * d])
        term = gate * jnp.dot(ys[j], wb_ref[offs[j]:offs[j + 1], :], preferred_element_type=F32)
        merged = term if merged is None else merged + term
    o_ref[...] = x + jnp.dot(merged.astype(BF16), wo_ref[...], preferred_element_type=F32)


def _merge(x2, ya, yb, yc, g, wg, bg, wb, wo, tm):
    n, d = x2.shape
    row = lambda w: pl.BlockSpec((tm, w), lambda i: (i, 0))
    return pl.pallas_call(
        _merge_kernel, grid=(n // tm,),
        in_specs=[row(d), row(A_WIDTH), row(B_WIDTH), row(C_WIDTH), _const_spec(g.shape), _const_spec(wg.shape),
                  _const_spec(bg.shape), _const_spec(wb.shape), _const_spec(wo.shape)],
        out_specs=row(d), out_shape=jax.ShapeDtypeStruct((n, d), F32),
        compiler_params=_cparams("parallel"), name="merge")(x2, ya, yb, yc, g, wg, bg, wb, wo)


def _ffn_kernel(x_ref, g_ref, wi_ref, wo_ref, o_ref, *, d_ff, chunk):
    x = x_ref[...]
    h = _rms(x, g_ref[...]).astype(BF16)
    acc = x
    for c0 in range(0, d_ff, chunk):
        c1 = min(c0 + chunk, d_ff)
        gp = jnp.dot(h, wi_ref[:, c0:c1], preferred_element_type=F32)
        up = jnp.dot(h, wi_ref[:, d_ff + c0:d_ff + c1], preferred_element_type=F32)
        act = (jax.nn.silu(gp) * up).astype(BF16)
        acc = acc + jnp.dot(act, wo_ref[c0:c1, :], preferred_element_type=F32)
    o_ref[...] = acc


def _ffn(x2, g, wi, wo, tm):
    n, d = x2.shape
    d_ff = wo.shape[0]
    row = pl.BlockSpec((tm, d), lambda i: (i, 0))
    kernel = functools.partial(_ffn_kernel, d_ff=d_ff, chunk=1024)
    return pl.pallas_call(
        kernel, grid=(n // tm,),
        in_specs=[row, _const_spec(g.shape), _const_spec(wi.shape), _const_spec(wo.shape)],
        out_specs=row, out_shape=jax.ShapeDtypeStruct((n, d), F32),
        compiler_params=_cparams("parallel"), name="ffn")(x2, g, wi, wo)


def _rope_tables(s):
    pos = jnp.arange(s, dtype=F32)
    inv = 1.0 / (ROPE_THETA ** (jnp.arange(0, HEAD_DIM, 2, dtype=F32) / HEAD_DIM))
    ang = pos[:, None] * inv[None, :]
    ang = jnp.concatenate([ang, ang], axis=-1)
    return jnp.cos(ang), jnp.sin(ang)


def _rel_bias_table(rel_bias_l):
    nk = A_KEY_TILES * Q_TILE
    assert Q_TILE - 1 < REL_CLIP
    rb = rel_bias_l.astype(F32)
    n_vec = nk + Q_TILE
    vec = jnp.concatenate([rb[:, 1:], jnp.tile(rb[:, -1:], (1, n_vec - 2 * REL_CLIP))], axis=1)
    skew = jnp.tile(vec, (1, nk))[:, :nk * (n_vec - 1)].reshape(A_HEADS, nk, n_vec - 1)
    toep = skew[:, :, nk - 1:nk - 1 + Q_TILE]
    j = jnp.arange(nk)[:, None]
    q = jnp.arange(Q_TILE)[None, :]
    dchunk = A_LEFT_CHUNKS + q // CHUNK - j // CHUNK
    valid = jnp.logical_and(dchunk >= 0, dchunk <= A_LEFT_CHUNKS)
    tab = jnp.where(valid[None], toep * LOG2_E, -jnp.inf)
    return jnp.transpose(tab, (1, 0, 2)).reshape(nk, A_HEADS * Q_TILE)


def kernel(x, g_mix, w_in, qk_gain_a, rel_bias, qk_gain_b, g_idx_k, conv_w, conv_b, lru_wa, lru_ba, lru_wx,
           lru_bx, lru_lambda, b_gate, w_branch, w_out, g_ffn, w_ffn_in, w_ffn_out):
    b, s, d = x.shape
    depth = g_mix.shape[0]
    assert s % B_KEY_TILE == 0 and d % LANES == 0
    tm = 512
    cos, sin = _rope_tables(s)
    half = (jnp.arange(HEAD_DIM) < HEAD_DIM // 2).astype(F32)
    cos_rm = jnp.tile(cos, (1, 4))
    sina_rm = jnp.tile(-sin * half[None, :], (1, 4))
    sinb_rm = jnp.tile(sin * (1.0 - half)[None, :], (1, 4))
    cosT, sinT = cos.T, sin.T
    hid = jnp.arange(256) // HEAD_DIM
    bd = (hid[:, None] == hid[None, :]).astype(BF16)
    row1 = lambda v: v.reshape(1, -1).astype(F32)
    o_aq, o_ak, o_av, o_bq, o_bk, o_bv = 0, 256, 512, 768, 1024, 1280
    o_iq, o_ik, o_iw, o_cx, o_cy, o_gt = 1536, 1792, 1856, 1860, 2372, 2884
    blocks = jnp.arange(C_WIDTH) // (C_WIDTH // C_BLOCKS)
    bmask = (blocks[:, None] == blocks[None, :])

    def dense_blocks(wblk):
        return jnp.where(bmask, jnp.tile(wblk.reshape(C_WIDTH, C_WIDTH // C_BLOCKS), (1, C_BLOCKS)), 0.0)

    for l in range(depth):
        w = w_in[l]
        zpad = jnp.zeros((d, 64), F32)
        wrm = jnp.concatenate([w[:, o_ak:o_ak + 256], w[:, o_bk:o_bk + 256], w[:, o_ik:o_ik + 64], zpad,
                               w[:, o_cx:o_cx + 512], w[:, o_cy:o_cy + 512]], axis=1).astype(BF16)
        wt = _transpose_cast(jnp.concatenate(
            [w[:, o_aq:o_aq + 256], w[:, o_av:o_av + 256], w[:, o_bq:o_bq + 256], w[:, o_bv:o_bv + 256],
             w[:, o_iq:o_iq + 256], w[:, o_iw:o_iw + 4], jnp.zeros((d, LANES - 4), F32)], axis=1))
        gki = jnp.concatenate([g_idx_k[l], jnp.zeros((64,), F32)]).reshape(1, 128)
        ka, kb, ki, cx, cy, qaT, vaT, qbT, vbT, qiT, iwT = _project(
            x, row1(g_mix[l]), wrm, wt, bd,
            row1(jnp.tile(qk_gain_a[l, 1], 4)), row1(jnp.tile(qk_gain_b[l, 1], 4)), gki,
            jnp.tile(qk_gain_a[l, 0], 4).reshape(256, 1), jnp.tile(qk_gain_b[l, 0], 4).reshape(256, 1),
            cos_rm, sina_rm, sinb_rm, cosT, sinT, tm)
        ya = _chunk_attention(qaT, ka, vaT, _rel_bias_table(rel_bias[l]))
        yb = _sparse_attention(qiT, iwT, qbT, ki, kb, vbT)
        wax = jnp.concatenate([dense_blocks(lru_wa[l]), dense_blocks(lru_wx[l])], axis=1).astype(BF16)
        yc = _rglru(cx, cy, conv_w[l], row1(conv_b[l]), wax, row1(lru_ba[l]), row1(lru_bx[l]),
                    row1(lru_lambda[l]), 512)
        n = b * s
        x1 = _merge(x.reshape(n, d), ya.reshape(n, A_WIDTH), yb.reshape(n, B_WIDTH), yc.reshape(n, C_WIDTH),
                    row1(g_mix[l]), w[:, o_gt:].astype(BF16), row1(b_gate[l]), w_branch[l].astype(BF16),
                    w_out[l].astype(BF16), tm)
        x = _ffn(x1, row1(g_ffn[l]), w_ffn_in[l].astype(BF16), w_ffn_out[l].astype(BF16), tm).reshape(b, s, d)
    return x
```

```python
import functools
import math

import jax
import jax.numpy as jnp
from jax import lax
from jax.experimental import pallas as pl
from jax.experimental.pallas import tpu as pltpu

F32 = jnp.float32
BF16 = jnp.bfloat16
I32 = jnp.int32

CHUNK = 64
HEAD_DIM = 64
EPS = 1e-6
ROPE_THETA = 10000.0
A_HEADS = 4
A_LEFT_CHUNKS = 8
REL_CLIP = 128
B_HEADS = 4
IDX_HEADS = 4
IDX_DIM = 64
TOPK_MAX = 256
C_WIDTH = 512
C_BLOCKS = 8
C_CONV = 4
LRU_C = 8.0
N_BRANCH = 3
A_WIDTH = A_HEADS * HEAD_DIM
B_WIDTH = B_HEADS * HEAD_DIM

LANES = 128
SUBLANES = 8
Q_TILE = 128
A_KEY_TILES = A_LEFT_CHUNKS * CHUNK // Q_TILE + 1
A_TILES_PER_STEP = 2
B_KEY_TILE = 512
COUNT_ROWS = 4 * SUBLANES
LOG2_E = math.log2(math.e)
INT_MIN = -2 ** 31
VMEM_LIMIT = 56 * 1024 * 1024

NT_DIMS = (((1,), (1,)), ((), ()))


def _cparams(*sem):
    return pltpu.CompilerParams(dimension_semantics=sem, vmem_limit_bytes=VMEM_LIMIT)


def _const_spec(shape):
    nd = len(shape)
    return pl.BlockSpec(shape, lambda *_: (0,) * nd, pipeline_mode=pl.Buffered(1))


def _rms(x, gain):
    ms = jnp.mean(x * x, axis=-1, keepdims=True)
    return x * lax.rsqrt(ms + EPS) * gain


def _proj_kernel(x_ref, g_ref, wrm_ref, wt_ref, bd_ref, gka_ref, gkb_ref, gki_ref, gqaT_ref, gqbT_ref,
                 cos_ref, sina_ref, sinb_ref, cosT_ref, sinT_ref,
                 ka_ref, kb_ref, ki_ref, cx_ref, cy_ref,
                 qaT_ref, vaT_ref, qbT_ref, vbT_ref, qiT_ref, iwT_ref):
    tm = x_ref.shape[1]
    h = _rms(x_ref[0], g_ref[...]).astype(BF16)

    def rm(c0, c1):
        return jnp.dot(h, wrm_ref[:, c0:c1], preferred_element_type=F32)

    def head_rms_rm(z, gain):
        sq = z * z
        hi = sq.astype(BF16)
        lo = (sq - hi.astype(F32)).astype(BF16)
        ssq = (jnp.dot(hi, bd_ref[...], preferred_element_type=F32)
               + jnp.dot(lo, bd_ref[...], preferred_element_type=F32))
        return z * lax.rsqrt(ssq * (1.0 / HEAD_DIM) + EPS) * gain

    def rope_rm(z):
        w = z.shape[1]
        zm = pltpu.roll(z, w - HEAD_DIM // 2, axis=1)
        zp = pltpu.roll(z, HEAD_DIM // 2, axis=1)
        return z * cos_ref[:, :w] + zm * sina_ref[:, :w] + zp * sinb_ref[:, :w]

    ka_ref[0] = head_rms_rm(rm(0, 256), gka_ref[...]).astype(BF16)
    kb_ref[0] = rope_rm(head_rms_rm(rm(256, 512), gkb_ref[...])).astype(BF16)
    zi = rm(512, 640)
    ms = jnp.sum(zi * zi, axis=-1, keepdims=True) * (1.0 / IDX_DIM)
    ki_ref[0] = rope_rm(zi * lax.rsqrt(ms + EPS) * gki_ref[...]).astype(BF16)
    cx_ref[0] = rm(640, 1152)
    cy_ref[0] = rm(1152, 1664)

    def tr(r0, r1):
        return lax.dot_general(wt_ref[r0:r1, :], h, NT_DIMS, preferred_element_type=F32)

    def head_rms_t(z, gain):
        z4 = z.reshape(4, HEAD_DIM, tm)
        ms4 = jnp.mean(z4 * z4, axis=1, keepdims=True)
        return (z4 * lax.rsqrt(ms4 + EPS)).reshape(4 * HEAD_DIM, tm) * gain

    def rope_t(z):
        z4 = z.reshape(4, HEAD_DIM, tm)
        rot = jnp.concatenate([-z4[:, HEAD_DIM // 2:, :], z4[:, :HEAD_DIM // 2, :]], axis=1)
        return (z4 * cosT_ref[...][None] + rot * sinT_ref[...][None]).reshape(4 * HEAD_DIM, tm)

    scale = HEAD_DIM ** -0.5 * LOG2_E
    qaT_ref[0] = (head_rms_t(tr(0, 256), gqaT_ref[...]) * scale).astype(BF16)
    vaT_ref[0] = tr(256, 512).astype(BF16)
    qbT_ref[0] = (rope_t(head_rms_t(tr(512, 768), gqbT_ref[...])) * scale).astype(BF16)
    vbT_ref[0] = tr(768, 1024).astype(BF16)
    qiT_ref[0] = rope_t(tr(1024, 1280)).astype(BF16)
    iwT_ref[0] = tr(1280, 1296)[0:SUBLANES] * (IDX_HEADS ** -0.5 * IDX_DIM ** -0.5)


def _project(x, g, wrm, wt, bd, gka, gkb, gki, gqaT, gqbT, cos, sina, sinb, cosT, sinT, tm):
    b, s, d = x.shape
    grid = (b, s // tm)
    row = lambda w: pl.BlockSpec((1, tm, w), lambda bi, si: (bi, si, 0))
    col = lambda r: pl.BlockSpec((1, r, tm), lambda bi, si: (bi, 0, si))
    tab_rm = pl.BlockSpec((tm, 256), lambda bi, si: (si, 0))
    tab_t = pl.BlockSpec((HEAD_DIM, tm), lambda bi, si: (0, si))
    in_specs = [row(d), _const_spec(g.shape), _const_spec(wrm.shape), _const_spec(wt.shape), _const_spec(bd.shape),
                _const_spec(gka.shape), _const_spec(gkb.shape), _const_spec(gki.shape),
                _const_spec(gqaT.shape), _const_spec(gqbT.shape),
                tab_rm, tab_rm, tab_rm, tab_t, tab_t]
    out_shape = (jax.ShapeDtypeStruct((b, s, 256), BF16), jax.ShapeDtypeStruct((b, s, 256), BF16),
                 jax.ShapeDtypeStruct((b, s, 128), BF16),
                 jax.ShapeDtypeStruct((b, s, C_WIDTH), F32), jax.ShapeDtypeStruct((b, s, C_WIDTH), F32),
                 jax.ShapeDtypeStruct((b, 256, s), BF16), jax.ShapeDtypeStruct((b, 256, s), BF16),
                 jax.ShapeDtypeStruct((b, 256, s), BF16), jax.ShapeDtypeStruct((b, 256, s), BF16),
                 jax.ShapeDtypeStruct((b, 256, s), BF16), jax.ShapeDtypeStruct((b, 8, s), F32))
    out_specs = (row(256), row(256), row(128), row(C_WIDTH), row(C_WIDTH),
                 col(256), col(256), col(256), col(256), col(256), col(8))
    return pl.pallas_call(_proj_kernel, grid=grid, in_specs=in_specs, out_specs=out_specs, out_shape=out_shape,
                          compiler_params=_cparams("parallel", "parallel"), name="proj")(
        x, g, wrm, wt, bd, gka, gkb, gki, gqaT, gqbT, cos, sina, sinb, cosT, sinT)


def _transpose_cast_kernel(w_ref, o_ref):
    o_ref[...] = w_ref[...].T.astype(BF16)


def _transpose_cast(w):
    d, n = w.shape
    return pl.pallas_call(
        _transpose_cast_kernel, grid=(n // LANES,),
        in_specs=[pl.BlockSpec((d, LANES), lambda i: (0, i))],
        out_specs=pl.BlockSpec((LANES, d), lambda i: (i, 0)),
        out_shape=jax.ShapeDtypeStruct((n, d), BF16),
        compiler_params=_cparams("parallel"), name="transpose_cast")(w)


def _block_diag_q(qT):
    rows = lax.broadcasted_iota(I32, qT.shape, 0) // HEAD_DIM
    zero = jnp.zeros_like(qT)
    return jnp.concatenate([jnp.where(rows == hh, qT, zero) for hh in range(4)], axis=1)


def _chunk_attn_kernel(qT_ref, k_ref, vT_ref, bias_ref, o_ref):
    for sub in range(A_TILES_PER_STEP):
        _chunk_attn_tile(pl.program_id(1) * A_TILES_PER_STEP + sub,
                         qT_ref.at[0, :, sub * Q_TILE:(sub + 1) * Q_TILE], k_ref, vT_ref, bias_ref,
                         o_ref.at[0, sub * Q_TILE:(sub + 1) * Q_TILE, :])


def _chunk_attn_tile(i, qT_ref, k_ref, vT_ref, bias_ref, o_ref):
    qbd = _block_diag_q(qT_ref[...])
    ks, vs, pens = [], [], []
    for t in range(A_KEY_TILES):
        kt = i - (A_KEY_TILES - 1) + t
        r0 = pl.multiple_of(jnp.maximum(kt, 0) * Q_TILE, Q_TILE)
        ks.append(k_ref[0, pl.ds(r0, Q_TILE), :])
        vs.append(vT_ref[0, :, pl.ds(r0, Q_TILE)])
        pens.append(jnp.where(kt >= 0, 0.0, -jnp.inf).astype(F32))
    kwin = jnp.concatenate(ks, axis=0)
    vwin = jnp.concatenate(vs, axis=1)
    s = jnp.dot(kwin, qbd, preferred_element_type=F32) + bias_ref[...]
    s = jnp.concatenate([s[t * Q_TILE:(t + 1) * Q_TILE] + pens[t] for t in range(A_KEY_TILES)], axis=0)
    m = jnp.max(s, axis=0, keepdims=True)
    p = jnp.exp2(s - m)
    l = jnp.sum(p, axis=0, keepdims=True)
    pb = p.astype(BF16)
    outs = []
    for hh in range(A_HEADS):
        lo, hi = hh * Q_TILE, (hh + 1) * Q_TILE
        o = jnp.dot(vwin[hh * HEAD_DIM:(hh + 1) * HEAD_DIM, :], pb[:, lo:hi], preferred_element_type=F32)
        outs.append(o / l[:, lo:hi])
    o_ref[...] = jnp.concatenate(outs, axis=0).T.astype(BF16)


def _chunk_attention(qaT, ka, vaT, biasT):
    b, _, s = qaT.shape
    step = A_TILES_PER_STEP * Q_TILE
    grid = (b, s // step)
    return pl.pallas_call(
        _chunk_attn_kernel, grid=grid,
        in_specs=[pl.BlockSpec((1, 256, step), lambda bi, i: (bi, 0, i)),
                  pl.BlockSpec((1, s, 256), lambda bi, i: (bi, 0, 0)),
                  pl.BlockSpec((1, 256, s), lambda bi, i: (bi, 0, 0)),
                  _const_spec(biasT.shape)],
        out_specs=pl.BlockSpec((1, step, 256), lambda bi, i: (bi, i, 0)),
        out_shape=jax.ShapeDtypeStruct((b, s, 256), BF16),
        compiler_params=_cparams("parallel", "parallel"), name="chunk_attn")(qaT, ka, vaT, biasT)


def _dsa_kernel(qiT_ref, iwT_ref, qbT_ref, ki_ref, kb_ref, vbT_ref, o_ref, keys_ref, *, topk):
    tk = B_KEY_TILE
    i = pl.program_id(1)
    nt = ((i + 1) * Q_TILE + tk - 1) // tk
    qiT = qiT_ref[0]
    qi = jnp.concatenate([qiT[hh * IDX_DIM:(hh + 1) * IDX_DIM] for hh in range(IDX_HEADS)], axis=1)
    qi = jnp.concatenate([qi, jnp.zeros_like(qi)], axis=0)
    w = iwT_ref[0]
    lane = lax.broadcasted_iota(I32, (1, Q_TILE), 1)
    key_limit = i * Q_TILE + CHUNK + jnp.where(lane >= CHUNK, CHUNK, 0)

    def tile_start(t):
        return pl.multiple_of(t * tk, tk)

    def over_key_tiles(span, carry):
        wide = 2 * tk
        carry = lax.fori_loop(0, lax.shift_right_logical(nt, 1),
                              lambda u, c: span(pl.multiple_of(u * wide, wide), wide, c), carry)
        return lax.cond((nt & 1) == 1, lambda c: span(tile_start(nt - 1), tk, c), lambda c: c, carry)

    def score_span(r0, rows, c):
        dots = jnp.dot(ki_ref[0, pl.ds(r0, rows), :], qi, preferred_element_type=F32)
        sc = jnp.maximum(dots[:, 0:Q_TILE], 0.0) * w[0:1, :]
        for hh in range(1, IDX_HEADS):
            sc = sc + jnp.maximum(dots[:, hh * Q_TILE:(hh + 1) * Q_TILE], 0.0) * w[hh:hh + 1, :]
        sc = jnp.where(sc == 0.0, 0.0, sc)
        bits = lax.bitcast_convert_type(sc, I32)
        key = jnp.where(bits < 0, bits ^ jnp.int32(0x7FFFFFFF), bits)
        admissible = lax.broadcasted_iota(I32, (rows, Q_TILE), 0) < key_limit - r0
        keys_ref[pl.ds(r0, rows), :] = jnp.where(admissible, key, jnp.int32(INT_MIN))
        return c

    over_key_tiles(score_span, 0)

    def count(pred):
        def span(r0, rows, acc):
            hit = jnp.where(pred(keys_ref[pl.ds(r0, rows), :]), 1, 0)
            return acc + jnp.sum(hit.reshape(rows // COUNT_ROWS, COUNT_ROWS, Q_TILE), axis=0)
        acc = over_key_tiles(span, jnp.zeros((COUNT_ROWS, Q_TILE), I32))
        return jnp.sum(acc, axis=0, keepdims=True)

    zero = jnp.zeros((1, Q_TILE), I32)
    thr = jnp.where(count(lambda k: k >= zero) >= topk, zero, jnp.int32(INT_MIN))

    def thr_bit(it, thr):
        cand = thr + (jnp.int32(1) << (30 - it))
        return jnp.where(count(lambda k: k >= cand) >= topk, cand, thr)

    thr = lax.fori_loop(0, 31, thr_bit, thr)

    n_gt = count(lambda k: k > thr)
    short = thr == jnp.int32(INT_MIN)
    need = jnp.where(short, 0, topk - n_gt).astype(F32)
    hk = tk // 2
    tril = (lax.broadcasted_iota(I32, (hk, hk), 0) >= lax.broadcasted_iota(I32, (hk, hk), 1)).astype(BF16)

    qbd = _block_diag_q(qbT_ref[0])

    def attn_span(r0, rows, carry):
        ms, ls, accs, seen = carry
        key = keys_ref[pl.ds(r0, rows), :]
        tied = key == thr
        tied_b = jnp.where(tied, 1.0, 0.0).astype(BF16)
        ranks = []
        for c0 in range(0, rows, hk):
            ranks.append(jnp.dot(tril, tied_b[c0:c0 + hk], preferred_element_type=F32) + seen)
            seen = ranks[-1][hk - 1:hk, :]
        sel = jnp.logical_or(key > thr, jnp.logical_and(tied, jnp.concatenate(ranks, axis=0) <= need))
        s_all = jnp.dot(kb_ref[0, pl.ds(r0, rows), :], qbd, preferred_element_type=F32)
        new_ms, new_ls, new_accs = [], [], []
        for hh in range(B_HEADS):
            s = jnp.where(sel, s_all[:, hh * Q_TILE:(hh + 1) * Q_TILE], -jnp.inf)
            m_new = jnp.maximum(ms[hh], jnp.max(s, axis=0, keepdims=True))
            alpha = jnp.exp2(ms[hh] - m_new)
            p = jnp.exp2(s - m_new)
            new_ls.append(alpha * ls[hh] + jnp.sum(p, axis=0, keepdims=True))
            pv = jnp.dot(vbT_ref[0, hh * HEAD_DIM:(hh + 1) * HEAD_DIM, pl.ds(r0, rows)], p.astype(BF16),
                         preferred_element_type=F32)
            new_accs.append(alpha * accs[hh] + pv)
            new_ms.append(m_new)
        return tuple(new_ms), tuple(new_ls), tuple(new_accs), seen

    init = (tuple(jnp.full((1, Q_TILE), -1e30, F32) for _ in range(B_HEADS)),
            tuple(jnp.zeros((1, Q_TILE), F32) for _ in range(B_HEADS)),
            tuple(jnp.zeros((HEAD_DIM, Q_TILE), F32) for _ in range(B_HEADS)),
            jnp.zeros((1, Q_TILE), F32))
    _, ls, accs, _ = over_key_tiles(attn_span, init)
    out = jnp.concatenate([accs[hh] / ls[hh] for hh in range(B_HEADS)], axis=0)
    o_ref[0] = out.T.astype(BF16)


def _sparse_attention(qiT, iwT, qbT, ki, kb, vbT):
    b, _, s = qbT.shape
    s_pad = -(-s // B_KEY_TILE) * B_KEY_TILE
    topk = min(TOPK_MAX, s // 4)
    grid = (b, s // Q_TILE)
    qspec = lambda r: pl.BlockSpec((1, r, Q_TILE), lambda bi, i: (bi, 0, i))
    kernel = functools.partial(_dsa_kernel, topk=topk)
    return pl.pallas_call(
        kernel, grid=grid,
        in_specs=[qspec(256), qspec(8), qspec(256),
                  pl.BlockSpec((1, s, 128), lambda bi, i: (bi, 0, 0)),
                  pl.BlockSpec((1, s, 256), lambda bi, i: (bi, 0, 0)),
                  pl.BlockSpec((1, 256, s), lambda bi, i: (bi, 0, 0))],
        out_specs=pl.BlockSpec((1, Q_TILE, 256), lambda bi, i: (bi, i, 0)),
        out_shape=jax.ShapeDtypeStruct((b, s, 256), BF16),
        scratch_shapes=[pltpu.VMEM((s_pad, Q_TILE), I32)],
        compiler_params=_cparams("parallel", "parallel"), name="sparse_attn")(qiT, iwT, qbT, ki, kb, vbT)


def _rglru_kernel(cx_ref, cy_ref, cw_ref, cb_ref, wax_ref, ba_ref, bx_ref, lam_ref, o_ref,
                  tail_ref, h_ref, a_ref, g_ref):
    ts = cx_ref.shape[1]

    @pl.when(pl.program_id(1) == 0)
    def _():
        tail_ref[...] = jnp.zeros_like(tail_ref)
        h_ref[...] = jnp.zeros_like(h_ref)

    x = cx_ref[0]
    tail = tail_ref[...]
    row8 = lax.broadcasted_iota(I32, (SUBLANES, C_WIDTH), 0)
    u = cb_ref[...] + x * cw_ref[C_CONV - 1:C_CONV, :]
    for k in range(1, C_CONV):
        xr = pltpu.roll(x, k, axis=0)
        head = jnp.where(row8 < k, pltpu.roll(tail, k, axis=0), xr[0:SUBLANES])
        xk = jnp.concatenate([head, xr[SUBLANES:]], axis=0)
        u = u + xk * cw_ref[C_CONV - 1 - k:C_CONV - k, :]
    tail_ref[...] = x[ts - SUBLANES:ts]

    gates = jnp.dot(u.astype(BF16), wax_ref[...], preferred_element_type=F32)
    r = jax.nn.sigmoid(gates[:, :C_WIDTH] + ba_ref[...])
    ig = jax.nn.sigmoid(gates[:, C_WIDTH:] + bx_ref[...])
    nlam = -lam_ref[...]
    softplus = jnp.maximum(nlam, 0.0) + jnp.log1p(jnp.exp(-jnp.abs(nlam)))
    a = jnp.exp(-LRU_C * r * softplus)
    a_ref[...] = a
    g_ref[...] = jnp.sqrt(1.0 - a * a) * (ig * u)

    def group(gi, hprev):
        r0 = pl.multiple_of(gi * SUBLANES, SUBLANES)
        av = a_ref[pl.ds(r0, SUBLANES), :]
        bv = g_ref[pl.ds(r0, SUBLANES), :]
        for d in (1, 2, 4):
            keep = row8 >= d
            a_sh = jnp.where(keep, pltpu.roll(av, d, axis=0), 1.0)
            b_sh = jnp.where(keep, pltpu.roll(bv, d, axis=0), 0.0)
            bv = av * b_sh + bv
            av = av * a_sh
        hs = av * hprev + bv
        g_ref[pl.ds(r0, SUBLANES), :] = hs
        return jnp.broadcast_to(hs[SUBLANES - 1:SUBLANES, :], hs.shape)

    h_ref[...] = lax.fori_loop(0, ts // SUBLANES, group, h_ref[...], unroll=8)
    o_ref[0] = (g_ref[...] * jax.nn.gelu(cy_ref[0])).astype(BF16)


def _rglru(cx, cy, cw, cb, wax, ba, bx, lam, ts):
    b, s, c = cx.shape
    blk = pl.BlockSpec((1, ts, c), lambda bi, si: (bi, si, 0))
    return pl.pallas_call(
        _rglru_kernel, grid=(b, s // ts),
        in_specs=[blk, blk, _const_spec(cw.shape), _const_spec(cb.shape), _const_spec(wax.shape),
                  _const_spec(ba.shape), _const_spec(bx.shape), _const_spec(lam.shape)],
        out_specs=blk, out_shape=jax.ShapeDtypeStruct((b, s, c), BF16),
        scratch_shapes=[pltpu.VMEM((SUBLANES, c), F32), pltpu.VMEM((SUBLANES, c), F32),
                        pltpu.VMEM((ts, c), F32), pltpu.VMEM((ts, c), F32)],
        compiler_params=_cparams("parallel", "arbitrary"), name="rglru")(cx, cy, cw, cb, wax, ba, bx, lam)


def _merge_kernel(x_ref, ya_ref, yb_ref, yc_ref, g_ref, wg_ref, bg_ref, wb_ref, wo_ref, o_ref):
    x = x_ref[...]
    d = x.shape[1]
    h = _rms(x, g_ref[...]).astype(BF16)
    ys = (ya_ref[...], yb_ref[...], yc_ref[...])
    offs = (0, A_WIDTH, A_WIDTH + B_WIDTH, A_WIDTH + B_WIDTH + C_WIDTH)
    merged = None
    for j in range(N_BRANCH):
        gate = jax.nn.sigmoid(jnp.dot(h, wg_ref[:, j * d:(j + 1) * d], preferred_element_type=F32)
                              + bg_ref[:, j * d:(j + 1) * d])
        term = gate * jnp.dot(ys[j], wb_ref[offs[j]:offs[j + 1], :], preferred_element_type=F32)
        merged = term if merged is None else merged + term
    o_ref[...] = x + jnp.dot(merged.astype(BF16), wo_ref[...], preferred_element_type=F32)


def _merge(x2, ya, yb, yc, g, wg, bg, wb, wo, tm):
    n, d = x2.shape
    row = lambda w: pl.BlockSpec((tm, w), lambda i: (i, 0))
    return pl.pallas_call(
        _merge_kernel, grid=(n // tm,),
        in_specs=[row(d), row(A_WIDTH), row(B_WIDTH), row(C_WIDTH), _const_spec(g.shape), _const_spec(wg.shape),
                  _const_spec(bg.shape), _const_spec(wb.shape), _const_spec(wo.shape)],
        out_specs=row(d), out_shape=jax.ShapeDtypeStruct((n, d), F32),
        compiler_params=_cparams("parallel"), name="merge")(x2, ya, yb, yc, g, wg, bg, wb, wo)


def _ffn_kernel(x_ref, g_ref, wi_ref, wo_ref, o_ref, *, d_ff, chunk):
    x = x_ref[...]
    h = _rms(x, g_ref[...]).astype(BF16)
    acc = x
    for c0 in range(0, d_ff, chunk):
        c1 = min(c0 + chunk, d_ff)
        gp = jnp.dot(h, wi_ref[:, c0:c1], preferred_element_type=F32)
        up = jnp.dot(h, wi_ref[:, d_ff + c0:d_ff + c1], preferred_element_type=F32)
        act = (jax.nn.silu(gp) * up).astype(BF16)
        acc = acc + jnp.dot(act, wo_ref[c0:c1, :], preferred_element_type=F32)
    o_ref[...] = acc


def _ffn(x2, g, wi, wo, tm):
    n, d = x2.shape
    d_ff = wo.shape[0]
    row = pl.BlockSpec((tm, d), lambda i: (i, 0))
    kernel = functools.partial(_ffn_kernel, d_ff=d_ff, chunk=1024)
    return pl.pallas_call(
        kernel, grid=(n // tm,),
        in_specs=[row, _const_spec(g.shape), _const_spec(wi.shape), _const_spec(wo.shape)],
        out_specs=row, out_shape=jax.ShapeDtypeStruct((n, d), F32),
        compiler_params=_cparams("parallel"), name="ffn")(x2, g, wi, wo)


def _rope_tables(s):
    pos = jnp.arange(s, dtype=F32)
    inv = 1.0 / (ROPE_THETA ** (jnp.arange(0, HEAD_DIM, 2, dtype=F32) / HEAD_DIM))
    ang = pos[:, None] * inv[None, :]
    ang = jnp.concatenate([ang, ang], axis=-1)
    return jnp.cos(ang), jnp.sin(ang)


def _rel_bias_table(rel_bias_l):
    nk = A_KEY_TILES * Q_TILE
    assert Q_TILE - 1 < REL_CLIP
    rb = rel_bias_l.astype(F32)
    n_vec = nk + Q_TILE
    vec = jnp.concatenate([rb[:, 1:], jnp.tile(rb[:, -1:], (1, n_vec - 2 * REL_CLIP))], axis=1)
    skew = jnp.tile(vec, (1, nk))[:, :nk * (n_vec - 1)].reshape(A_HEADS, nk, n_vec - 1)
    toep = skew[:, :, nk - 1:nk - 1 + Q_TILE]
    j = jnp.arange(nk)[:, None]
    q = jnp.arange(Q_TILE)[None, :]
    dchunk = A_LEFT_CHUNKS + q // CHUNK - j // CHUNK
    valid = jnp.logical_and(dchunk >= 0, dchunk <= A_LEFT_CHUNKS)
    tab = jnp.where(valid[None], toep * LOG2_E, -jnp.inf)
    return jnp.transpose(tab, (1, 0, 2)).reshape(nk, A_HEADS * Q_TILE)


def kernel(x, g_mix, w_in, qk_gain_a, rel_bias, qk_gain_b, g_idx_k, conv_w, conv_b, lru_wa, lru_ba, lru_wx,
           lru_bx, lru_lambda, b_gate, w_branch, w_out, g_ffn, w_ffn_in, w_ffn_out):
    b, s, d = x.shape
    depth = g_mix.shape[0]
    assert s % B_KEY_TILE == 0 and d % LANES == 0
    tm = 512
    cos, sin = _rope_tables(s)
    half = (jnp.arange(HEAD_DIM) < HEAD_DIM // 2).astype(F32)
    cos_rm = jnp.tile(cos, (1, 4))
    sina_rm = jnp.tile(-sin * half[None, :], (1, 4))
    sinb_rm = jnp.tile(sin * (1.0 - half)[None, :], (1, 4))
    cosT, sinT = cos.T, sin.T
    hid = jnp.arange(256) // HEAD_DIM
    bd = (hid[:, None] == hid[None, :]).astype(BF16)
    row1 = lambda v: v.reshape(1, -1).astype(F32)
    o_aq, o_ak, o_av, o_bq, o_bk, o_bv = 0, 256, 512, 768, 1024, 1280
    o_iq, o_ik, o_iw, o_cx, o_cy, o_gt = 1536, 1792, 1856, 1860, 2372, 2884
    blocks = jnp.arange(C_WIDTH) // (C_WIDTH // C_BLOCKS)
    bmask = (blocks[:, None] == blocks[None, :])

    def dense_blocks(wblk):
        return jnp.where(bmask, jnp.tile(wblk.reshape(C_WIDTH, C_WIDTH // C_BLOCKS), (1, C_BLOCKS)), 0.0)

    for l in range(depth):
        w = w_in[l]
        zpad = jnp.zeros((d, 64), F32)
        wrm = jnp.concatenate([w[:, o_ak:o_ak + 256], w[:, o_bk:o_bk + 256], w[:, o_ik:o_ik + 64], zpad,
                               w[:, o_cx:o_cx + 512], w[:, o_cy:o_cy + 512]], axis=1).astype(BF16)
        wt = _transpose_cast(jnp.concatenate(
            [w[:, o_aq:o_aq + 256], w[:, o_av:o_av + 256], w[:, o_bq:o_bq + 256], w[:, o_bv:o_bv + 256],
             w[:, o_iq:o_iq + 256], w[:, o_iw:o_iw + 4], jnp.zeros((d, LANES - 4), F32)], axis=1))
        gki = jnp.concatenate([g_idx_k[l], jnp.zeros((64,), F32)]).reshape(1, 128)
        ka, kb, ki, cx, cy, qaT, vaT, qbT, vbT, qiT, iwT = _project(
            x, row1(g_mix[l]), wrm, wt, bd,
            row1(jnp.tile(qk_gain_a[l, 1], 4)), row1(jnp.tile(qk_gain_b[l, 1], 4)), gki,
            jnp.tile(qk_gain_a[l, 0], 4).reshape(256, 1), jnp.tile(qk_gain_b[l, 0], 4).reshape(256, 1),
            cos_rm, sina_rm, sinb_rm, cosT, sinT, tm)
        ya = _chunk_attention(qaT, ka, vaT, _rel_bias_table(rel_bias[l]))
        yb = _sparse_attention(qiT, iwT, qbT, ki, kb, vbT)
        wax = jnp.concatenate([dense_blocks(lru_wa[l]), dense_blocks(lru_wx[l])], axis=1).astype(BF16)
        yc = _rglru(cx, cy, conv_w[l], row1(conv_b[l]), wax, row1(lru_ba[l]), row1(lru_bx[l]),
                    row1(lru_lambda[l]), 512)
        n = b * s
        x1 = _merge(x.reshape(n, d), ya.reshape(n, A_WIDTH), yb.reshape(n, B_WIDTH), yc.reshape(n, C_WIDTH),
                    row1(g_mix[l]), w[:, o_gt:].astype(BF16), row1(b_gate[l]), w_branch[l].astype(BF16),
                    w_out[l].astype(BF16), tm)
        x = _ffn(x1, row1(g_ffn[l]), w_ffn_in[l].astype(BF16), w_ffn_out[l].astype(BF16), tm).reshape(b, s, d)
    return x
```

```python
import functools
import math

import jax
import jax.numpy as jnp
from jax import lax
from jax.experimental import pallas as pl
from jax.experimental.pallas import tpu as pltpu

F32 = jnp.float32
BF16 = jnp.bfloat16
I32 = jnp.int32

CHUNK = 64
HEAD_DIM = 64
EPS = 1e-6
ROPE_THETA = 10000.0
A_HEADS = 4
A_LEFT_CHUNKS = 8
REL_CLIP = 128
B_HEADS = 4
IDX_HEADS = 4
IDX_DIM = 64
TOPK_MAX = 256
C_WIDTH = 512
C_BLOCKS = 8
C_CONV = 4
LRU_C = 8.0
N_BRANCH = 3
A_WIDTH = A_HEADS * HEAD_DIM
B_WIDTH = B_HEADS * HEAD_DIM

LANES = 128
SUBLANES = 8
Q_TILE = 128
A_KEY_TILES = A_LEFT_CHUNKS * CHUNK // Q_TILE + 1
A_TILES_PER_STEP = 2
B_KEY_TILE = 512
COUNT_ROWS = 4 * SUBLANES
LOG2_E = math.log2(math.e)
INT_MIN = -2 ** 31
VMEM_LIMIT = 56 * 1024 * 1024

NT_DIMS = (((1,), (1,)), ((), ()))


def _cparams(*sem):
    return pltpu.CompilerParams(dimension_semantics=sem, vmem_limit_bytes=VMEM_LIMIT)


def _const_spec(shape):
    nd = len(shape)
    return pl.BlockSpec(shape, lambda *_: (0,) * nd, pipeline_mode=pl.Buffered(1))


def _layer_spec(stacked, layer):
    nd = stacked.ndim - 1
    return pl.BlockSpec((None,) + stacked.shape[1:], lambda *_: (layer,) + (0,) * nd, pipeline_mode=pl.Buffered(1))


def _rms(x, gain):
    ms = jnp.mean(x * x, axis=-1, keepdims=True)
    return x * lax.rsqrt(ms + EPS) * gain


def _proj_kernel(x_ref, g_ref, wrm_ref, wt_ref, bd_ref, gka_ref, gkb_ref, gki_ref, gqaT_ref, gqbT_ref,
                 cos_ref, sina_ref, sinb_ref, cosT_ref, sinT_ref,
                 ka_ref, kb_ref, ki_ref, cx_ref, cy_ref,
                 qaT_ref, vaT_ref, qbT_ref, vbT_ref, qiT_ref, iwT_ref):
    tm = x_ref.shape[1]
    h = _rms(x_ref[0], g_ref[...]).astype(BF16)

    def rm(c0, c1):
        return jnp.dot(h, wrm_ref[:, c0:c1], preferred_element_type=F32)

    def head_rms_rm(z, gain):
        sq = z * z
        hi = sq.astype(BF16)
        lo = (sq - hi.astype(F32)).astype(BF16)
        ssq = (jnp.dot(hi, bd_ref[...], preferred_element_type=F32)
               + jnp.dot(lo, bd_ref[...], preferred_element_type=F32))
        return z * lax.rsqrt(ssq * (1.0 / HEAD_DIM) + EPS) * gain

    def rope_rm(z):
        w = z.shape[1]
        zm = pltpu.roll(z, w - HEAD_DIM // 2, axis=1)
        zp = pltpu.roll(z, HEAD_DIM // 2, axis=1)
        return z * cos_ref[:, :w] + zm * sina_ref[:, :w] + zp * sinb_ref[:, :w]

    ka_ref[0] = head_rms_rm(rm(0, 256), gka_ref[...]).astype(BF16)
    kb_ref[0] = rope_rm(head_rms_rm(rm(256, 512), gkb_ref[...])).astype(BF16)
    zi = rm(512, 640)
    ms = jnp.sum(zi * zi, axis=-1, keepdims=True) * (1.0 / IDX_DIM)
    ki_ref[0] = rope_rm(zi * lax.rsqrt(ms + EPS) * gki_ref[...]).astype(BF16)
    cx_ref[0] = rm(640, 1152)
    cy_ref[0] = rm(1152, 1664)

    def tr(r0, r1):
        return lax.dot_general(wt_ref[r0:r1, :], h, NT_DIMS, preferred_element_type=F32)

    def head_rms_t(z, gain):
        z4 = z.reshape(4, HEAD_DIM, tm)
        ms4 = jnp.mean(z4 * z4, axis=1, keepdims=True)
        return (z4 * lax.rsqrt(ms4 + EPS)).reshape(4 * HEAD_DIM, tm) * gain

    def rope_t(z):
        z4 = z.reshape(4, HEAD_DIM, tm)
        rot = jnp.concatenate([-z4[:, HEAD_DIM // 2:, :], z4[:, :HEAD_DIM // 2, :]], axis=1)
        return (z4 * cosT_ref[...][None] + rot * sinT_ref[...][None]).reshape(4 * HEAD_DIM, tm)

    scale = HEAD_DIM ** -0.5 * LOG2_E
    qaT_ref[0] = (head_rms_t(tr(0, 256), gqaT_ref[...]) * scale).astype(BF16)
    vaT_ref[0] = tr(256, 512).astype(BF16)
    qbT_ref[0] = (rope_t(head_rms_t(tr(512, 768), gqbT_ref[...])) * scale).astype(BF16)
    vbT_ref[0] = tr(768, 1024).astype(BF16)
    qiT_ref[0] = rope_t(tr(1024, 1280)).astype(BF16)
    iwT_ref[0] = tr(1280, 1296)[0:SUBLANES] * (IDX_HEADS ** -0.5 * IDX_DIM ** -0.5)


def _project(layer, x, g, wrm, wt, bd, gka, gkb, gki, gqaT, gqbT, cos, sina, sinb, cosT, sinT, tm):
    b, s, d = x.shape
    grid = (b, s // tm)
    row = lambda w: pl.BlockSpec((1, tm, w), lambda bi, si: (bi, si, 0))
    col = lambda r: pl.BlockSpec((1, r, tm), lambda bi, si: (bi, 0, si))
    tab_rm = pl.BlockSpec((tm, 256), lambda bi, si: (si, 0))
    tab_t = pl.BlockSpec((HEAD_DIM, tm), lambda bi, si: (0, si))
    per_layer = lambda a: _layer_spec(a, layer)
    in_specs = [row(d), per_layer(g), per_layer(wrm), per_layer(wt), _const_spec(bd.shape),
                per_layer(gka), per_layer(gkb), per_layer(gki), per_layer(gqaT), per_layer(gqbT),
                tab_rm, tab_rm, tab_rm, tab_t, tab_t]
    out_shape = (jax.ShapeDtypeStruct((b, s, 256), BF16), jax.ShapeDtypeStruct((b, s, 256), BF16),
                 jax.ShapeDtypeStruct((b, s, 128), BF16),
                 jax.ShapeDtypeStruct((b, s, C_WIDTH), F32), jax.ShapeDtypeStruct((b, s, C_WIDTH), F32),
                 jax.ShapeDtypeStruct((b, 256, s), BF16), jax.ShapeDtypeStruct((b, 256, s), BF16),
                 jax.ShapeDtypeStruct((b, 256, s), BF16), jax.ShapeDtypeStruct((b, 256, s), BF16),
                 jax.ShapeDtypeStruct((b, 256, s), BF16), jax.ShapeDtypeStruct((b, 8, s), F32))
    out_specs = (row(256), row(256), row(128), row(C_WIDTH), row(C_WIDTH),
                 col(256), col(256), col(256), col(256), col(256), col(8))
    return pl.pallas_call(_proj_kernel, grid=grid, in_specs=in_specs, out_specs=out_specs, out_shape=out_shape,
                          compiler_params=_cparams("parallel", "parallel"), name="proj")(
        x, g, wrm, wt, bd, gka, gkb, gki, gqaT, gqbT, cos, sina, sinb, cosT, sinT)


def _transpose_cast_kernel(w_ref, o_ref):
    o_ref[...] = w_ref[...].T.astype(BF16)


def _transpose_cast(w):
    depth, d, n = w.shape
    return pl.pallas_call(
        _transpose_cast_kernel, grid=(depth, n // LANES),
        in_specs=[pl.BlockSpec((None, d, LANES), lambda l, i: (l, 0, i))],
        out_specs=pl.BlockSpec((None, LANES, d), lambda l, i: (l, i, 0)),
        out_shape=jax.ShapeDtypeStruct((depth, n, d), BF16),
        compiler_params=_cparams("parallel", "parallel"), name="transpose_cast")(w)


def _block_diag_q(qT):
    rows = lax.broadcasted_iota(I32, qT.shape, 0) // HEAD_DIM
    zero = jnp.zeros_like(qT)
    return jnp.concatenate([jnp.where(rows == hh, qT, zero) for hh in range(4)], axis=1)


def _chunk_attn_kernel(qT_ref, k_ref, vT_ref, bias_ref, o_ref):
    for sub in range(A_TILES_PER_STEP):
        _chunk_attn_tile(pl.program_id(1) * A_TILES_PER_STEP + sub,
                         qT_ref.at[0, :, sub * Q_TILE:(sub + 1) * Q_TILE], k_ref, vT_ref, bias_ref,
                         o_ref.at[0, sub * Q_TILE:(sub + 1) * Q_TILE, :])


def _chunk_attn_tile(i, qT_ref, k_ref, vT_ref, bias_ref, o_ref):
    qbd = _block_diag_q(qT_ref[...])
    ks, vs, pens = [], [], []
    for t in range(A_KEY_TILES):
        kt = i - (A_KEY_TILES - 1) + t
        r0 = pl.multiple_of(jnp.maximum(kt, 0) * Q_TILE, Q_TILE)
        ks.append(k_ref[0, pl.ds(r0, Q_TILE), :])
        vs.append(vT_ref[0, :, pl.ds(r0, Q_TILE)])
        pens.append(jnp.where(kt >= 0, 0.0, -jnp.inf).astype(F32))
    kwin = jnp.concatenate(ks, axis=0)
    vwin = jnp.concatenate(vs, axis=1)
    s = jnp.dot(kwin, qbd, preferred_element_type=F32) + bias_ref[...]
    s = jnp.concatenate([s[t * Q_TILE:(t + 1) * Q_TILE] + pens[t] for t in range(A_KEY_TILES)], axis=0)
    m = jnp.max(s, axis=0, keepdims=True)
    p = jnp.exp2(s - m)
    l = jnp.sum(p, axis=0, keepdims=True)
    pb = p.astype(BF16)
    outs = []
    for hh in range(A_HEADS):
        lo, hi = hh * Q_TILE, (hh + 1) * Q_TILE
        o = jnp.dot(vwin[hh * HEAD_DIM:(hh + 1) * HEAD_DIM, :], pb[:, lo:hi], preferred_element_type=F32)
        outs.append(o / l[:, lo:hi])
    o_ref[...] = jnp.concatenate(outs, axis=0).T.astype(BF16)


def _chunk_attention(layer, qaT, ka, vaT, biasT):
    b, _, s = qaT.shape
    step = A_TILES_PER_STEP * Q_TILE
    grid = (b, s // step)
    return pl.pallas_call(
        _chunk_attn_kernel, grid=grid,
        in_specs=[pl.BlockSpec((1, 256, step), lambda bi, i: (bi, 0, i)),
                  pl.BlockSpec((1, s, 256), lambda bi, i: (bi, 0, 0)),
                  pl.BlockSpec((1, 256, s), lambda bi, i: (bi, 0, 0)),
                  _layer_spec(biasT, layer)],
        out_specs=pl.BlockSpec((1, step, 256), lambda bi, i: (bi, i, 0)),
        out_shape=jax.ShapeDtypeStruct((b, s, 256), BF16),
        compiler_params=_cparams("parallel", "parallel"), name="chunk_attn")(qaT, ka, vaT, biasT)


def _dsa_kernel(qiT_ref, iwT_ref, qbT_ref, ki_ref, kb_ref, vbT_ref, o_ref, keys_ref, *, topk):
    tk = B_KEY_TILE
    i = pl.program_id(1)
    nt = ((i + 1) * Q_TILE + tk - 1) // tk
    qiT = qiT_ref[0]
    qi = jnp.concatenate([qiT[hh * IDX_DIM:(hh + 1) * IDX_DIM] for hh in range(IDX_HEADS)], axis=1)
    qi = jnp.concatenate([qi, jnp.zeros_like(qi)], axis=0)
    w = iwT_ref[0]
    lane = lax.broadcasted_iota(I32, (1, Q_TILE), 1)
    key_limit = i * Q_TILE + CHUNK + jnp.where(lane >= CHUNK, CHUNK, 0)

    def tile_start(t):
        return pl.multiple_of(t * tk, tk)

    def over_key_tiles(span, carry):
        wide = 2 * tk
        carry = lax.fori_loop(0, lax.shift_right_logical(nt, 1),
                              lambda u, c: span(pl.multiple_of(u * wide, wide), wide, c), carry)
        return lax.cond((nt & 1) == 1, lambda c: span(tile_start(nt - 1), tk, c), lambda c: c, carry)

    def score_span(r0, rows, c):
        dots = jnp.dot(ki_ref[0, pl.ds(r0, rows), :], qi, preferred_element_type=F32)
        sc = jnp.maximum(dots[:, 0:Q_TILE], 0.0) * w[0:1, :]
        for hh in range(1, IDX_HEADS):
            sc = sc + jnp.maximum(dots[:, hh * Q_TILE:(hh + 1) * Q_TILE], 0.0) * w[hh:hh + 1, :]
        sc = jnp.where(sc == 0.0, 0.0, sc)
        bits = lax.bitcast_convert_type(sc, I32)
        key = jnp.where(bits < 0, bits ^ jnp.int32(0x7FFFFFFF), bits)
        admissible = lax.broadcasted_iota(I32, (rows, Q_TILE), 0) < key_limit - r0
        keys_ref[pl.ds(r0, rows), :] = jnp.where(admissible, key, jnp.int32(INT_MIN))
        return c

    over_key_tiles(score_span, 0)

    def count(pred):
        def span(r0, rows, acc):
            hit = jnp.where(pred(keys_ref[pl.ds(r0, rows), :]), 1, 0)
            return acc + jnp.sum(hit.reshape(rows // COUNT_ROWS, COUNT_ROWS, Q_TILE), axis=0)
        acc = over_key_tiles(span, jnp.zeros((COUNT_ROWS, Q_TILE), I32))
        return jnp.sum(acc, axis=0, keepdims=True)

    zero = jnp.zeros((1, Q_TILE), I32)
    n_nonneg = count(lambda k: k >= zero)
    thr = jnp.where(n_nonneg >= topk, zero, jnp.int32(INT_MIN))
    n_gt = jnp.where(n_nonneg >= topk, zero, n_nonneg)

    def thr_bit(it, carry):
        thr, n_gt = carry
        cand = thr + (jnp.int32(1) << (30 - it))
        n_cand = count(lambda k: k >= cand)
        return jnp.where(n_cand >= topk, cand, thr), jnp.where(n_cand >= topk, n_gt, n_cand)

    thr, n_gt = lax.fori_loop(0, 31, thr_bit, (thr, n_gt))

    short = thr == jnp.int32(INT_MIN)
    need = jnp.where(short, 0, topk - n_gt).astype(F32)
    hk = tk // 2
    tril = (lax.broadcasted_iota(I32, (hk, hk), 0) >= lax.broadcasted_iota(I32, (hk, hk), 1)).astype(BF16)

    qbd = _block_diag_q(qbT_ref[0])

    def attn_span(r0, rows, carry):
        ms, ls, accs, seen = carry
        key = keys_ref[pl.ds(r0, rows), :]
        tied = key == thr
        tied_b = jnp.where(tied, 1.0, 0.0).astype(BF16)
        ranks = []
        for c0 in range(0, rows, hk):
            ranks.append(jnp.dot(tril, tied_b[c0:c0 + hk], preferred_element_type=F32) + seen)
            seen = ranks[-1][hk - 1:hk, :]
        sel = jnp.logical_or(key > thr, jnp.logical_and(tied, jnp.concatenate(ranks, axis=0) <= need))
        s_all = jnp.dot(kb_ref[0, pl.ds(r0, rows), :], qbd, preferred_element_type=F32)
        new_ms, new_ls, new_accs = [], [], []
        for hh in range(B_HEADS):
            s = jnp.where(sel, s_all[:, hh * Q_TILE:(hh + 1) * Q_TILE], -jnp.inf)
            m_new = jnp.maximum(ms[hh], jnp.max(s, axis=0, keepdims=True))
            alpha = jnp.exp2(ms[hh] - m_new)
            p = jnp.exp2(s - m_new)
            new_ls.append(alpha * ls[hh] + jnp.sum(p, axis=0, keepdims=True))
            pv = jnp.dot(vbT_ref[0, hh * HEAD_DIM:(hh + 1) * HEAD_DIM, pl.ds(r0, rows)], p.astype(BF16),
                         preferred_element_type=F32)
            new_accs.append(alpha * accs[hh] + pv)
            new_ms.append(m_new)
        return tuple(new_ms), tuple(new_ls), tuple(new_accs), seen

    init = (tuple(jnp.full((1, Q_TILE), -1e30, F32) for _ in range(B_HEADS)),
            tuple(jnp.zeros((1, Q_TILE), F32) for _ in range(B_HEADS)),
            tuple(jnp.zeros((HEAD_DIM, Q_TILE), F32) for _ in range(B_HEADS)),
            jnp.zeros((1, Q_TILE), F32))
    _, ls, accs, _ = over_key_tiles(attn_span, init)
    out = jnp.concatenate([accs[hh] / ls[hh] for hh in range(B_HEADS)], axis=0)
    o_ref[0] = out.T.astype(BF16)


def _sparse_attention(qiT, iwT, qbT, ki, kb, vbT):
    b, _, s = qbT.shape
    s_pad = -(-s // B_KEY_TILE) * B_KEY_TILE
    topk = min(TOPK_MAX, s // 4)
    grid = (b, s // Q_TILE)
    qspec = lambda r: pl.BlockSpec((1, r, Q_TILE), lambda bi, i: (bi, 0, i))
    kernel = functools.partial(_dsa_kernel, topk=topk)
    return pl.pallas_call(
        kernel, grid=grid,
        in_specs=[qspec(256), qspec(8), qspec(256),
                  pl.BlockSpec((1, s, 128), lambda bi, i: (bi, 0, 0)),
                  pl.BlockSpec((1, s, 256), lambda bi, i: (bi, 0, 0)),
                  pl.BlockSpec((1, 256, s), lambda bi, i: (bi, 0, 0))],
        out_specs=pl.BlockSpec((1, Q_TILE, 256), lambda bi, i: (bi, i, 0)),
        out_shape=jax.ShapeDtypeStruct((b, s, 256), BF16),
        scratch_shapes=[pltpu.VMEM((s_pad, Q_TILE), I32)],
        compiler_params=_cparams("parallel", "parallel"), name="sparse_attn")(qiT, iwT, qbT, ki, kb, vbT)


def _rglru_kernel(cx_ref, cy_ref, cw_ref, cb_ref, wax_ref, ba_ref, bx_ref, lam_ref, o_ref,
                  tail_ref, h_ref, a_ref, g_ref):
    ts = cx_ref.shape[1]

    @pl.when(pl.program_id(1) == 0)
    def _():
        tail_ref[...] = jnp.zeros_like(tail_ref)
        h_ref[...] = jnp.zeros_like(h_ref)

    x = cx_ref[0]
    tail = tail_ref[...]
    row8 = lax.broadcasted_iota(I32, (SUBLANES, C_WIDTH), 0)
    u = cb_ref[...] + x * cw_ref[C_CONV - 1:C_CONV, :]
    for k in range(1, C_CONV):
        xr = pltpu.roll(x, k, axis=0)
        head = jnp.where(row8 < k, pltpu.roll(tail, k, axis=0), xr[0:SUBLANES])
        xk = jnp.concatenate([head, xr[SUBLANES:]], axis=0)
        u = u + xk * cw_ref[C_CONV - 1 - k:C_CONV - k, :]
    tail_ref[...] = x[ts - SUBLANES:ts]

    gates = jnp.dot(u.astype(BF16), wax_ref[...], preferred_element_type=F32)
    r = jax.nn.sigmoid(gates[:, :C_WIDTH] + ba_ref[...])
    ig = jax.nn.sigmoid(gates[:, C_WIDTH:] + bx_ref[...])
    nlam = -lam_ref[...]
    softplus = jnp.maximum(nlam, 0.0) + jnp.log1p(jnp.exp(-jnp.abs(nlam)))
    a = jnp.exp(-LRU_C * r * softplus)
    a_ref[...] = a
    g_ref[...] = jnp.sqrt(1.0 - a * a) * (ig * u)

    def group(gi, hprev):
        r0 = pl.multiple_of(gi * SUBLANES, SUBLANES)
        av = a_ref[pl.ds(r0, SUBLANES), :]
        bv = g_ref[pl.ds(r0, SUBLANES), :]
        for d in (1, 2, 4):
            keep = row8 >= d
            a_sh = jnp.where(keep, pltpu.roll(av, d, axis=0), 1.0)
            b_sh = jnp.where(keep, pltpu.roll(bv, d, axis=0), 0.0)
            bv = av * b_sh + bv
            av = av * a_sh
        hs = av * hprev + bv
        g_ref[pl.ds(r0, SUBLANES), :] = hs
        return jnp.broadcast_to(hs[SUBLANES - 1:SUBLANES, :], hs.shape)

    h_ref[...] = lax.fori_loop(0, ts // SUBLANES, group, h_ref[...], unroll=8)
    o_ref[0] = (g_ref[...] * jax.nn.gelu(cy_ref[0])).astype(BF16)


def _rglru(layer, cx, cy, cw, cb, wax, ba, bx, lam, ts):
    b, s, c = cx.shape
    blk = pl.BlockSpec((1, ts, c), lambda bi, si: (bi, si, 0))
    return pl.pallas_call(
        _rglru_kernel, grid=(b, s // ts),
        in_specs=[blk, blk] + [_layer_spec(a, layer) for a in (cw, cb, wax, ba, bx, lam)],
        out_specs=blk, out_shape=jax.ShapeDtypeStruct((b, s, c), BF16),
        scratch_shapes=[pltpu.VMEM((SUBLANES, c), F32), pltpu.VMEM((SUBLANES, c), F32),
                        pltpu.VMEM((ts, c), F32), pltpu.VMEM((ts, c), F32)],
        compiler_params=_cparams("parallel", "arbitrary"), name="rglru")(cx, cy, cw, cb, wax, ba, bx, lam)


def _merge_kernel(x_ref, ya_ref, yb_ref, yc_ref, g_ref, wg_ref, bg_ref, wb_ref, wo_ref, o_ref):
    x = x_ref[...]
    d = x.shape[1]
    h = _rms(x, g_ref[...]).astype(BF16)
    ys = (ya_ref[...], yb_ref[...], yc_ref[...])
    offs = (0, A_WIDTH, A_WIDTH + B_WIDTH, A_WIDTH + B_WIDTH + C_WIDTH)
    merged = None
    for j in range(N_BRANCH):
        gate = jax.nn.sigmoid(jnp.dot(h, wg_ref[:, j * d:(j + 1) * d], preferred_element_type=F32)
                              + bg_ref[:, j * d:(j + 1) * d])
        term = gate * jnp.dot(ys[j], wb_ref[offs[j]:offs[j + 1], :], preferred_element_type=F32)
        merged = term if merged is None else merged + term
    o_ref[...] = x + jnp.dot(merged.astype(BF16), wo_ref[...], preferred_element_type=F32)


def _merge(layer, x2, ya, yb, yc, g, wg, bg, wb, wo, tm):
    n, d = x2.shape
    row = lambda w: pl.BlockSpec((tm, w), lambda i: (i, 0))
    return pl.pallas_call(
        _merge_kernel, grid=(n // tm,),
        in_specs=[row(d), row(A_WIDTH), row(B_WIDTH), row(C_WIDTH)]
        + [_layer_spec(a, layer) for a in (g, wg, bg, wb, wo)],
        out_specs=row(d), out_shape=jax.ShapeDtypeStruct((n, d), F32),
        compiler_params=_cparams("parallel"), name="merge")(x2, ya, yb, yc, g, wg, bg, wb, wo)


def _ffn_kernel(x_ref, g_ref, wi_ref, wo_ref, o_ref, *, d_ff, chunk):
    x = x_ref[...]
    h = _rms(x, g_ref[...]).astype(BF16)
    acc = x
    for c0 in range(0, d_ff, chunk):
        c1 = min(c0 + chunk, d_ff)
        gp = jnp.dot(h, wi_ref[:, c0:c1], preferred_element_type=F32)
        up = jnp.dot(h, wi_ref[:, d_ff + c0:d_ff + c1], preferred_element_type=F32)
        act = (jax.nn.silu(gp) * up).astype(BF16)
        acc = acc + jnp.dot(act, wo_ref[c0:c1, :], preferred_element_type=F32)
    o_ref[...] = acc


def _ffn(layer, x2, g, wi, wo, tm):
    n, d = x2.shape
    d_ff = wo.shape[1]
    row = pl.BlockSpec((tm, d), lambda i: (i, 0))
    kernel = functools.partial(_ffn_kernel, d_ff=d_ff, chunk=1024)
    return pl.pallas_call(
        kernel, grid=(n // tm,),
        in_specs=[row] + [_layer_spec(a, layer) for a in (g, wi, wo)],
        out_specs=row, out_shape=jax.ShapeDtypeStruct((n, d), F32),
        compiler_params=_cparams("parallel"), name="ffn")(x2, g, wi, wo)


def _rope_tables(s):
    pos = jnp.arange(s, dtype=F32)
    inv = 1.0 / (ROPE_THETA ** (jnp.arange(0, HEAD_DIM, 2, dtype=F32) / HEAD_DIM))
    ang = pos[:, None] * inv[None, :]
    cos, sin = jnp.cos(ang), jnp.sin(ang)
    return jnp.concatenate([cos, cos], axis=-1), jnp.concatenate([sin, sin], axis=-1)


def _rel_bias_tables(rel_bias):
    depth = rel_bias.shape[0]
    nk = A_KEY_TILES * Q_TILE
    assert Q_TILE - 1 < REL_CLIP
    rb = rel_bias.astype(F32)
    n_vec = nk + Q_TILE
    vec = jnp.concatenate([rb[..., 1:], jnp.tile(rb[..., -1:], (1, 1, n_vec - 2 * REL_CLIP))], axis=-1)
    skew = jnp.tile(vec, (1, 1, nk))[..., :nk * (n_vec - 1)].reshape(depth, A_HEADS, nk, n_vec - 1)
    toep = skew[..., nk - 1:nk - 1 + Q_TILE]
    j = jnp.arange(nk)[:, None]
    q = jnp.arange(Q_TILE)[None, :]
    dchunk = A_LEFT_CHUNKS + q // CHUNK - j // CHUNK
    valid = jnp.logical_and(dchunk >= 0, dchunk <= A_LEFT_CHUNKS)
    tab = jnp.where(valid, toep * LOG2_E, -jnp.inf)
    return jnp.transpose(tab, (0, 2, 1, 3)).reshape(depth, nk, A_HEADS * Q_TILE)


def kernel(x, g_mix, w_in, qk_gain_a, rel_bias, qk_gain_b, g_idx_k, conv_w, conv_b, lru_wa, lru_ba, lru_wx,
           lru_bx, lru_lambda, b_gate, w_branch, w_out, g_ffn, w_ffn_in, w_ffn_out):
    b, s, d = x.shape
    depth = g_mix.shape[0]
    assert s % B_KEY_TILE == 0 and d % LANES == 0
    tm = 512
    cos, sin = _rope_tables(s)
    half = (jnp.arange(HEAD_DIM) < HEAD_DIM // 2).astype(F32)
    cos_rm = jnp.tile(cos, (1, 4))
    sina_rm = jnp.tile(-sin * half[None, :], (1, 4))
    sinb_rm = jnp.tile(sin * (1.0 - half)[None, :], (1, 4))
    cosT, sinT = cos.T, sin.T
    hid = jnp.arange(256) // HEAD_DIM
    bd = (hid[:, None] == hid[None, :]).astype(BF16)
    rows = lambda v: v.reshape(depth, 1, -1).astype(F32)
    o_aq, o_ak, o_av, o_bq, o_bk, o_bv = 0, 256, 512, 768, 1024, 1280
    o_iq, o_ik, o_iw, o_cx, o_cy, o_gt = 1536, 1792, 1856, 1860, 2372, 2884
    blocks = jnp.arange(C_WIDTH) // (C_WIDTH // C_BLOCKS)
    bmask = (blocks[:, None] == blocks[None, :])

    def dense_blocks(wblk):
        return jnp.where(bmask, jnp.tile(wblk.reshape(depth, C_WIDTH, C_WIDTH // C_BLOCKS), (1, 1, C_BLOCKS)), 0.0)

    cols = lambda o, n: w_in[:, :, o:o + n]
    wrm = jnp.concatenate([cols(o_ak, 256), cols(o_bk, 256), cols(o_ik, 64), jnp.zeros((depth, d, 64), F32),
                           cols(o_cx, 512), cols(o_cy, 512)], axis=2).astype(BF16)
    wt = _transpose_cast(jnp.concatenate(
        [cols(o_aq, 256), cols(o_av, 256), cols(o_bq, 256), cols(o_bv, 256), cols(o_iq, 256), cols(o_iw, 4),
         jnp.zeros((depth, d, LANES - 4), F32)], axis=2))
    w_gate = w_in[:, :, o_gt:].astype(BF16)
    gka, gkb = rows(jnp.tile(qk_gain_a[:, 1], (1, 4))), rows(jnp.tile(qk_gain_b[:, 1], (1, 4)))
    gki = rows(jnp.concatenate([g_idx_k, jnp.zeros((depth, 64), F32)], axis=1))
    gqaT = jnp.tile(qk_gain_a[:, 0], (1, 4)).reshape(depth, 256, 1)
    gqbT = jnp.tile(qk_gain_b[:, 0], (1, 4)).reshape(depth, 256, 1)
    bias_tabs = _rel_bias_tables(rel_bias)
    wax = jnp.concatenate([dense_blocks(lru_wa), dense_blocks(lru_wx)], axis=2).astype(BF16)
    g_mix3, g_ffn3, b_gate3 = rows(g_mix), rows(g_ffn), rows(b_gate)
    cb3, ba3, bx3, lam3 = rows(conv_b), rows(lru_ba), rows(lru_bx), rows(lru_lambda)
    wb, wo = w_branch.astype(BF16), w_out.astype(BF16)
    wfi, wfo = w_ffn_in.astype(BF16), w_ffn_out.astype(BF16)

    n = b * s
    for l in range(depth):
        ka, kb, ki, cx, cy, qaT, vaT, qbT, vbT, qiT, iwT = _project(
            l, x, g_mix3, wrm, wt, bd, gka, gkb, gki, gqaT, gqbT, cos_rm, sina_rm, sinb_rm, cosT, sinT, tm)
        ya = _chunk_attention(l, qaT, ka, vaT, bias_tabs)
        yb = _sparse_attention(qiT, iwT, qbT, ki, kb, vbT)
        yc = _rglru(l, cx, cy, conv_w, cb3, wax, ba3, bx3, lam3, 512)
        x1 = _merge(l, x.reshape(n, d), ya.reshape(n, A_WIDTH), yb.reshape(n, B_WIDTH), yc.reshape(n, C_WIDTH),
                    g_mix3, w_gate, b_gate3, wb, wo, tm)
        x = _ffn(l, x1, g_ffn3, wfi, wfo, tm).reshape(b, s, d)
    return x
```

```python
import functools
import math

import jax
import jax.numpy as jnp
from jax import lax
from jax.experimental import pallas as pl
from jax.experimental.pallas import tpu as pltpu

F32 = jnp.float32
BF16 = jnp.bfloat16
I32 = jnp.int32

CHUNK = 64
HEAD_DIM = 64
EPS = 1e-6
ROPE_THETA = 10000.0
A_HEADS = 4
A_LEFT_CHUNKS = 8
REL_CLIP = 128
B_HEADS = 4
IDX_HEADS = 4
IDX_DIM = 64
TOPK_MAX = 256
C_WIDTH = 512
C_BLOCKS = 8
C_CONV = 4
LRU_C = 8.0
N_BRANCH = 3
A_WIDTH = A_HEADS * HEAD_DIM
B_WIDTH = B_HEADS * HEAD_DIM

LANES = 128
SUBLANES = 8
Q_TILE = 128
A_KEY_TILES = A_LEFT_CHUNKS * CHUNK // Q_TILE + 1
A_TILES_PER_STEP = 2
B_KEY_TILE = 512
COUNT_ROWS = 4 * SUBLANES
LOG2_E = math.log2(math.e)
INT_MIN = -2 ** 31
VMEM_LIMIT = 56 * 1024 * 1024

NT_DIMS = (((1,), (1,)), ((), ()))


def _cparams(*sem):
    return pltpu.CompilerParams(dimension_semantics=sem, vmem_limit_bytes=VMEM_LIMIT)


def _const_spec(shape):
    nd = len(shape)
    return pl.BlockSpec(shape, lambda *_: (0,) * nd, pipeline_mode=pl.Buffered(1))


def _layer_spec(stacked, layer):
    nd = stacked.ndim - 1
    return pl.BlockSpec((None,) + stacked.shape[1:], lambda *_: (layer,) + (0,) * nd, pipeline_mode=pl.Buffered(1))


def _rms(x, gain):
    ms = jnp.mean(x * x, axis=-1, keepdims=True)
    return x * lax.rsqrt(ms + EPS) * gain


def _proj_kernel(x_ref, g_ref, wrm_ref, wt_ref, bd_ref, gka_ref, gkb_ref, gki_ref, gqaT_ref, gqbT_ref,
                 cos_ref, sina_ref, sinb_ref, cosT_ref, sinT_ref,
                 ka_ref, kb_ref, ki_ref, cx_ref, cy_ref,
                 qaT_ref, vaT_ref, qbT_ref, vbT_ref, qiT_ref, iwT_ref):
    tm = x_ref.shape[1]
    h = _rms(x_ref[0], g_ref[...]).astype(BF16)

    def rm(c0, c1):
        return jnp.dot(h, wrm_ref[:, c0:c1], preferred_element_type=F32)

    def head_rms_rm(z, gain):
        ssq = jnp.dot((z * z).astype(BF16), bd_ref[...], preferred_element_type=F32)
        return z * lax.rsqrt(ssq * (1.0 / HEAD_DIM) + EPS) * gain

    def rope_rm(z):
        w = z.shape[1]
        zm = pltpu.roll(z, w - HEAD_DIM // 2, axis=1)
        zp = pltpu.roll(z, HEAD_DIM // 2, axis=1)
        return z * cos_ref[:, :w] + zm * sina_ref[:, :w] + zp * sinb_ref[:, :w]

    ka_ref[0] = head_rms_rm(rm(0, 256), gka_ref[...]).astype(BF16)
    kb_ref[0] = rope_rm(head_rms_rm(rm(256, 512), gkb_ref[...])).astype(BF16)
    zi = rm(512, 640)
    ms = jnp.sum(zi * zi, axis=-1, keepdims=True) * (1.0 / IDX_DIM)
    ki_ref[0] = rope_rm(zi * lax.rsqrt(ms + EPS) * gki_ref[...]).astype(BF16)
    cx_ref[0] = rm(640, 1152)
    cy_ref[0] = rm(1152, 1664)

    def tr(r0, r1):
        return lax.dot_general(wt_ref[r0:r1, :], h, NT_DIMS, preferred_element_type=F32)

    def head_rms_t(z, gain):
        z4 = z.reshape(4, HEAD_DIM, tm)
        ms4 = jnp.mean(z4 * z4, axis=1, keepdims=True)
        return (z4 * lax.rsqrt(ms4 + EPS)).reshape(4 * HEAD_DIM, tm) * gain

    def rope_t(z):
        z4 = z.reshape(4, HEAD_DIM, tm)
        rot = jnp.concatenate([-z4[:, HEAD_DIM // 2:, :], z4[:, :HEAD_DIM // 2, :]], axis=1)
        return (z4 * cosT_ref[...][None] + rot * sinT_ref[...][None]).reshape(4 * HEAD_DIM, tm)

    scale = HEAD_DIM ** -0.5 * LOG2_E
    qaT_ref[0] = (head_rms_t(tr(0, 256), gqaT_ref[...]) * scale).astype(BF16)
    vaT_ref[0] = tr(256, 512).astype(BF16)
    qbT_ref[0] = (rope_t(head_rms_t(tr(512, 768), gqbT_ref[...])) * scale).astype(BF16)
    vbT_ref[0] = tr(768, 1024).astype(BF16)
    qiT_ref[0] = rope_t(tr(1024, 1280)).astype(BF16)
    iwT_ref[0] = tr(1280, 1296)[0:SUBLANES] * (IDX_HEADS ** -0.5 * IDX_DIM ** -0.5)


def _project(layer, x, g, wrm, wt, bd, gka, gkb, gki, gqaT, gqbT, cos, sina, sinb, cosT, sinT, tm):
    b, s, d = x.shape
    grid = (b, s // tm)
    row = lambda w: pl.BlockSpec((1, tm, w), lambda bi, si: (bi, si, 0))
    col = lambda r: pl.BlockSpec((1, r, tm), lambda bi, si: (bi, 0, si))
    tab_rm = pl.BlockSpec((tm, 256), lambda bi, si: (si, 0))
    tab_t = pl.BlockSpec((HEAD_DIM, tm), lambda bi, si: (0, si))
    per_layer = lambda a: _layer_spec(a, layer)
    in_specs = [row(d), per_layer(g), per_layer(wrm), per_layer(wt), _const_spec(bd.shape),
                per_layer(gka), per_layer(gkb), per_layer(gki), per_layer(gqaT), per_layer(gqbT),
                tab_rm, tab_rm, tab_rm, tab_t, tab_t]
    out_shape = (jax.ShapeDtypeStruct((b, s, 256), BF16), jax.ShapeDtypeStruct((b, s, 256), BF16),
                 jax.ShapeDtypeStruct((b, s, 128), BF16),
                 jax.ShapeDtypeStruct((b, s, C_WIDTH), F32), jax.ShapeDtypeStruct((b, s, C_WIDTH), F32),
                 jax.ShapeDtypeStruct((b, 256, s), BF16), jax.ShapeDtypeStruct((b, 256, s), BF16),
                 jax.ShapeDtypeStruct((b, 256, s), BF16), jax.ShapeDtypeStruct((b, 256, s), BF16),
                 jax.ShapeDtypeStruct((b, 256, s), BF16), jax.ShapeDtypeStruct((b, 8, s), F32))
    out_specs = (row(256), row(256), row(128), row(C_WIDTH), row(C_WIDTH),
                 col(256), col(256), col(256), col(256), col(256), col(8))
    return pl.pallas_call(_proj_kernel, grid=grid, in_specs=in_specs, out_specs=out_specs, out_shape=out_shape,
                          compiler_params=_cparams("parallel", "parallel"), name="proj")(
        x, g, wrm, wt, bd, gka, gkb, gki, gqaT, gqbT, cos, sina, sinb, cosT, sinT)


def _transpose_cast_kernel(w_ref, o_ref):
    o_ref[...] = w_ref[...].T.astype(BF16)


def _transpose_cast(w):
    depth, d, n = w.shape
    return pl.pallas_call(
        _transpose_cast_kernel, grid=(depth, n // LANES),
        in_specs=[pl.BlockSpec((None, d, LANES), lambda l, i: (l, 0, i))],
        out_specs=pl.BlockSpec((None, LANES, d), lambda l, i: (l, i, 0)),
        out_shape=jax.ShapeDtypeStruct((depth, n, d), BF16),
        compiler_params=_cparams("parallel", "parallel"), name="transpose_cast")(w)


def _block_diag_q(qT):
    rows = lax.broadcasted_iota(I32, qT.shape, 0) // HEAD_DIM
    zero = jnp.zeros_like(qT)
    return jnp.concatenate([jnp.where(rows == hh, qT, zero) for hh in range(4)], axis=1)


def _chunk_attn_kernel(qT_ref, k_ref, vT_ref, bias_ref, o_ref):
    for sub in range(A_TILES_PER_STEP):
        _chunk_attn_tile(pl.program_id(1) * A_TILES_PER_STEP + sub,
                         qT_ref.at[0, :, sub * Q_TILE:(sub + 1) * Q_TILE], k_ref, vT_ref, bias_ref,
                         o_ref.at[0, sub * Q_TILE:(sub + 1) * Q_TILE, :])


def _chunk_attn_tile(i, qT_ref, k_ref, vT_ref, bias_ref, o_ref):
    qbd = _block_diag_q(qT_ref[...])
    ks, vs, pens = [], [], []
    for t in range(A_KEY_TILES):
        kt = i - (A_KEY_TILES - 1) + t
        r0 = pl.multiple_of(jnp.maximum(kt, 0) * Q_TILE, Q_TILE)
        ks.append(k_ref[0, pl.ds(r0, Q_TILE), :])
        vs.append(vT_ref[0, :, pl.ds(r0, Q_TILE)])
        pens.append(jnp.where(kt >= 0, 0.0, -jnp.inf).astype(F32))
    kwin = jnp.concatenate(ks, axis=0)
    vwin = jnp.concatenate(vs, axis=1)
    s = jnp.dot(kwin, qbd, preferred_element_type=F32) + bias_ref[...]
    s = jnp.concatenate([s[t * Q_TILE:(t + 1) * Q_TILE] + pens[t] for t in range(A_KEY_TILES)], axis=0)
    m = jnp.max(s, axis=0, keepdims=True)
    p = jnp.exp2(s - m)
    l = jnp.sum(p, axis=0, keepdims=True)
    pb = p.astype(BF16)
    outs = []
    for hh in range(A_HEADS):
        lo, hi = hh * Q_TILE, (hh + 1) * Q_TILE
        o = jnp.dot(vwin[hh * HEAD_DIM:(hh + 1) * HEAD_DIM, :], pb[:, lo:hi], preferred_element_type=F32)
        outs.append(o / l[:, lo:hi])
    o_ref[...] = jnp.concatenate(outs, axis=0).T.astype(BF16)


def _chunk_attention(layer, qaT, ka, vaT, biasT):
    b, _, s = qaT.shape
    step = A_TILES_PER_STEP * Q_TILE
    grid = (b, s // step)
    return pl.pallas_call(
        _chunk_attn_kernel, grid=grid,
        in_specs=[pl.BlockSpec((1, 256, step), lambda bi, i: (bi, 0, i)),
                  pl.BlockSpec((1, s, 256), lambda bi, i: (bi, 0, 0)),
                  pl.BlockSpec((1, 256, s), lambda bi, i: (bi, 0, 0)),
                  _layer_spec(biasT, layer)],
        out_specs=pl.BlockSpec((1, step, 256), lambda bi, i: (bi, i, 0)),
        out_shape=jax.ShapeDtypeStruct((b, s, 256), BF16),
        compiler_params=_cparams("parallel", "parallel"), name="chunk_attn")(qaT, ka, vaT, biasT)


def _dsa_kernel(qiT_ref, iwT_ref, qbT_ref, ki_ref, kb_ref, vbT_ref, o_ref, keys_ref, *, topk):
    tk = B_KEY_TILE
    i = pl.program_id(1)
    n_small = lax.shift_right_logical(i + 2, 1)
    n_wide = lax.shift_right_logical(n_small, 2)
    has_mid, has_small = (n_small & 2) != 0, (n_small & 1) != 0
    qiT = qiT_ref[0]
    qi = jnp.concatenate([qiT[hh * IDX_DIM:(hh + 1) * IDX_DIM] for hh in range(IDX_HEADS)], axis=1)
    qi = jnp.concatenate([qi, jnp.zeros_like(qi)], axis=0)
    w = iwT_ref[0]
    lane = lax.broadcasted_iota(I32, (1, Q_TILE), 1)
    key_limit = i * Q_TILE + CHUNK + jnp.where(lane >= CHUNK, CHUNK, 0)

    def over_key_tiles(span, carry):
        wide, small = 2 * tk, tk // 2
        carry = lax.fori_loop(0, n_wide, lambda u, c: span(pl.multiple_of(u * wide, wide), wide, c), carry)
        r_mid = pl.multiple_of(n_wide * wide, wide)
        carry = lax.cond(has_mid, lambda c: span(r_mid, tk, c), lambda c: c, carry)
        r_small = pl.multiple_of(r_mid + jnp.where(has_mid, tk, 0), small)
        return lax.cond(has_small, lambda c: span(r_small, small, c), lambda c: c, carry)

    def score_span(r0, rows, c):
        dots = jnp.dot(ki_ref[0, pl.ds(r0, rows), :], qi, preferred_element_type=F32)
        sc = jnp.maximum(dots[:, 0:Q_TILE], 0.0) * w[0:1, :]
        for hh in range(1, IDX_HEADS):
            sc = sc + jnp.maximum(dots[:, hh * Q_TILE:(hh + 1) * Q_TILE], 0.0) * w[hh:hh + 1, :]
        sc = jnp.where(sc == 0.0, 0.0, sc)
        bits = lax.bitcast_convert_type(sc, I32)
        key = jnp.where(bits < 0, bits ^ jnp.int32(0x7FFFFFFF), bits)
        admissible = lax.broadcasted_iota(I32, (rows, Q_TILE), 0) < key_limit - r0
        keys_ref[pl.ds(r0, rows), :] = jnp.where(admissible, key, jnp.int32(INT_MIN))
        return c

    over_key_tiles(score_span, 0)

    def count(pred):
        def span(r0, rows, acc):
            hit = jnp.where(pred(keys_ref[pl.ds(r0, rows), :]), 1, 0)
            return acc + jnp.sum(hit.reshape(rows // COUNT_ROWS, COUNT_ROWS, Q_TILE), axis=0)
        acc = over_key_tiles(span, jnp.zeros((COUNT_ROWS, Q_TILE), I32))
        return jnp.sum(acc, axis=0, keepdims=True)

    zero = jnp.zeros((1, Q_TILE), I32)
    n_nonneg = count(lambda k: k >= zero)
    thr = jnp.where(n_nonneg >= topk, zero, jnp.int32(INT_MIN))
    n_gt = jnp.where(n_nonneg >= topk, zero, n_nonneg)

    def thr_bit(it, carry):
        thr, n_gt = carry
        cand = thr + (jnp.int32(1) << (30 - it))
        n_cand = count(lambda k: k >= cand)
        return jnp.where(n_cand >= topk, cand, thr), jnp.where(n_cand >= topk, n_gt, n_cand)

    thr, n_gt = lax.fori_loop(0, 31, thr_bit, (thr, n_gt))

    short = thr == jnp.int32(INT_MIN)
    need = jnp.where(short, 0, topk - n_gt).astype(F32)
    hk = tk // 2
    tril = (lax.broadcasted_iota(I32, (hk, hk), 0) >= lax.broadcasted_iota(I32, (hk, hk), 1)).astype(BF16)

    qbd = _block_diag_q(qbT_ref[0])

    def attn_span(r0, rows, carry):
        ms, ls, accs, seen = carry
        key = keys_ref[pl.ds(r0, rows), :]
        tied = key == thr
        tied_b = jnp.where(tied, 1.0, 0.0).astype(BF16)
        ranks = []
        for c0 in range(0, rows, hk):
            ranks.append(jnp.dot(tril, tied_b[c0:c0 + hk], preferred_element_type=F32) + seen)
            seen = ranks[-1][hk - 1:hk, :]
        sel = jnp.logical_or(key > thr, jnp.logical_and(tied, jnp.concatenate(ranks, axis=0) <= need))
        s_all = jnp.dot(kb_ref[0, pl.ds(r0, rows), :], qbd, preferred_element_type=F32)
        new_ms, new_ls, new_accs = [], [], []
        for hh in range(B_HEADS):
            s = jnp.where(sel, s_all[:, hh * Q_TILE:(hh + 1) * Q_TILE], -jnp.inf)
            m_new = jnp.maximum(ms[hh], jnp.max(s, axis=0, keepdims=True))
            alpha = jnp.exp2(ms[hh] - m_new)
            p = jnp.exp2(s - m_new)
            new_ls.append(alpha * ls[hh] + jnp.sum(p, axis=0, keepdims=True))
            pv = jnp.dot(vbT_ref[0, hh * HEAD_DIM:(hh + 1) * HEAD_DIM, pl.ds(r0, rows)], p.astype(BF16),
                         preferred_element_type=F32)
            new_accs.append(alpha * accs[hh] + pv)
            new_ms.append(m_new)
        return tuple(new_ms), tuple(new_ls), tuple(new_accs), seen

    init = (tuple(jnp.full((1, Q_TILE), -1e30, F32) for _ in range(B_HEADS)),
            tuple(jnp.zeros((1, Q_TILE), F32) for _ in range(B_HEADS)),
            tuple(jnp.zeros((HEAD_DIM, Q_TILE), F32) for _ in range(B_HEADS)),
            jnp.zeros((1, Q_TILE), F32))
    _, ls, accs, _ = over_key_tiles(attn_span, init)
    out = jnp.concatenate([accs[hh] / ls[hh] for hh in range(B_HEADS)], axis=0)
    o_ref[0] = out.T.astype(BF16)


def _sparse_attention(qiT, iwT, qbT, ki, kb, vbT):
    b, _, s = qbT.shape
    s_pad = -(-s // B_KEY_TILE) * B_KEY_TILE
    topk = min(TOPK_MAX, s // 4)
    grid = (b, s // Q_TILE)
    qspec = lambda r: pl.BlockSpec((1, r, Q_TILE), lambda bi, i: (bi, 0, i))
    kernel = functools.partial(_dsa_kernel, topk=topk)
    return pl.pallas_call(
        kernel, grid=grid,
        in_specs=[qspec(256), qspec(8), qspec(256),
                  pl.BlockSpec((1, s, 128), lambda bi, i: (bi, 0, 0)),
                  pl.BlockSpec((1, s, 256), lambda bi, i: (bi, 0, 0)),
                  pl.BlockSpec((1, 256, s), lambda bi, i: (bi, 0, 0))],
        out_specs=pl.BlockSpec((1, Q_TILE, 256), lambda bi, i: (bi, i, 0)),
        out_shape=jax.ShapeDtypeStruct((b, s, 256), BF16),
        scratch_shapes=[pltpu.VMEM((s_pad, Q_TILE), I32)],
        compiler_params=_cparams("parallel", "parallel"), name="sparse_attn")(qiT, iwT, qbT, ki, kb, vbT)


def _rglru_kernel(cx_ref, cy_ref, cw_ref, cb_ref, wax_ref, ba_ref, bx_ref, lam_ref, o_ref,
                  tail_ref, h_ref, a_ref, g_ref):
    ts = cx_ref.shape[1]

    @pl.when(pl.program_id(1) == 0)
    def _():
        tail_ref[...] = jnp.zeros_like(tail_ref)
        h_ref[...] = jnp.zeros_like(h_ref)

    x = cx_ref[0]
    tail = tail_ref[...]
    row8 = lax.broadcasted_iota(I32, (SUBLANES, C_WIDTH), 0)
    u = cb_ref[...] + x * cw_ref[C_CONV - 1:C_CONV, :]
    for k in range(1, C_CONV):
        xr = pltpu.roll(x, k, axis=0)
        head = jnp.where(row8 < k, pltpu.roll(tail, k, axis=0), xr[0:SUBLANES])
        xk = jnp.concatenate([head, xr[SUBLANES:]], axis=0)
        u = u + xk * cw_ref[C_CONV - 1 - k:C_CONV - k, :]
    tail_ref[...] = x[ts - SUBLANES:ts]

    gates = jnp.dot(u.astype(BF16), wax_ref[...], preferred_element_type=F32)
    r = jax.nn.sigmoid(gates[:, :C_WIDTH] + ba_ref[...])
    ig = jax.nn.sigmoid(gates[:, C_WIDTH:] + bx_ref[...])
    nlam = -lam_ref[...]
    softplus = jnp.maximum(nlam, 0.0) + jnp.log1p(jnp.exp(-jnp.abs(nlam)))
    a = jnp.exp(-LRU_C * r * softplus)
    a_ref[...] = a
    g_ref[...] = jnp.sqrt(1.0 - a * a) * (ig * u)

    def group(gi, hprev):
        r0 = pl.multiple_of(gi * SUBLANES, SUBLANES)
        av = a_ref[pl.ds(r0, SUBLANES), :]
        bv = g_ref[pl.ds(r0, SUBLANES), :]
        for d in (1, 2, 4):
            keep = row8 >= d
            a_sh = jnp.where(keep, pltpu.roll(av, d, axis=0), 1.0)
            b_sh = jnp.where(keep, pltpu.roll(bv, d, axis=0), 0.0)
            bv = av * b_sh + bv
            av = av * a_sh
        hs = av * hprev + bv
        g_ref[pl.ds(r0, SUBLANES), :] = hs
        return jnp.broadcast_to(hs[SUBLANES - 1:SUBLANES, :], hs.shape)

    h_ref[...] = lax.fori_loop(0, ts // SUBLANES, group, h_ref[...], unroll=8)
    o_ref[0] = (g_ref[...] * jax.nn.gelu(cy_ref[0])).astype(BF16)


def _rglru(layer, cx, cy, cw, cb, wax, ba, bx, lam, ts):
    b, s, c = cx.shape
    blk = pl.BlockSpec((1, ts, c), lambda bi, si: (bi, si, 0))
    return pl.pallas_call(
        _rglru_kernel, grid=(b, s // ts),
        in_specs=[blk, blk] + [_layer_spec(a, layer) for a in (cw, cb, wax, ba, bx, lam)],
        out_specs=blk, out_shape=jax.ShapeDtypeStruct((b, s, c), BF16),
        scratch_shapes=[pltpu.VMEM((SUBLANES, c), F32), pltpu.VMEM((SUBLANES, c), F32),
                        pltpu.VMEM((ts, c), F32), pltpu.VMEM((ts, c), F32)],
        compiler_params=_cparams("parallel", "arbitrary"), name="rglru")(cx, cy, cw, cb, wax, ba, bx, lam)


def _merge_kernel(x_ref, ya_ref, yb_ref, yc_ref, g_ref, wg_ref, bg_ref, wb_ref, wo_ref, o_ref):
    x = x_ref[...]
    d = x.shape[1]
    h = _rms(x, g_ref[...]).astype(BF16)
    ys = (ya_ref[...], yb_ref[...], yc_ref[...])
    offs = (0, A_WIDTH, A_WIDTH + B_WIDTH, A_WIDTH + B_WIDTH + C_WIDTH)
    merged = None
    for j in range(N_BRANCH):
        gate = jax.nn.sigmoid(jnp.dot(h, wg_ref[:, j * d:(j + 1) * d], preferred_element_type=F32)
                              + bg_ref[:, j * d:(j + 1) * d])
        term = gate * jnp.dot(ys[j], wb_ref[offs[j]:offs[j + 1], :], preferred_element_type=F32)
        merged = term if merged is None else merged + term
    o_ref[...] = x + jnp.dot(merged.astype(BF16), wo_ref[...], preferred_element_type=F32)


def _merge(layer, x2, ya, yb, yc, g, wg, bg, wb, wo, tm):
    n, d = x2.shape
    row = lambda w: pl.BlockSpec((tm, w), lambda i: (i, 0))
    return pl.pallas_call(
        _merge_kernel, grid=(n // tm,),
        in_specs=[row(d), row(A_WIDTH), row(B_WIDTH), row(C_WIDTH)]
        + [_layer_spec(a, layer) for a in (g, wg, bg, wb, wo)],
        out_specs=row(d), out_shape=jax.ShapeDtypeStruct((n, d), F32),
        compiler_params=_cparams("parallel"), name="merge")(x2, ya, yb, yc, g, wg, bg, wb, wo)


def _ffn_kernel(x_ref, g_ref, wi_ref, wo_ref, o_ref, *, d_ff, chunk):
    x = x_ref[...]
    h = _rms(x, g_ref[...]).astype(BF16)
    acc = x
    for c0 in range(0, d_ff, chunk):
        c1 = min(c0 + chunk, d_ff)
        gp = jnp.dot(h, wi_ref[:, c0:c1], preferred_element_type=F32)
        up = jnp.dot(h, wi_ref[:, d_ff + c0:d_ff + c1], preferred_element_type=F32)
        act = (jax.nn.silu(gp) * up).astype(BF16)
        acc = acc + jnp.dot(act, wo_ref[c0:c1, :], preferred_element_type=F32)
    o_ref[...] = acc


def _ffn(layer, x2, g, wi, wo, tm):
    n, d = x2.shape
    d_ff = wo.shape[1]
    row = pl.BlockSpec((tm, d), lambda i: (i, 0))
    kernel = functools.partial(_ffn_kernel, d_ff=d_ff, chunk=1024)
    return pl.pallas_call(
        kernel, grid=(n // tm,),
        in_specs=[row] + [_layer_spec(a, layer) for a in (g, wi, wo)],
        out_specs=row, out_shape=jax.ShapeDtypeStruct((n, d), F32),
        compiler_params=_cparams("parallel"), name="ffn")(x2, g, wi, wo)


def _rope_tables(s):
    pos = jnp.arange(s, dtype=F32)
    inv = 1.0 / (ROPE_THETA ** (jnp.arange(0, HEAD_DIM, 2, dtype=F32) / HEAD_DIM))
    ang = pos[:, None] * inv[None, :]
    cos, sin = jnp.cos(ang), jnp.sin(ang)
    return jnp.concatenate([cos, cos], axis=-1), jnp.concatenate([sin, sin], axis=-1)


def _bias_table_kernel(vec_ref, o_ref):
    nk, q_tile = o_ref.shape
    rolled = pltpu.roll(jnp.broadcast_to(vec_ref[...], (nk, nk + q_tile)), 0, axis=1, stride=1, stride_axis=0)
    j = lax.broadcasted_iota(I32, (nk, q_tile), 0)
    q = lax.broadcasted_iota(I32, (nk, q_tile), 1)
    shift = CHUNK.bit_length() - 1
    dchunk = A_LEFT_CHUNKS + (q >> shift) - (j >> shift)
    valid = jnp.logical_and(dchunk >= 0, dchunk <= A_LEFT_CHUNKS)
    o_ref[...] = jnp.where(valid, rolled[:, nk:] * LOG2_E, -jnp.inf)


def _rel_bias_tables(rel_bias):
    depth = rel_bias.shape[0]
    nk = A_KEY_TILES * Q_TILE
    assert REL_CLIP == Q_TILE and CHUNK & (CHUNK - 1) == 0
    rb = rel_bias.astype(F32)
    vec = jnp.concatenate([rb, jnp.tile(rb[..., -1:], (1, 1, nk + Q_TILE - rb.shape[-1]))], axis=-1)
    return pl.pallas_call(
        _bias_table_kernel, grid=(depth, A_HEADS),
        in_specs=[pl.BlockSpec((None, None, 1, nk + Q_TILE), lambda l, h: (l, h, 0, 0))],
        out_specs=pl.BlockSpec((None, nk, Q_TILE), lambda l, h: (l, 0, h)),
        out_shape=jax.ShapeDtypeStruct((depth, nk, A_HEADS * Q_TILE), F32),
        compiler_params=_cparams("parallel", "parallel"), name="bias_table")(vec[:, :, None, :])


def kernel(x, g_mix, w_in, qk_gain_a, rel_bias, qk_gain_b, g_idx_k, conv_w, conv_b, lru_wa, lru_ba, lru_wx,
           lru_bx, lru_lambda, b_gate, w_branch, w_out, g_ffn, w_ffn_in, w_ffn_out):
    b, s, d = x.shape
    depth = g_mix.shape[0]
    assert s % B_KEY_TILE == 0 and d % LANES == 0
    tm = 512
    cos, sin = _rope_tables(s)
    half = (jnp.arange(HEAD_DIM) < HEAD_DIM // 2).astype(F32)
    cos_rm = jnp.tile(cos, (1, 4))
    sina_rm = jnp.tile(-sin * half[None, :], (1, 4))
    sinb_rm = jnp.tile(sin * (1.0 - half)[None, :], (1, 4))
    cosT, sinT = cos.T, sin.T
    hid = jnp.arange(256) // HEAD_DIM
    bd = (hid[:, None] == hid[None, :]).astype(BF16)
    rows = lambda v: v.reshape(depth, 1, -1).astype(F32)
    o_aq, o_ak, o_av, o_bq, o_bk, o_bv = 0, 256, 512, 768, 1024, 1280
    o_iq, o_ik, o_iw, o_cx, o_cy, o_gt = 1536, 1792, 1856, 1860, 2372, 2884
    blocks = jnp.arange(C_WIDTH) // (C_WIDTH // C_BLOCKS)
    bmask = (blocks[:, None] == blocks[None, :])

    def dense_blocks(wblk):
        return jnp.where(bmask, jnp.tile(wblk.reshape(depth, C_WIDTH, C_WIDTH // C_BLOCKS), (1, 1, C_BLOCKS)), 0.0)

    cols = lambda o, n: w_in[:, :, o:o + n]
    wrm = jnp.concatenate([cols(o_ak, 256), cols(o_bk, 256), cols(o_ik, 64), jnp.zeros((depth, d, 64), F32),
                           cols(o_cx, 512), cols(o_cy, 512)], axis=2).astype(BF16)
    wt = _transpose_cast(jnp.concatenate(
        [cols(o_aq, 256), cols(o_av, 256), cols(o_bq, 256), cols(o_bv, 256), cols(o_iq, 256), cols(o_iw, 4),
         jnp.zeros((depth, d, LANES - 4), F32)], axis=2))
    w_gate = w_in[:, :, o_gt:].astype(BF16)
    gka, gkb = rows(jnp.tile(qk_gain_a[:, 1], (1, 4))), rows(jnp.tile(qk_gain_b[:, 1], (1, 4)))
    gki = rows(jnp.concatenate([g_idx_k, jnp.zeros((depth, 64), F32)], axis=1))
    gqaT = jnp.tile(qk_gain_a[:, 0], (1, 4)).reshape(depth, 256, 1)
    gqbT = jnp.tile(qk_gain_b[:, 0], (1, 4)).reshape(depth, 256, 1)
    bias_tabs = _rel_bias_tables(rel_bias)
    wax = jnp.concatenate([dense_blocks(lru_wa), dense_blocks(lru_wx)], axis=2).astype(BF16)
    g_mix3, g_ffn3, b_gate3 = rows(g_mix), rows(g_ffn), rows(b_gate)
    cb3, ba3, bx3, lam3 = rows(conv_b), rows(lru_ba), rows(lru_bx), rows(lru_lambda)
    wb, wo = w_branch.astype(BF16), w_out.astype(BF16)
    wfi, wfo = w_ffn_in.astype(BF16), w_ffn_out.astype(BF16)

    n = b * s
    for l in range(depth):
        ka, kb, ki, cx, cy, qaT, vaT, qbT, vbT, qiT, iwT = _project(
            l, x, g_mix3, wrm, wt, bd, gka, gkb, gki, gqaT, gqbT, cos_rm, sina_rm, sinb_rm, cosT, sinT, tm)
        ya = _chunk_attention(l, qaT, ka, vaT, bias_tabs)
        yb = _sparse_attention(qiT, iwT, qbT, ki, kb, vbT)
        yc = _rglru(l, cx, cy, conv_w, cb3, wax, ba3, bx3, lam3, 512)
        x1 = _merge(l, x.reshape(n, d), ya.reshape(n, A_WIDTH), yb.reshape(n, B_WIDTH), yc.reshape(n, C_WIDTH),
                    g_mix3, w_gate, b_gate3, wb, wo, tm)
        x = _ffn(l, x1, g_ffn3, wfi, wfo, tm).reshape(b, s, d)
    return x
```

```python
import functools
import math

import jax
import jax.numpy as jnp
from jax import lax
from jax.experimental import pallas as pl
from jax.experimental.pallas import tpu as pltpu

F32 = jnp.float32
BF16 = jnp.bfloat16
I32 = jnp.int32
U32 = jnp.uint32

CHUNK = 64
HEAD_DIM = 64
EPS = 1e-6
ROPE_THETA = 10000.0
A_HEADS = 4
A_LEFT_CHUNKS = 8
REL_CLIP = 128
B_HEADS = 4
IDX_HEADS = 4
IDX_DIM = 64
TOPK_MAX = 256
C_WIDTH = 512
C_BLOCKS = 8
C_CONV = 4
LRU_C = 8.0
N_BRANCH = 3
A_WIDTH = A_HEADS * HEAD_DIM
B_WIDTH = B_HEADS * HEAD_DIM

LANES = 128
SUBLANES = 8
Q_TILE = 128
A_KEY_TILES = A_LEFT_CHUNKS * CHUNK // Q_TILE + 1
A_TILES_PER_STEP = 2
B_KEY_TILE = 512
PLANE_GROUP = 32 * SUBLANES
LOG2_E = math.log2(math.e)
INT_MIN = -2 ** 31
VMEM_LIMIT = 56 * 1024 * 1024

NT_DIMS = (((1,), (1,)), ((), ()))


def _cparams(*sem):
    return pltpu.CompilerParams(dimension_semantics=sem, vmem_limit_bytes=VMEM_LIMIT)


def _const_spec(shape):
    nd = len(shape)
    return pl.BlockSpec(shape, lambda *_: (0,) * nd, pipeline_mode=pl.Buffered(1))


def _layer_spec(stacked, layer):
    nd = stacked.ndim - 1
    return pl.BlockSpec((None,) + stacked.shape[1:], lambda *_: (layer,) + (0,) * nd, pipeline_mode=pl.Buffered(1))


def _rms(x, gain):
    ms = jnp.mean(x * x, axis=-1, keepdims=True)
    return x * lax.rsqrt(ms + EPS) * gain


def _proj_kernel(x_ref, g_ref, wrm_ref, wt_ref, bd_ref, gka_ref, gkb_ref, gki_ref, gqaT_ref, gqbT_ref,
                 cos_ref, sina_ref, sinb_ref, cosT_ref, sinT_ref,
                 ka_ref, kb_ref, ki_ref, cx_ref, cy_ref,
                 qaT_ref, vaT_ref, qbT_ref, vbT_ref, qiT_ref, iwT_ref):
    tm = x_ref.shape[1]
    h = _rms(x_ref[0], g_ref[...]).astype(BF16)

    def rm(c0, c1):
        return jnp.dot(h, wrm_ref[:, c0:c1], preferred_element_type=F32)

    def head_rms_rm(z, gain):
        ssq = jnp.dot((z * z).astype(BF16), bd_ref[...], preferred_element_type=F32)
        return z * lax.rsqrt(ssq * (1.0 / HEAD_DIM) + EPS) * gain

    def rope_rm(z):
        w = z.shape[1]
        zm = pltpu.roll(z, w - HEAD_DIM // 2, axis=1)
        zp = pltpu.roll(z, HEAD_DIM // 2, axis=1)
        return z * cos_ref[:, :w] + zm * sina_ref[:, :w] + zp * sinb_ref[:, :w]

    ka_ref[0] = head_rms_rm(rm(0, 256), gka_ref[...]).astype(BF16)
    kb_ref[0] = rope_rm(head_rms_rm(rm(256, 512), gkb_ref[...])).astype(BF16)
    zi = rm(512, 640)
    ms = jnp.sum(zi * zi, axis=-1, keepdims=True) * (1.0 / IDX_DIM)
    ki_ref[0] = rope_rm(zi * lax.rsqrt(ms + EPS) * gki_ref[...]).astype(BF16)
    cx_ref[0] = rm(640, 1152)
    cy_ref[0] = rm(1152, 1664)

    def tr(r0, r1):
        return lax.dot_general(wt_ref[r0:r1, :], h, NT_DIMS, preferred_element_type=F32)

    def head_rms_t(z, gain):
        z4 = z.reshape(4, HEAD_DIM, tm)
        ms4 = jnp.mean(z4 * z4, axis=1, keepdims=True)
        return (z4 * lax.rsqrt(ms4 + EPS)).reshape(4 * HEAD_DIM, tm) * gain

    def rope_t(z):
        z4 = z.reshape(4, HEAD_DIM, tm)
        rot = jnp.concatenate([-z4[:, HEAD_DIM // 2:, :], z4[:, :HEAD_DIM // 2, :]], axis=1)
        return (z4 * cosT_ref[...][None] + rot * sinT_ref[...][None]).reshape(4 * HEAD_DIM, tm)

    scale = HEAD_DIM ** -0.5 * LOG2_E
    qaT_ref[0] = (head_rms_t(tr(0, 256), gqaT_ref[...]) * scale).astype(BF16)
    vaT_ref[0] = tr(256, 512).astype(BF16)
    qbT_ref[0] = (rope_t(head_rms_t(tr(512, 768), gqbT_ref[...])) * scale).astype(BF16)
    vbT_ref[0] = tr(768, 1024).astype(BF16)
    qiT_ref[0] = rope_t(tr(1024, 1280)).astype(BF16)
    iwT_ref[0] = tr(1280, 1296)[0:SUBLANES] * (IDX_HEADS ** -0.5 * IDX_DIM ** -0.5)


def _project(layer, x, g, wrm, wt, bd, gka, gkb, gki, gqaT, gqbT, cos, sina, sinb, cosT, sinT, tm):
    b, s, d = x.shape
    grid = (b, s // tm)
    row = lambda w: pl.BlockSpec((1, tm, w), lambda bi, si: (bi, si, 0))
    col = lambda r: pl.BlockSpec((1, r, tm), lambda bi, si: (bi, 0, si))
    tab_rm = pl.BlockSpec((tm, 256), lambda bi, si: (si, 0))
    tab_t = pl.BlockSpec((HEAD_DIM, tm), lambda bi, si: (0, si))
    per_layer = lambda a: _layer_spec(a, layer)
    in_specs = [row(d), per_layer(g), per_layer(wrm), per_layer(wt), _const_spec(bd.shape),
                per_layer(gka), per_layer(gkb), per_layer(gki), per_layer(gqaT), per_layer(gqbT),
                tab_rm, tab_rm, tab_rm, tab_t, tab_t]
    out_shape = (jax.ShapeDtypeStruct((b, s, 256), BF16), jax.ShapeDtypeStruct((b, s, 256), BF16),
                 jax.ShapeDtypeStruct((b, s, 128), BF16),
                 jax.ShapeDtypeStruct((b, s, C_WIDTH), F32), jax.ShapeDtypeStruct((b, s, C_WIDTH), F32),
                 jax.ShapeDtypeStruct((b, 256, s), BF16), jax.ShapeDtypeStruct((b, 256, s), BF16),
                 jax.ShapeDtypeStruct((b, 256, s), BF16), jax.ShapeDtypeStruct((b, 256, s), BF16),
                 jax.ShapeDtypeStruct((b, 256, s), BF16), jax.ShapeDtypeStruct((b, 8, s), F32))
    out_specs = (row(256), row(256), row(128), row(C_WIDTH), row(C_WIDTH),
                 col(256), col(256), col(256), col(256), col(256), col(8))
    return pl.pallas_call(_proj_kernel, grid=grid, in_specs=in_specs, out_specs=out_specs, out_shape=out_shape,
                          compiler_params=_cparams("parallel", "parallel"), name="proj")(
        x, g, wrm, wt, bd, gka, gkb, gki, gqaT, gqbT, cos, sina, sinb, cosT, sinT)


def _transpose_cast_kernel(w_ref, o_ref):
    o_ref[...] = w_ref[...].T.astype(BF16)


def _transpose_cast(w):
    depth, d, n = w.shape
    return pl.pallas_call(
        _transpose_cast_kernel, grid=(depth, n // LANES),
        in_specs=[pl.BlockSpec((None, d, LANES), lambda l, i: (l, 0, i))],
        out_specs=pl.BlockSpec((None, LANES, d), lambda l, i: (l, i, 0)),
        out_shape=jax.ShapeDtypeStruct((depth, n, d), BF16),
        compiler_params=_cparams("parallel", "parallel"), name="transpose_cast")(w)


def _block_diag_q(qT):
    rows = lax.broadcasted_iota(I32, qT.shape, 0) // HEAD_DIM
    zero = jnp.zeros_like(qT)
    return jnp.concatenate([jnp.where(rows == hh, qT, zero) for hh in range(4)], axis=1)


def _chunk_attn_kernel(qT_ref, k_ref, vT_ref, bias_ref, o_ref):
    for sub in range(A_TILES_PER_STEP):
        _chunk_attn_tile(pl.program_id(1) * A_TILES_PER_STEP + sub,
                         qT_ref.at[0, :, sub * Q_TILE:(sub + 1) * Q_TILE], k_ref, vT_ref, bias_ref,
                         o_ref.at[0, sub * Q_TILE:(sub + 1) * Q_TILE, :])


def _chunk_attn_tile(i, qT_ref, k_ref, vT_ref, bias_ref, o_ref):
    qbd = _block_diag_q(qT_ref[...])
    ks, vs, pens = [], [], []
    for t in range(A_KEY_TILES):
        kt = i - (A_KEY_TILES - 1) + t
        r0 = pl.multiple_of(jnp.maximum(kt, 0) * Q_TILE, Q_TILE)
        ks.append(k_ref[0, pl.ds(r0, Q_TILE), :])
        vs.append(vT_ref[0, :, pl.ds(r0, Q_TILE)])
        pens.append(jnp.where(kt >= 0, 0.0, -jnp.inf).astype(F32))
    kwin = jnp.concatenate(ks, axis=0)
    vwin = jnp.concatenate(vs, axis=1)
    s = jnp.dot(kwin, qbd, preferred_element_type=F32) + bias_ref[...]
    s = jnp.concatenate([s[t * Q_TILE:(t + 1) * Q_TILE] + pens[t] for t in range(A_KEY_TILES)], axis=0)
    m = jnp.max(s, axis=0, keepdims=True)
    p = jnp.exp2(s - m)
    l = jnp.sum(p, axis=0, keepdims=True)
    pb = p.astype(BF16)
    outs = []
    for hh in range(A_HEADS):
        lo, hi = hh * Q_TILE, (hh + 1) * Q_TILE
        o = jnp.dot(vwin[hh * HEAD_DIM:(hh + 1) * HEAD_DIM, :], pb[:, lo:hi], preferred_element_type=F32)
        outs.append(o / l[:, lo:hi])
    o_ref[...] = jnp.concatenate(outs, axis=0).T.astype(BF16)


def _chunk_attention(layer, qaT, ka, vaT, biasT):
    b, _, s = qaT.shape
    step = A_TILES_PER_STEP * Q_TILE
    grid = (b, s // step)
    return pl.pallas_call(
        _chunk_attn_kernel, grid=grid,
        in_specs=[pl.BlockSpec((1, 256, step), lambda bi, i: (bi, 0, i)),
                  pl.BlockSpec((1, s, 256), lambda bi, i: (bi, 0, 0)),
                  pl.BlockSpec((1, 256, s), lambda bi, i: (bi, 0, 0)),
                  _layer_spec(biasT, layer)],
        out_specs=pl.BlockSpec((1, step, 256), lambda bi, i: (bi, i, 0)),
        out_shape=jax.ShapeDtypeStruct((b, s, 256), BF16),
        compiler_params=_cparams("parallel", "parallel"), name="chunk_attn")(qaT, ka, vaT, biasT)


def _bit_transpose32(words):
    a = list(words)
    mask, j = 0x0000FFFF, 16
    while j:
        k = 0
        while k < 32:
            t = (a[k] ^ (a[k + j] >> jnp.uint32(j))) & jnp.uint32(mask)
            a[k] = a[k] ^ t
            a[k + j] = a[k + j] ^ (t << jnp.uint32(j))
            k = (k + j + 1) & ~j
        j >>= 1
        mask = (mask ^ (mask << j)) & 0xFFFFFFFF
    return a


def _dsa_kernel(qiT_ref, iwT_ref, qbT_ref, ki_ref, kb_ref, vbT_ref, o_ref, keys_ref, planes_ref, *, topk):
    tk = B_KEY_TILE
    i = pl.program_id(1)
    n_small = lax.shift_right_logical(i + 2, 1)
    n_wide = lax.shift_right_logical(n_small, 2)
    has_mid, has_small = (n_small & 2) != 0, (n_small & 1) != 0
    qiT = qiT_ref[0]
    qi = jnp.concatenate([qiT[hh * IDX_DIM:(hh + 1) * IDX_DIM] for hh in range(IDX_HEADS)], axis=1)
    qi = jnp.concatenate([qi, jnp.zeros_like(qi)], axis=0)
    w = iwT_ref[0]
    lane = lax.broadcasted_iota(I32, (1, Q_TILE), 1)
    key_limit = i * Q_TILE + CHUNK + jnp.where(lane >= CHUNK, CHUNK, 0)

    def over_key_tiles(span, carry):
        wide, small = 2 * tk, tk // 2
        carry = lax.fori_loop(0, n_wide, lambda u, c: span(pl.multiple_of(u * wide, wide), wide, c), carry)
        r_mid = pl.multiple_of(n_wide * wide, wide)
        carry = lax.cond(has_mid, lambda c: span(r_mid, tk, c), lambda c: c, carry)
        r_small = pl.multiple_of(r_mid + jnp.where(has_mid, tk, 0), small)
        return lax.cond(has_small, lambda c: span(r_small, small, c), lambda c: c, carry)

    def score_span(r0, rows, c):
        dots = jnp.dot(ki_ref[0, pl.ds(r0, rows), :], qi, preferred_element_type=F32)
        sc = jnp.maximum(dots[:, 0:Q_TILE], 0.0) * w[0:1, :]
        for hh in range(1, IDX_HEADS):
            sc = sc + jnp.maximum(dots[:, hh * Q_TILE:(hh + 1) * Q_TILE], 0.0) * w[hh:hh + 1, :]
        sc = jnp.where(sc == 0.0, 0.0, sc)
        bits = lax.bitcast_convert_type(sc, I32)
        key = jnp.where(bits < 0, bits ^ jnp.int32(0x7FFFFFFF), bits)
        admissible = lax.broadcasted_iota(I32, (rows, Q_TILE), 0) < key_limit - r0
        key = jnp.where(admissible, key, jnp.int32(INT_MIN))
        keys_ref[pl.ds(r0, rows), :] = key
        ukey = lax.bitcast_convert_type(key, U32) ^ jnp.uint32(0x80000000)
        for g0 in range(0, rows, PLANE_GROUP):
            words = [ukey[g0 + j * SUBLANES:g0 + (j + 1) * SUBLANES] for j in range(32)]
            planes_ref[pl.ds(r0 + g0, PLANE_GROUP), :] = jnp.concatenate(_bit_transpose32(words), axis=0)
        return c

    @pl.when(i == 0)
    def _():
        planes_ref[...] = jnp.zeros_like(planes_ref)

    over_key_tiles(score_span, 0)

    n_groups = planes_ref.shape[0] // PLANE_GROUP
    full = jnp.full((SUBLANES, Q_TILE), 0xFFFFFFFF, U32)
    live0 = tuple(jnp.where(g < n_small, full, jnp.zeros_like(full)) for g in range(n_groups))

    def thr_bit(it, carry):
        live, n_gt, thr_bits = carry
        row = pl.multiple_of(it * SUBLANES, SUBLANES)
        ones = [live[g] & planes_ref[pl.ds(g * PLANE_GROUP + row, SUBLANES), :] for g in range(n_groups)]
        parts = [lax.population_count(o) for o in ones]
        while len(parts) > 1:
            parts = [parts[j] + parts[j + 1] for j in range(0, len(parts) - 1, 2)] + parts[len(parts) & ~1:]
        n_cand = n_gt + jnp.sum(parts[0].astype(I32), axis=0, keepdims=True)
        accept = n_cand >= topk
        live = tuple(jnp.where(accept, ones[g], live[g] ^ ones[g]) for g in range(n_groups))
        bit = jnp.int32(1) << (31 - it)
        return live, jnp.where(accept, n_gt, n_cand), thr_bits | jnp.where(accept, bit, 0)

    zero = jnp.zeros((1, Q_TILE), I32)
    _, n_gt, thr_bits = lax.fori_loop(0, 32, thr_bit, (live0, zero, zero))
    thr = thr_bits ^ jnp.int32(INT_MIN)

    short = thr == jnp.int32(INT_MIN)
    need = jnp.where(short, 0, topk - n_gt).astype(F32)
    hk = tk // 2
    tril = (lax.broadcasted_iota(I32, (hk, hk), 0) >= lax.broadcasted_iota(I32, (hk, hk), 1)).astype(BF16)

    qbd = _block_diag_q(qbT_ref[0])

    def attn_span(r0, rows, carry):
        ms, ls, accs, seen = carry
        key = keys_ref[pl.ds(r0, rows), :]
        tied = key == thr
        tied_b = jnp.where(tied, 1.0, 0.0).astype(BF16)
        ranks = []
        for c0 in range(0, rows, hk):
            ranks.append(jnp.dot(tril, tied_b[c0:c0 + hk], preferred_element_type=F32) + seen)
            seen = ranks[-1][hk - 1:hk, :]
        sel = jnp.logical_or(key > thr, jnp.logical_and(tied, jnp.concatenate(ranks, axis=0) <= need))
        s_all = jnp.dot(kb_ref[0, pl.ds(r0, rows), :], qbd, preferred_element_type=F32)
        new_ms, new_ls, new_accs = [], [], []
        for hh in range(B_HEADS):
            s = jnp.where(sel, s_all[:, hh * Q_TILE:(hh + 1) * Q_TILE], -jnp.inf)
            m_new = jnp.maximum(ms[hh], jnp.max(s, axis=0, keepdims=True))
            alpha = jnp.exp2(ms[hh] - m_new)
            p = jnp.exp2(s - m_new)
            new_ls.append(alpha * ls[hh] + jnp.sum(p, axis=0, keepdims=True))
            pv = jnp.dot(vbT_ref[0, hh * HEAD_DIM:(hh + 1) * HEAD_DIM, pl.ds(r0, rows)], p.astype(BF16),
                         preferred_element_type=F32)
            new_accs.append(alpha * accs[hh] + pv)
            new_ms.append(m_new)
        return tuple(new_ms), tuple(new_ls), tuple(new_accs), seen

    init = (tuple(jnp.full((1, Q_TILE), -1e30, F32) for _ in range(B_HEADS)),
            tuple(jnp.zeros((1, Q_TILE), F32) for _ in range(B_HEADS)),
            tuple(jnp.zeros((HEAD_DIM, Q_TILE), F32) for _ in range(B_HEADS)),
            jnp.zeros((1, Q_TILE), F32))
    _, ls, accs, _ = over_key_tiles(attn_span, init)
    out = jnp.concatenate([accs[hh] / ls[hh] for hh in range(B_HEADS)], axis=0)
    o_ref[0] = out.T.astype(BF16)


def _sparse_attention(qiT, iwT, qbT, ki, kb, vbT):
    b, _, s = qbT.shape
    s_pad = -(-s // B_KEY_TILE) * B_KEY_TILE
    topk = min(TOPK_MAX, s // 4)
    grid = (b, s // Q_TILE)
    qspec = lambda r: pl.BlockSpec((1, r, Q_TILE), lambda bi, i: (bi, 0, i))
    kernel = functools.partial(_dsa_kernel, topk=topk)
    return pl.pallas_call(
        kernel, grid=grid,
        in_specs=[qspec(256), qspec(8), qspec(256),
                  pl.BlockSpec((1, s, 128), lambda bi, i: (bi, 0, 0)),
                  pl.BlockSpec((1, s, 256), lambda bi, i: (bi, 0, 0)),
                  pl.BlockSpec((1, 256, s), lambda bi, i: (bi, 0, 0))],
        out_specs=pl.BlockSpec((1, Q_TILE, 256), lambda bi, i: (bi, i, 0)),
        out_shape=jax.ShapeDtypeStruct((b, s, 256), BF16),
        scratch_shapes=[pltpu.VMEM((s_pad, Q_TILE), I32), pltpu.VMEM((s_pad, Q_TILE), U32)],
        compiler_params=_cparams("parallel", "arbitrary"), name="sparse_attn")(qiT, iwT, qbT, ki, kb, vbT)


def _rglru_kernel(cx_ref, cy_ref, cw_ref, cb_ref, wax_ref, ba_ref, bx_ref, lam_ref, o_ref,
                  tail_ref, h_ref, a_ref, g_ref):
    ts = cx_ref.shape[1]

    @pl.when(pl.program_id(1) == 0)
    def _():
        tail_ref[...] = jnp.zeros_like(tail_ref)
        h_ref[...] = jnp.zeros_like(h_ref)

    x = cx_ref[0]
    tail = tail_ref[...]
    row8 = lax.broadcasted_iota(I32, (SUBLANES, C_WIDTH), 0)
    u = cb_ref[...] + x * cw_ref[C_CONV - 1:C_CONV, :]
    for k in range(1, C_CONV):
        xr = pltpu.roll(x, k, axis=0)
        head = jnp.where(row8 < k, pltpu.roll(tail, k, axis=0), xr[0:SUBLANES])
        xk = jnp.concatenate([head, xr[SUBLANES:]], axis=0)
        u = u + xk * cw_ref[C_CONV - 1 - k:C_CONV - k, :]
    tail_ref[...] = x[ts - SUBLANES:ts]

    gates = jnp.dot(u.astype(BF16), wax_ref[...], preferred_element_type=F32)
    r = jax.nn.sigmoid(gates[:, :C_WIDTH] + ba_ref[...])
    ig = jax.nn.sigmoid(gates[:, C_WIDTH:] + bx_ref[...])
    nlam = -lam_ref[...]
    softplus = jnp.maximum(nlam, 0.0) + jnp.log1p(jnp.exp(-jnp.abs(nlam)))
    a = jnp.exp(-LRU_C * r * softplus)
    a_ref[...] = a
    g_ref[...] = jnp.sqrt(1.0 - a * a) * (ig * u)

    def group(gi, hprev):
        r0 = pl.multiple_of(gi * SUBLANES, SUBLANES)
        av = a_ref[pl.ds(r0, SUBLANES), :]
        bv = g_ref[pl.ds(r0, SUBLANES), :]
        for d in (1, 2, 4):
            keep = row8 >= d
            a_sh = jnp.where(keep, pltpu.roll(av, d, axis=0), 1.0)
            b_sh = jnp.where(keep, pltpu.roll(bv, d, axis=0), 0.0)
            bv = av * b_sh + bv
            av = av * a_sh
        hs = av * hprev + bv
        g_ref[pl.ds(r0, SUBLANES), :] = hs
        return jnp.broadcast_to(hs[SUBLANES - 1:SUBLANES, :], hs.shape)

    h_ref[...] = lax.fori_loop(0, ts // SUBLANES, group, h_ref[...], unroll=8)
    o_ref[0] = (g_ref[...] * jax.nn.gelu(cy_ref[0])).astype(BF16)


def _rglru(layer, cx, cy, cw, cb, wax, ba, bx, lam, ts):
    b, s, c = cx.shape
    blk = pl.BlockSpec((1, ts, c), lambda bi, si: (bi, si, 0))
    return pl.pallas_call(
        _rglru_kernel, grid=(b, s // ts),
        in_specs=[blk, blk] + [_layer_spec(a, layer) for a in (cw, cb, wax, ba, bx, lam)],
        out_specs=blk, out_shape=jax.ShapeDtypeStruct((b, s, c), BF16),
        scratch_shapes=[pltpu.VMEM((SUBLANES, c), F32), pltpu.VMEM((SUBLANES, c), F32),
                        pltpu.VMEM((ts, c), F32), pltpu.VMEM((ts, c), F32)],
        compiler_params=_cparams("parallel", "arbitrary"), name="rglru")(cx, cy, cw, cb, wax, ba, bx, lam)


def _merge_kernel(x_ref, ya_ref, yb_ref, yc_ref, g_ref, wg_ref, bg_ref, wb_ref, wo_ref, o_ref):
    x = x_ref[...]
    d = x.shape[1]
    h = _rms(x, g_ref[...]).astype(BF16)
    ys = (ya_ref[...], yb_ref[...], yc_ref[...])
    offs = (0, A_WIDTH, A_WIDTH + B_WIDTH, A_WIDTH + B_WIDTH + C_WIDTH)
    merged = None
    for j in range(N_BRANCH):
        gate = jax.nn.sigmoid(jnp.dot(h, wg_ref[:, j * d:(j + 1) * d], preferred_element_type=F32)
                              + bg_ref[:, j * d:(j + 1) * d])
        term = gate * jnp.dot(ys[j], wb_ref[offs[j]:offs[j + 1], :], preferred_element_type=F32)
        merged = term if merged is None else merged + term
    o_ref[...] = x + jnp.dot(merged.astype(BF16), wo_ref[...], preferred_element_type=F32)


def _merge(layer, x2, ya, yb, yc, g, wg, bg, wb, wo, tm):
    n, d = x2.shape
    row = lambda w: pl.BlockSpec((tm, w), lambda i: (i, 0))
    return pl.pallas_call(
        _merge_kernel, grid=(n // tm,),
        in_specs=[row(d), row(A_WIDTH), row(B_WIDTH), row(C_WIDTH)]
        + [_layer_spec(a, layer) for a in (g, wg, bg, wb, wo)],
        out_specs=row(d), out_shape=jax.ShapeDtypeStruct((n, d), F32),
        compiler_params=_cparams("parallel"), name="merge")(x2, ya, yb, yc, g, wg, bg, wb, wo)


def _ffn_kernel(x_ref, g_ref, wi_ref, wo_ref, o_ref, *, d_ff, chunk):
    x = x_ref[...]
    h = _rms(x, g_ref[...]).astype(BF16)
    acc = x
    for c0 in range(0, d_ff, chunk):
        c1 = min(c0 + chunk, d_ff)
        gp = jnp.dot(h, wi_ref[:, c0:c1], preferred_element_type=F32)
        up = jnp.dot(h, wi_ref[:, d_ff + c0:d_ff + c1], preferred_element_type=F32)
        act = (jax.nn.silu(gp) * up).astype(BF16)
        acc = acc + jnp.dot(act, wo_ref[c0:c1, :], preferred_element_type=F32)
    o_ref[...] = acc


def _ffn(layer, x2, g, wi, wo, tm):
    n, d = x2.shape
    d_ff = wo.shape[1]
    row = pl.BlockSpec((tm, d), lambda i: (i, 0))
    kernel = functools.partial(_ffn_kernel, d_ff=d_ff, chunk=1024)
    return pl.pallas_call(
        kernel, grid=(n // tm,),
        in_specs=[row] + [_layer_spec(a, layer) for a in (g, wi, wo)],
        out_specs=row, out_shape=jax.ShapeDtypeStruct((n, d), F32),
        compiler_params=_cparams("parallel"), name="ffn")(x2, g, wi, wo)


def _rope_tables(s):
    pos = jnp.arange(s, dtype=F32)
    inv = 1.0 / (ROPE_THETA ** (jnp.arange(0, HEAD_DIM, 2, dtype=F32) / HEAD_DIM))
    ang = pos[:, None] * inv[None, :]
    cos, sin = jnp.cos(ang), jnp.sin(ang)
    return jnp.concatenate([cos, cos], axis=-1), jnp.concatenate([sin, sin], axis=-1)


def _bias_table_kernel(vec_ref, o_ref):
    nk, q_tile = o_ref.shape
    rolled = pltpu.roll(jnp.broadcast_to(vec_ref[...], (nk, nk + q_tile)), 0, axis=1, stride=1, stride_axis=0)
    j = lax.broadcasted_iota(I32, (nk, q_tile), 0)
    q = lax.broadcasted_iota(I32, (nk, q_tile), 1)
    shift = CHUNK.bit_length() - 1
    dchunk = A_LEFT_CHUNKS + (q >> shift) - (j >> shift)
    valid = jnp.logical_and(dchunk >= 0, dchunk <= A_LEFT_CHUNKS)
    o_ref[...] = jnp.where(valid, rolled[:, nk:] * LOG2_E, -jnp.inf)


def _rel_bias_tables(rel_bias):
    depth = rel_bias.shape[0]
    nk = A_KEY_TILES * Q_TILE
    assert REL_CLIP == Q_TILE and CHUNK & (CHUNK - 1) == 0
    rb = rel_bias.astype(F32)
    vec = jnp.concatenate([rb, jnp.tile(rb[..., -1:], (1, 1, nk + Q_TILE - rb.shape[-1]))], axis=-1)
    return pl.pallas_call(
        _bias_table_kernel, grid=(depth, A_HEADS),
        in_specs=[pl.BlockSpec((None, None, 1, nk + Q_TILE), lambda l, h: (l, h, 0, 0))],
        out_specs=pl.BlockSpec((None, nk, Q_TILE), lambda l, h: (l, 0, h)),
        out_shape=jax.ShapeDtypeStruct((depth, nk, A_HEADS * Q_TILE), F32),
        compiler_params=_cparams("parallel", "parallel"), name="bias_table")(vec[:, :, None, :])


def kernel(x, g_mix, w_in, qk_gain_a, rel_bias, qk_gain_b, g_idx_k, conv_w, conv_b, lru_wa, lru_ba, lru_wx,
           lru_bx, lru_lambda, b_gate, w_branch, w_out, g_ffn, w_ffn_in, w_ffn_out):
    b, s, d = x.shape
    depth = g_mix.shape[0]
    assert s % B_KEY_TILE == 0 and d % LANES == 0
    tm = 512
    cos, sin = _rope_tables(s)
    half = (jnp.arange(HEAD_DIM) < HEAD_DIM // 2).astype(F32)
    cos_rm = jnp.tile(cos, (1, 4))
    sina_rm = jnp.tile(-sin * half[None, :], (1, 4))
    sinb_rm = jnp.tile(sin * (1.0 - half)[None, :], (1, 4))
    cosT, sinT = cos.T, sin.T
    hid = jnp.arange(256) // HEAD_DIM
    bd = (hid[:, None] == hid[None, :]).astype(BF16)
    rows = lambda v: v.reshape(depth, 1, -1).astype(F32)
    o_aq, o_ak, o_av, o_bq, o_bk, o_bv = 0, 256, 512, 768, 1024, 1280
    o_iq, o_ik, o_iw, o_cx, o_cy, o_gt = 1536, 1792, 1856, 1860, 2372, 2884
    blocks = jnp.arange(C_WIDTH) // (C_WIDTH // C_BLOCKS)
    bmask = (blocks[:, None] == blocks[None, :])

    def dense_blocks(wblk):
        return jnp.where(bmask, jnp.tile(wblk.reshape(depth, C_WIDTH, C_WIDTH // C_BLOCKS), (1, 1, C_BLOCKS)), 0.0)

    cols = lambda o, n: w_in[:, :, o:o + n]
    wrm = jnp.concatenate([cols(o_ak, 256), cols(o_bk, 256), cols(o_ik, 64), jnp.zeros((depth, d, 64), F32),
                           cols(o_cx, 512), cols(o_cy, 512)], axis=2).astype(BF16)
    wt = _transpose_cast(jnp.concatenate(
        [cols(o_aq, 256), cols(o_av, 256), cols(o_bq, 256), cols(o_bv, 256), cols(o_iq, 256), cols(o_iw, 4),
         jnp.zeros((depth, d, LANES - 4), F32)], axis=2))
    w_gate = w_in[:, :, o_gt:].astype(BF16)
    gka, gkb = rows(jnp.tile(qk_gain_a[:, 1], (1, 4))), rows(jnp.tile(qk_gain_b[:, 1], (1, 4)))
    gki = rows(jnp.concatenate([g_idx_k, jnp.zeros((depth, 64), F32)], axis=1))
    gqaT = jnp.tile(qk_gain_a[:, 0], (1, 4)).reshape(depth, 256, 1)
    gqbT = jnp.tile(qk_gain_b[:, 0], (1, 4)).reshape(depth, 256, 1)
    bias_tabs = _rel_bias_tables(rel_bias)
    wax = jnp.concatenate([dense_blocks(lru_wa), dense_blocks(lru_wx)], axis=2).astype(BF16)
    g_mix3, g_ffn3, b_gate3 = rows(g_mix), rows(g_ffn), rows(b_gate)
    cb3, ba3, bx3, lam3 = rows(conv_b), rows(lru_ba), rows(lru_bx), rows(lru_lambda)
    wb, wo = w_branch.astype(BF16), w_out.astype(BF16)
    wfi, wfo = w_ffn_in.astype(BF16), w_ffn_out.astype(BF16)

    n = b * s
    for l in range(depth):
        ka, kb, ki, cx, cy, qaT, vaT, qbT, vbT, qiT, iwT = _project(
            l, x, g_mix3, wrm, wt, bd, gka, gkb, gki, gqaT, gqbT, cos_rm, sina_rm, sinb_rm, cosT, sinT, tm)
        ya = _chunk_attention(l, qaT, ka, vaT, bias_tabs)
        yb = _sparse_attention(qiT, iwT, qbT, ki, kb, vbT)
        yc = _rglru(l, cx, cy, conv_w, cb3, wax, ba3, bx3, lam3, 512)
        x1 = _merge(l, x.reshape(n, d), ya.reshape(n, A_WIDTH), yb.reshape(n, B_WIDTH), yc.reshape(n, C_WIDTH),
                    g_mix3, w_gate, b_gate3, wb, wo, tm)
        x = _ffn(l, x1, g_ffn3, wfi, wfo, tm).reshape(b, s, d)
    return x
```

```python
import functools
import math

import jax
import jax.numpy as jnp
from jax import lax
from jax.experimental import pallas as pl
from jax.experimental.pallas import tpu as pltpu

F32 = jnp.float32
BF16 = jnp.bfloat16
I32 = jnp.int32
U32 = jnp.uint32

CHUNK = 64
HEAD_DIM = 64
EPS = 1e-6
ROPE_THETA = 10000.0
A_HEADS = 4
A_LEFT_CHUNKS = 8
REL_CLIP = 128
B_HEADS = 4
IDX_HEADS = 4
IDX_DIM = 64
TOPK_MAX = 256
C_WIDTH = 512
C_BLOCKS = 8
C_CONV = 4
LRU_C = 8.0
N_BRANCH = 3
A_WIDTH = A_HEADS * HEAD_DIM
B_WIDTH = B_HEADS * HEAD_DIM

LANES = 128
SUBLANES = 8
Q_TILE = 128
A_KEY_TILES = A_LEFT_CHUNKS * CHUNK // Q_TILE + 1
A_TILES_PER_STEP = 4
B_KEY_TILE = 512
PLANE_GROUP = 32 * SUBLANES
LOG2_E = math.log2(math.e)
INT_MIN = -2 ** 31
VMEM_LIMIT = 56 * 1024 * 1024

NT_DIMS = (((1,), (1,)), ((), ()))


def _cparams(*sem):
    return pltpu.CompilerParams(dimension_semantics=sem, vmem_limit_bytes=VMEM_LIMIT)


def _const_spec(shape):
    nd = len(shape)
    return pl.BlockSpec(shape, lambda *_: (0,) * nd, pipeline_mode=pl.Buffered(1))


def _layer_spec(stacked, layer):
    nd = stacked.ndim - 1
    return pl.BlockSpec((None,) + stacked.shape[1:], lambda *_: (layer,) + (0,) * nd, pipeline_mode=pl.Buffered(1))


def _rms(x, gain):
    ms = jnp.mean(x * x, axis=-1, keepdims=True)
    return x * lax.rsqrt(ms + EPS) * gain


def _proj_kernel(x_ref, g_ref, wrm_ref, wt_ref, bd_ref, gka_ref, gkb_ref, gki_ref, gqaT_ref, gqbT_ref,
                 cos_ref, sina_ref, sinb_ref, cosT_ref, sinT_ref,
                 ka_ref, kb_ref, ki_ref, cx_ref, cy_ref,
                 qaT_ref, vaT_ref, qbT_ref, vbT_ref, qiT_ref, iwT_ref):
    tm = x_ref.shape[1]
    h = _rms(x_ref[0], g_ref[...]).astype(BF16)

    def rm(c0, c1):
        return jnp.dot(h, wrm_ref[:, c0:c1], preferred_element_type=F32)

    def head_rms_rm(z, gain):
        ssq = jnp.dot((z * z).astype(BF16), bd_ref[...], preferred_element_type=F32)
        return z * lax.rsqrt(ssq * (1.0 / HEAD_DIM) + EPS) * gain

    def rope_rm(z):
        w = z.shape[1]
        zm = pltpu.roll(z, w - HEAD_DIM // 2, axis=1)
        zp = pltpu.roll(z, HEAD_DIM // 2, axis=1)
        return z * cos_ref[:, :w] + zm * sina_ref[:, :w] + zp * sinb_ref[:, :w]

    ka_ref[0] = head_rms_rm(rm(0, 256), gka_ref[...]).astype(BF16)
    kb_ref[0] = rope_rm(head_rms_rm(rm(256, 512), gkb_ref[...])).astype(BF16)
    zi = rm(512, 640)
    ms = jnp.sum(zi * zi, axis=-1, keepdims=True) * (1.0 / IDX_DIM)
    ki_ref[0] = rope_rm(zi * lax.rsqrt(ms + EPS) * gki_ref[...]).astype(BF16)
    cx_ref[0] = rm(640, 1152)
    cy_ref[0] = rm(1152, 1664)

    def tr(r0, r1):
        return lax.dot_general(wt_ref[r0:r1, :], h, NT_DIMS, preferred_element_type=F32)

    def head_rms_t(z, gain):
        z4 = z.reshape(4, HEAD_DIM, tm)
        ms4 = jnp.mean(z4 * z4, axis=1, keepdims=True)
        return (z4 * lax.rsqrt(ms4 + EPS)).reshape(4 * HEAD_DIM, tm) * gain

    def rope_t(z):
        z4 = z.reshape(4, HEAD_DIM, tm)
        rot = jnp.concatenate([-z4[:, HEAD_DIM // 2:, :], z4[:, :HEAD_DIM // 2, :]], axis=1)
        return (z4 * cosT_ref[...][None] + rot * sinT_ref[...][None]).reshape(4 * HEAD_DIM, tm)

    scale = HEAD_DIM ** -0.5 * LOG2_E
    qaT_ref[0] = (head_rms_t(tr(0, 256), gqaT_ref[...]) * scale).astype(BF16)
    vaT_ref[0] = tr(256, 512).astype(BF16)
    qbT_ref[0] = (rope_t(head_rms_t(tr(512, 768), gqbT_ref[...])) * scale).astype(BF16)
    vbT_ref[0] = tr(768, 1024).astype(BF16)
    qiT_ref[0] = rope_t(tr(1024, 1280)).astype(BF16)
    iwT_ref[0] = tr(1280, 1296)[0:SUBLANES] * (IDX_HEADS ** -0.5 * IDX_DIM ** -0.5)


def _project(layer, x, g, wrm, wt, bd, gka, gkb, gki, gqaT, gqbT, cos, sina, sinb, cosT, sinT, tm):
    b, s, d = x.shape
    grid = (b, s // tm)
    row = lambda w: pl.BlockSpec((1, tm, w), lambda bi, si: (bi, si, 0))
    col = lambda r: pl.BlockSpec((1, r, tm), lambda bi, si: (bi, 0, si))
    tab_rm = pl.BlockSpec((tm, 256), lambda bi, si: (si, 0))
    tab_t = pl.BlockSpec((HEAD_DIM, tm), lambda bi, si: (0, si))
    per_layer = lambda a: _layer_spec(a, layer)
    in_specs = [row(d), per_layer(g), per_layer(wrm), per_layer(wt), _const_spec(bd.shape),
                per_layer(gka), per_layer(gkb), per_layer(gki), per_layer(gqaT), per_layer(gqbT),
                tab_rm, tab_rm, tab_rm, tab_t, tab_t]
    out_shape = (jax.ShapeDtypeStruct((b, s, 256), BF16), jax.ShapeDtypeStruct((b, s, 256), BF16),
                 jax.ShapeDtypeStruct((b, s, 128), BF16),
                 jax.ShapeDtypeStruct((b, s, C_WIDTH), F32), jax.ShapeDtypeStruct((b, s, C_WIDTH), F32),
                 jax.ShapeDtypeStruct((b, 256, s), BF16), jax.ShapeDtypeStruct((b, 256, s), BF16),
                 jax.ShapeDtypeStruct((b, 256, s), BF16), jax.ShapeDtypeStruct((b, 256, s), BF16),
                 jax.ShapeDtypeStruct((b, 256, s), BF16), jax.ShapeDtypeStruct((b, 8, s), F32))
    out_specs = (row(256), row(256), row(128), row(C_WIDTH), row(C_WIDTH),
                 col(256), col(256), col(256), col(256), col(256), col(8))
    return pl.pallas_call(_proj_kernel, grid=grid, in_specs=in_specs, out_specs=out_specs, out_shape=out_shape,
                          compiler_params=_cparams("parallel", "parallel"), name="proj")(
        x, g, wrm, wt, bd, gka, gkb, gki, gqaT, gqbT, cos, sina, sinb, cosT, sinT)


def _transpose_cast_kernel(w_ref, o_ref):
    o_ref[...] = w_ref[...].T.astype(BF16)


def _transpose_cast(w):
    depth, d, n = w.shape
    return pl.pallas_call(
        _transpose_cast_kernel, grid=(depth, n // LANES),
        in_specs=[pl.BlockSpec((None, d, LANES), lambda l, i: (l, 0, i))],
        out_specs=pl.BlockSpec((None, LANES, d), lambda l, i: (l, i, 0)),
        out_shape=jax.ShapeDtypeStruct((depth, n, d), BF16),
        compiler_params=_cparams("parallel", "parallel"), name="transpose_cast")(w)


def _block_diag_q(qT):
    rows = lax.broadcasted_iota(I32, qT.shape, 0) // HEAD_DIM
    zero = jnp.zeros_like(qT)
    return jnp.concatenate([jnp.where(rows == hh, qT, zero) for hh in range(4)], axis=1)


def _chunk_attn_kernel(qT_ref, k_ref, vT_ref, bias_ref, o_ref):
    for sub in range(A_TILES_PER_STEP):
        _chunk_attn_tile(pl.program_id(1) * A_TILES_PER_STEP + sub,
                         qT_ref.at[0, :, sub * Q_TILE:(sub + 1) * Q_TILE], k_ref, vT_ref, bias_ref,
                         o_ref.at[0, sub * Q_TILE:(sub + 1) * Q_TILE, :])


def _chunk_attn_tile(i, qT_ref, k_ref, vT_ref, bias_ref, o_ref):
    qbd = _block_diag_q(qT_ref[...])
    ks, vs, pens = [], [], []
    for t in range(A_KEY_TILES):
        kt = i - (A_KEY_TILES - 1) + t
        r0 = pl.multiple_of(jnp.maximum(kt, 0) * Q_TILE, Q_TILE)
        ks.append(k_ref[0, pl.ds(r0, Q_TILE), :])
        vs.append(vT_ref[0, :, pl.ds(r0, Q_TILE)])
        pens.append(jnp.where(kt >= 0, 0.0, -jnp.inf).astype(F32))
    kwin = jnp.concatenate(ks, axis=0)
    vwin = jnp.concatenate(vs, axis=1)
    s = jnp.dot(kwin, qbd, preferred_element_type=F32) + bias_ref[...]
    s = jnp.concatenate([s[t * Q_TILE:(t + 1) * Q_TILE] + pens[t] for t in range(A_KEY_TILES)], axis=0)
    m = jnp.max(s, axis=0, keepdims=True)
    p = jnp.exp2(s - m)
    l = jnp.sum(p, axis=0, keepdims=True)
    pb = p.astype(BF16)
    outs = []
    for hh in range(A_HEADS):
        lo, hi = hh * Q_TILE, (hh + 1) * Q_TILE
        o = jnp.dot(vwin[hh * HEAD_DIM:(hh + 1) * HEAD_DIM, :], pb[:, lo:hi], preferred_element_type=F32)
        outs.append(o / l[:, lo:hi])
    o_ref[...] = jnp.concatenate(outs, axis=0).T.astype(BF16)


def _chunk_attention(layer, qaT, ka, vaT, biasT):
    b, _, s = qaT.shape
    step = A_TILES_PER_STEP * Q_TILE
    grid = (b, s // step)
    return pl.pallas_call(
        _chunk_attn_kernel, grid=grid,
        in_specs=[pl.BlockSpec((1, 256, step), lambda bi, i: (bi, 0, i)),
                  pl.BlockSpec((1, s, 256), lambda bi, i: (bi, 0, 0)),
                  pl.BlockSpec((1, 256, s), lambda bi, i: (bi, 0, 0)),
                  _layer_spec(biasT, layer)],
        out_specs=pl.BlockSpec((1, step, 256), lambda bi, i: (bi, i, 0)),
        out_shape=jax.ShapeDtypeStruct((b, s, 256), BF16),
        compiler_params=_cparams("parallel", "parallel"), name="chunk_attn")(qaT, ka, vaT, biasT)


def _bit_transpose32(words):
    a = list(words)
    mask, j = 0x0000FFFF, 16
    while j:
        k = 0
        while k < 32:
            t = (a[k] ^ (a[k + j] >> jnp.uint32(j))) & jnp.uint32(mask)
            a[k] = a[k] ^ t
            a[k + j] = a[k + j] ^ (t << jnp.uint32(j))
            k = (k + j + 1) & ~j
        j >>= 1
        mask = (mask ^ (mask << j)) & 0xFFFFFFFF
    return a


def _dsa_kernel(qiT_ref, iwT_ref, qbT_ref, ki_ref, kb_ref, vbT_ref, o_ref, keys_ref, planes_ref, *, topk):
    tk = B_KEY_TILE
    i = pl.program_id(1)
    n_small = lax.shift_right_logical(i + 2, 1)
    n_wide = lax.shift_right_logical(n_small, 2)
    has_mid, has_small = (n_small & 2) != 0, (n_small & 1) != 0
    qiT = qiT_ref[0]
    qi = jnp.concatenate([qiT[hh * IDX_DIM:(hh + 1) * IDX_DIM] for hh in range(IDX_HEADS)], axis=1)
    qi = jnp.concatenate([qi, jnp.zeros_like(qi)], axis=0)
    w = iwT_ref[0]
    lane = lax.broadcasted_iota(I32, (1, Q_TILE), 1)
    key_limit = i * Q_TILE + CHUNK + jnp.where(lane >= CHUNK, CHUNK, 0)

    def over_key_tiles(span, carry):
        wide, small = 2 * tk, tk // 2
        carry = lax.fori_loop(0, n_wide, lambda u, c: span(pl.multiple_of(u * wide, wide), wide, c), carry)
        r_mid = pl.multiple_of(n_wide * wide, wide)
        carry = lax.cond(has_mid, lambda c: span(r_mid, tk, c), lambda c: c, carry)
        r_small = pl.multiple_of(r_mid + jnp.where(has_mid, tk, 0), small)
        return lax.cond(has_small, lambda c: span(r_small, small, c), lambda c: c, carry)

    def score_span(r0, rows, c):
        dots = jnp.dot(ki_ref[0, pl.ds(r0, rows), :], qi, preferred_element_type=F32)
        sc = jnp.maximum(dots[:, 0:Q_TILE], 0.0) * w[0:1, :]
        for hh in range(1, IDX_HEADS):
            sc = sc + jnp.maximum(dots[:, hh * Q_TILE:(hh + 1) * Q_TILE], 0.0) * w[hh:hh + 1, :]
        sc = jnp.where(sc == 0.0, 0.0, sc)
        bits = lax.bitcast_convert_type(sc, I32)
        key = jnp.where(bits < 0, bits ^ jnp.int32(0x7FFFFFFF), bits)
        admissible = lax.broadcasted_iota(I32, (rows, Q_TILE), 0) < key_limit - r0
        key = jnp.where(admissible, key, jnp.int32(INT_MIN))
        keys_ref[pl.ds(r0, rows), :] = key
        ukey = lax.bitcast_convert_type(key, U32) ^ jnp.uint32(0x80000000)
        for g0 in range(0, rows, PLANE_GROUP):
            words = [ukey[g0 + j * SUBLANES:g0 + (j + 1) * SUBLANES] for j in range(32)]
            planes_ref[pl.ds(r0 + g0, PLANE_GROUP), :] = jnp.concatenate(_bit_transpose32(words), axis=0)
        return c

    @pl.when(i == 0)
    def _():
        planes_ref[...] = jnp.zeros_like(planes_ref)

    over_key_tiles(score_span, 0)

    max_groups = planes_ref.shape[0] // PLANE_GROUP
    full = jnp.full((SUBLANES, Q_TILE), 0xFFFFFFFF, U32)
    zero = jnp.zeros((1, Q_TILE), I32)

    def search(n_groups):
        live0 = tuple(jnp.where(g < n_small, full, jnp.zeros_like(full)) for g in range(n_groups))

        def members(masks):
            parts = [lax.population_count(m) for m in masks]
            while len(parts) > 1:
                parts = [parts[j] + parts[j + 1] for j in range(0, len(parts) - 1, 2)] + parts[len(parts) & ~1:]
            return jnp.sum(parts[0].astype(I32), axis=0, keepdims=True)

        def two_bits(it, carry):
            live, n_gt, thr_bits = carry
            row = pl.multiple_of(it * (2 * SUBLANES), 2 * SUBLANES)
            p_hi = [planes_ref[pl.ds(g * PLANE_GROUP + row, SUBLANES), :] for g in range(n_groups)]
            p_lo = [planes_ref[pl.ds(g * PLANE_GROUP + row + SUBLANES, SUBLANES), :] for g in range(n_groups)]
            set_hi = [live[g] & p_hi[g] for g in range(n_groups)]
            clr_hi = [live[g] ^ set_hi[g] for g in range(n_groups)]
            set_both = [set_hi[g] & p_lo[g] for g in range(n_groups)]
            clr_set = [clr_hi[g] & p_lo[g] for g in range(n_groups)]
            n_hi = n_gt + members(set_hi)
            take_hi = n_hi >= topk
            n_gt = jnp.where(take_hi, n_gt, n_hi)
            n_lo = n_gt + jnp.where(take_hi, members(set_both), members(clr_set))
            take_lo = n_lo >= topk
            new_live = []
            for g in range(n_groups):
                kept = jnp.where(take_hi, set_hi[g], clr_hi[g])
                low = jnp.where(take_hi, set_both[g], clr_set[g])
                new_live.append(jnp.where(take_lo, low, kept ^ low))
            bit_hi = jnp.int32(1) << (31 - 2 * it)
            bit_lo = lax.shift_right_logical(bit_hi, 1)
            thr_bits = thr_bits | jnp.where(take_hi, bit_hi, 0) | jnp.where(take_lo, bit_lo, 0)
            return tuple(new_live), jnp.where(take_lo, n_gt, n_lo), thr_bits

        _, n_gt, thr_bits = lax.fori_loop(0, 16, two_bits, (live0, zero, zero))
        return n_gt, thr_bits

    if max_groups % 2 == 0:
        n_gt, thr_bits = lax.cond(n_small <= max_groups // 2, lambda: search(max_groups // 2), lambda: search(max_groups))
    else:
        n_gt, thr_bits = search(max_groups)
    thr = thr_bits ^ jnp.int32(INT_MIN)

    short = thr == jnp.int32(INT_MIN)
    need = jnp.where(short, 0, topk - n_gt).astype(F32)
    hk = tk // 2
    tril = (lax.broadcasted_iota(I32, (hk, hk), 0) >= lax.broadcasted_iota(I32, (hk, hk), 1)).astype(BF16)

    qbd = _block_diag_q(qbT_ref[0])

    def attn_span(r0, rows, carry):
        ms, ls, accs, seen = carry
        key = keys_ref[pl.ds(r0, rows), :]
        tied = key == thr
        tied_b = jnp.where(tied, 1.0, 0.0).astype(BF16)
        ranks = []
        for c0 in range(0, rows, hk):
            ranks.append(jnp.dot(tril, tied_b[c0:c0 + hk], preferred_element_type=F32) + seen)
            seen = ranks[-1][hk - 1:hk, :]
        sel = jnp.logical_or(key > thr, jnp.logical_and(tied, jnp.concatenate(ranks, axis=0) <= need))
        s_all = jnp.dot(kb_ref[0, pl.ds(r0, rows), :], qbd, preferred_element_type=F32)
        new_ms, new_ls, new_accs = [], [], []
        for hh in range(B_HEADS):
            s = jnp.where(sel, s_all[:, hh * Q_TILE:(hh + 1) * Q_TILE], -jnp.inf)
            m_new = jnp.maximum(ms[hh], jnp.max(s, axis=0, keepdims=True))
            alpha = jnp.exp2(ms[hh] - m_new)
            p = jnp.exp2(s - m_new)
            new_ls.append(alpha * ls[hh] + jnp.sum(p, axis=0, keepdims=True))
            pv = jnp.dot(vbT_ref[0, hh * HEAD_DIM:(hh + 1) * HEAD_DIM, pl.ds(r0, rows)], p.astype(BF16),
                         preferred_element_type=F32)
            new_accs.append(alpha * accs[hh] + pv)
            new_ms.append(m_new)
        return tuple(new_ms), tuple(new_ls), tuple(new_accs), seen

    init = (tuple(jnp.full((1, Q_TILE), -1e30, F32) for _ in range(B_HEADS)),
            tuple(jnp.zeros((1, Q_TILE), F32) for _ in range(B_HEADS)),
            tuple(jnp.zeros((HEAD_DIM, Q_TILE), F32) for _ in range(B_HEADS)),
            jnp.zeros((1, Q_TILE), F32))
    _, ls, accs, _ = over_key_tiles(attn_span, init)
    out = jnp.concatenate([accs[hh] / ls[hh] for hh in range(B_HEADS)], axis=0)
    o_ref[0] = out.T.astype(BF16)


def _sparse_attention(qiT, iwT, qbT, ki, kb, vbT):
    b, _, s = qbT.shape
    s_pad = -(-s // B_KEY_TILE) * B_KEY_TILE
    topk = min(TOPK_MAX, s // 4)
    grid = (b, s // Q_TILE)
    qspec = lambda r: pl.BlockSpec((1, r, Q_TILE), lambda bi, i: (bi, 0, i))
    kernel = functools.partial(_dsa_kernel, topk=topk)
    return pl.pallas_call(
        kernel, grid=grid,
        in_specs=[qspec(256), qspec(8), qspec(256),
                  pl.BlockSpec((1, s, 128), lambda bi, i: (bi, 0, 0)),
                  pl.BlockSpec((1, s, 256), lambda bi, i: (bi, 0, 0)),
                  pl.BlockSpec((1, 256, s), lambda bi, i: (bi, 0, 0))],
        out_specs=pl.BlockSpec((1, Q_TILE, 256), lambda bi, i: (bi, i, 0)),
        out_shape=jax.ShapeDtypeStruct((b, s, 256), BF16),
        scratch_shapes=[pltpu.VMEM((s_pad, Q_TILE), I32), pltpu.VMEM((s_pad, Q_TILE), U32)],
        compiler_params=_cparams("parallel", "arbitrary"), name="sparse_attn")(qiT, iwT, qbT, ki, kb, vbT)


def _rglru_kernel(cx_ref, cy_ref, cw_ref, cb_ref, wax_ref, ba_ref, bx_ref, lam_ref, o_ref,
                  tail_ref, h_ref, a_ref, g_ref):
    ts = cx_ref.shape[1]

    @pl.when(pl.program_id(1) == 0)
    def _():
        tail_ref[...] = jnp.zeros_like(tail_ref)
        h_ref[...] = jnp.zeros_like(h_ref)

    x = cx_ref[0]
    tail = tail_ref[...]
    row8 = lax.broadcasted_iota(I32, (SUBLANES, C_WIDTH), 0)
    u = cb_ref[...] + x * cw_ref[C_CONV - 1:C_CONV, :]
    for k in range(1, C_CONV):
        xr = pltpu.roll(x, k, axis=0)
        head = jnp.where(row8 < k, pltpu.roll(tail, k, axis=0), xr[0:SUBLANES])
        xk = jnp.concatenate([head, xr[SUBLANES:]], axis=0)
        u = u + xk * cw_ref[C_CONV - 1 - k:C_CONV - k, :]
    tail_ref[...] = x[ts - SUBLANES:ts]

    gates = jnp.dot(u.astype(BF16), wax_ref[...], preferred_element_type=F32)
    r = jax.nn.sigmoid(gates[:, :C_WIDTH] + ba_ref[...])
    ig = jax.nn.sigmoid(gates[:, C_WIDTH:] + bx_ref[...])
    nlam = -lam_ref[...]
    softplus = jnp.maximum(nlam, 0.0) + jnp.log1p(jnp.exp(-jnp.abs(nlam)))
    a = jnp.exp(-LRU_C * r * softplus)
    a_ref[...] = a
    g_ref[...] = jnp.sqrt(1.0 - a * a) * (ig * u)

    def group(gi, hprev):
        r0 = pl.multiple_of(gi * SUBLANES, SUBLANES)
        av = a_ref[pl.ds(r0, SUBLANES), :]
        bv = g_ref[pl.ds(r0, SUBLANES), :]
        for d in (1, 2, 4):
            keep = row8 >= d
            a_sh = jnp.where(keep, pltpu.roll(av, d, axis=0), 1.0)
            b_sh = jnp.where(keep, pltpu.roll(bv, d, axis=0), 0.0)
            bv = av * b_sh + bv
            av = av * a_sh
        hs = av * hprev + bv
        g_ref[pl.ds(r0, SUBLANES), :] = hs
        return jnp.broadcast_to(hs[SUBLANES - 1:SUBLANES, :], hs.shape)

    h_ref[...] = lax.fori_loop(0, ts // SUBLANES, group, h_ref[...], unroll=8)
    o_ref[0] = (g_ref[...] * jax.nn.gelu(cy_ref[0])).astype(BF16)


def _rglru(layer, cx, cy, cw, cb, wax, ba, bx, lam, ts):
    b, s, c = cx.shape
    blk = pl.BlockSpec((1, ts, c), lambda bi, si: (bi, si, 0))
    return pl.pallas_call(
        _rglru_kernel, grid=(b, s // ts),
        in_specs=[blk, blk] + [_layer_spec(a, layer) for a in (cw, cb, wax, ba, bx, lam)],
        out_specs=blk, out_shape=jax.ShapeDtypeStruct((b, s, c), BF16),
        scratch_shapes=[pltpu.VMEM((SUBLANES, c), F32), pltpu.VMEM((SUBLANES, c), F32),
                        pltpu.VMEM((ts, c), F32), pltpu.VMEM((ts, c), F32)],
        compiler_params=_cparams("parallel", "arbitrary"), name="rglru")(cx, cy, cw, cb, wax, ba, bx, lam)


def _merge_kernel(x_ref, ya_ref, yb_ref, yc_ref, g_ref, wg_ref, bg_ref, wb_ref, wo_ref, o_ref):
    x = x_ref[...]
    d = x.shape[1]
    h = _rms(x, g_ref[...]).astype(BF16)
    ys = (ya_ref[...], yb_ref[...], yc_ref[...])
    offs = (0, A_WIDTH, A_WIDTH + B_WIDTH, A_WIDTH + B_WIDTH + C_WIDTH)
    merged = None
    for j in range(N_BRANCH):
        gate = jax.nn.sigmoid(jnp.dot(h, wg_ref[:, j * d:(j + 1) * d], preferred_element_type=F32)
                              + bg_ref[:, j * d:(j + 1) * d])
        term = gate * jnp.dot(ys[j], wb_ref[offs[j]:offs[j + 1], :], preferred_element_type=F32)
        merged = term if merged is None else merged + term
    o_ref[...] = x + jnp.dot(merged.astype(BF16), wo_ref[...], preferred_element_type=F32)


def _merge(layer, x2, ya, yb, yc, g, wg, bg, wb, wo, tm):
    n, d = x2.shape
    row = lambda w: pl.BlockSpec((tm, w), lambda i: (i, 0))
    return pl.pallas_call(
        _merge_kernel, grid=(n // tm,),
        in_specs=[row(d), row(A_WIDTH), row(B_WIDTH), row(C_WIDTH)]
        + [_layer_spec(a, layer) for a in (g, wg, bg, wb, wo)],
        out_specs=row(d), out_shape=jax.ShapeDtypeStruct((n, d), F32),
        compiler_params=_cparams("parallel"), name="merge")(x2, ya, yb, yc, g, wg, bg, wb, wo)


def _ffn_kernel(x_ref, g_ref, wi_ref, wo_ref, o_ref, *, d_ff, chunk):
    x = x_ref[...]
    h = _rms(x, g_ref[...]).astype(BF16)
    acc = x
    for c0 in range(0, d_ff, chunk):
        c1 = min(c0 + chunk, d_ff)
        gp = jnp.dot(h, wi_ref[:, c0:c1], preferred_element_type=F32)
        up = jnp.dot(h, wi_ref[:, d_ff + c0:d_ff + c1], preferred_element_type=F32)
        act = (jax.nn.silu(gp) * up).astype(BF16)
        acc = acc + jnp.dot(act, wo_ref[c0:c1, :], preferred_element_type=F32)
    o_ref[...] = acc


def _ffn(layer, x2, g, wi, wo, tm):
    n, d = x2.shape
    d_ff = wo.shape[1]
    row = pl.BlockSpec((tm, d), lambda i: (i, 0))
    kernel = functools.partial(_ffn_kernel, d_ff=d_ff, chunk=1024)
    return pl.pallas_call(
        kernel, grid=(n // tm,),
        in_specs=[row] + [_layer_spec(a, layer) for a in (g, wi, wo)],
        out_specs=row, out_shape=jax.ShapeDtypeStruct((n, d), F32),
        compiler_params=_cparams("parallel"), name="ffn")(x2, g, wi, wo)


def _rope_tables(s):
    pos = jnp.arange(s, dtype=F32)
    inv = 1.0 / (ROPE_THETA ** (jnp.arange(0, HEAD_DIM, 2, dtype=F32) / HEAD_DIM))
    ang = pos[:, None] * inv[None, :]
    cos, sin = jnp.cos(ang), jnp.sin(ang)
    return jnp.concatenate([cos, cos], axis=-1), jnp.concatenate([sin, sin], axis=-1)


def _bias_table_kernel(vec_ref, o_ref):
    nk, q_tile = o_ref.shape
    rolled = pltpu.roll(jnp.broadcast_to(vec_ref[...], (nk, nk + q_tile)), 0, axis=1, stride=1, stride_axis=0)
    j = lax.broadcasted_iota(I32, (nk, q_tile), 0)
    q = lax.broadcasted_iota(I32, (nk, q_tile), 1)
    shift = CHUNK.bit_length() - 1
    dchunk = A_LEFT_CHUNKS + (q >> shift) - (j >> shift)
    valid = jnp.logical_and(dchunk >= 0, dchunk <= A_LEFT_CHUNKS)
    o_ref[...] = jnp.where(valid, rolled[:, nk:] * LOG2_E, -jnp.inf)


def _rel_bias_tables(rel_bias):
    depth = rel_bias.shape[0]
    nk = A_KEY_TILES * Q_TILE
    assert REL_CLIP == Q_TILE and CHUNK & (CHUNK - 1) == 0
    rb = rel_bias.astype(F32)
    vec = jnp.concatenate([rb, jnp.tile(rb[..., -1:], (1, 1, nk + Q_TILE - rb.shape[-1]))], axis=-1)
    return pl.pallas_call(
        _bias_table_kernel, grid=(depth, A_HEADS),
        in_specs=[pl.BlockSpec((None, None, 1, nk + Q_TILE), lambda l, h: (l, h, 0, 0))],
        out_specs=pl.BlockSpec((None, nk, Q_TILE), lambda l, h: (l, 0, h)),
        out_shape=jax.ShapeDtypeStruct((depth, nk, A_HEADS * Q_TILE), F32),
        compiler_params=_cparams("parallel", "parallel"), name="bias_table")(vec[:, :, None, :])


def kernel(x, g_mix, w_in, qk_gain_a, rel_bias, qk_gain_b, g_idx_k, conv_w, conv_b, lru_wa, lru_ba, lru_wx,
           lru_bx, lru_lambda, b_gate, w_branch, w_out, g_ffn, w_ffn_in, w_ffn_out):
    b, s, d = x.shape
    depth = g_mix.shape[0]
    assert s % B_KEY_TILE == 0 and d % LANES == 0
    tm = 512
    cos, sin = _rope_tables(s)
    half = (jnp.arange(HEAD_DIM) < HEAD_DIM // 2).astype(F32)
    cos_rm = jnp.tile(cos, (1, 4))
    sina_rm = jnp.tile(-sin * half[None, :], (1, 4))
    sinb_rm = jnp.tile(sin * (1.0 - half)[None, :], (1, 4))
    cosT, sinT = cos.T, sin.T
    hid = jnp.arange(256) // HEAD_DIM
    bd = (hid[:, None] == hid[None, :]).astype(BF16)
    rows = lambda v: v.reshape(depth, 1, -1).astype(F32)
    o_aq, o_ak, o_av, o_bq, o_bk, o_bv = 0, 256, 512, 768, 1024, 1280
    o_iq, o_ik, o_iw, o_cx, o_cy, o_gt = 1536, 1792, 1856, 1860, 2372, 2884
    blocks = jnp.arange(C_WIDTH) // (C_WIDTH // C_BLOCKS)
    bmask = (blocks[:, None] == blocks[None, :])

    def dense_blocks(wblk):
        return jnp.where(bmask, jnp.tile(wblk.reshape(depth, C_WIDTH, C_WIDTH // C_BLOCKS), (1, 1, C_BLOCKS)), 0.0)

    cols = lambda o, n: w_in[:, :, o:o + n]
    wrm = jnp.concatenate([cols(o_ak, 256), cols(o_bk, 256), cols(o_ik, 64), jnp.zeros((depth, d, 64), F32),
                           cols(o_cx, 512), cols(o_cy, 512)], axis=2).astype(BF16)
    wt = _transpose_cast(jnp.concatenate(
        [cols(o_aq, 256), cols(o_av, 256), cols(o_bq, 256), cols(o_bv, 256), cols(o_iq, 256), cols(o_iw, 4),
         jnp.zeros((depth, d, LANES - 4), F32)], axis=2))
    w_gate = w_in[:, :, o_gt:].astype(BF16)
    gka, gkb = rows(jnp.tile(qk_gain_a[:, 1], (1, 4))), rows(jnp.tile(qk_gain_b[:, 1], (1, 4)))
    gki = rows(jnp.concatenate([g_idx_k, jnp.zeros((depth, 64), F32)], axis=1))
    gqaT = jnp.tile(qk_gain_a[:, 0], (1, 4)).reshape(depth, 256, 1)
    gqbT = jnp.tile(qk_gain_b[:, 0], (1, 4)).reshape(depth, 256, 1)
    bias_tabs = _rel_bias_tables(rel_bias)
    wax = jnp.concatenate([dense_blocks(lru_wa), dense_blocks(lru_wx)], axis=2).astype(BF16)
    g_mix3, g_ffn3, b_gate3 = rows(g_mix), rows(g_ffn), rows(b_gate)
    cb3, ba3, bx3, lam3 = rows(conv_b), rows(lru_ba), rows(lru_bx), rows(lru_lambda)
    wb, wo = w_branch.astype(BF16), w_out.astype(BF16)
    wfi, wfo = w_ffn_in.astype(BF16), w_ffn_out.astype(BF16)

    n = b * s
    for l in range(depth):
        ka, kb, ki, cx, cy, qaT, vaT, qbT, vbT, qiT, iwT = _project(
            l, x, g_mix3, wrm, wt, bd, gka, gkb, gki, gqaT, gqbT, cos_rm, sina_rm, sinb_rm, cosT, sinT, tm)
        ya = _chunk_attention(l, qaT, ka, vaT, bias_tabs)
        yb = _sparse_attention(qiT, iwT, qbT, ki, kb, vbT)
        yc = _rglru(l, cx, cy, conv_w, cb3, wax, ba3, bx3, lam3, 512)
        x1 = _merge(l, x.reshape(n, d), ya.reshape(n, A_WIDTH), yb.reshape(n, B_WIDTH), yc.reshape(n, C_WIDTH),
                    g_mix3, w_gate, b_gate3, wb, wo, tm)
        x = _ffn(l, x1, g_ffn3, wfi, wfo, tm).reshape(b, s, d)
    return x
```

```python
import functools
import math

import jax
import jax.numpy as jnp
from jax import lax
from jax.experimental import pallas as pl
from jax.experimental.pallas import tpu as pltpu

F32 = jnp.float32
BF16 = jnp.bfloat16
I32 = jnp.int32
U32 = jnp.uint32

CHUNK = 64
HEAD_DIM = 64
EPS = 1e-6
ROPE_THETA = 10000.0
A_HEADS = 4
A_LEFT_CHUNKS = 8
REL_CLIP = 128
B_HEADS = 4
IDX_HEADS = 4
IDX_DIM = 64
TOPK_MAX = 256
C_WIDTH = 512
C_BLOCKS = 8
C_CONV = 4
LRU_C = 8.0
N_BRANCH = 3
A_WIDTH = A_HEADS * HEAD_DIM
B_WIDTH = B_HEADS * HEAD_DIM

LANES = 128
SUBLANES = 8
Q_TILE = 128
A_KEY_TILES = A_LEFT_CHUNKS * CHUNK // Q_TILE + 1
A_TILES_PER_STEP = 4
B_KEY_TILE = 512
PLANE_GROUP = 32 * SUBLANES
LOG2_E = math.log2(math.e)
INT_MIN = -2 ** 31
VMEM_LIMIT = 56 * 1024 * 1024

NT_DIMS = (((1,), (1,)), ((), ()))


def _cparams(*sem):
    return pltpu.CompilerParams(dimension_semantics=sem, vmem_limit_bytes=VMEM_LIMIT)


def _const_spec(shape):
    nd = len(shape)
    return pl.BlockSpec(shape, lambda *_: (0,) * nd, pipeline_mode=pl.Buffered(1))


def _layer_spec(stacked, layer):
    nd = stacked.ndim - 1
    return pl.BlockSpec((None,) + stacked.shape[1:], lambda *_: (layer,) + (0,) * nd, pipeline_mode=pl.Buffered(1))


def _rms(x, gain):
    ms = jnp.mean(x * x, axis=-1, keepdims=True)
    return x * lax.rsqrt(ms + EPS) * gain


def _proj_kernel(x_ref, g_ref, wrm_ref, wt_ref, bd_ref, gka_ref, gkb_ref, gki_ref, gqaT_ref, gqbT_ref,
                 cos_ref, sina_ref, sinb_ref, cosT_ref, sinT_ref,
                 ka_ref, kb_ref, ki_ref, cx_ref, cy_ref,
                 qaT_ref, vaT_ref, qbT_ref, vbT_ref, qiT_ref, iwT_ref):
    tm = x_ref.shape[1]
    h = _rms(x_ref[0], g_ref[...]).astype(BF16)

    def rm(c0, c1):
        return jnp.dot(h, wrm_ref[:, c0:c1], preferred_element_type=F32)

    def head_rms_rm(z, gain):
        ssq = jnp.dot((z * z).astype(BF16), bd_ref[...], preferred_element_type=F32)
        return z * lax.rsqrt(ssq * (1.0 / HEAD_DIM) + EPS) * gain

    def rope_rm(z):
        w = z.shape[1]
        zm = pltpu.roll(z, w - HEAD_DIM // 2, axis=1)
        zp = pltpu.roll(z, HEAD_DIM // 2, axis=1)
        return z * cos_ref[:, :w] + zm * sina_ref[:, :w] + zp * sinb_ref[:, :w]

    ka_ref[0] = head_rms_rm(rm(0, 256), gka_ref[...]).astype(BF16)
    kb_ref[0] = rope_rm(head_rms_rm(rm(256, 512), gkb_ref[...])).astype(BF16)
    zi = rm(512, 640)
    ms = jnp.sum(zi * zi, axis=-1, keepdims=True) * (1.0 / IDX_DIM)
    ki_ref[0] = rope_rm(zi * lax.rsqrt(ms + EPS) * gki_ref[...]).astype(BF16)
    cx_ref[0] = rm(640, 1152)
    cy_ref[0] = rm(1152, 1664)

    def tr(r0, r1):
        return lax.dot_general(wt_ref[r0:r1, :], h, NT_DIMS, preferred_element_type=F32)

    def head_rms_t(z, gain):
        z4 = z.reshape(4, HEAD_DIM, tm)
        ms4 = jnp.mean(z4 * z4, axis=1, keepdims=True)
        return (z4 * lax.rsqrt(ms4 + EPS)).reshape(4 * HEAD_DIM, tm) * gain

    def rope_t(z):
        z4 = z.reshape(4, HEAD_DIM, tm)
        rot = jnp.concatenate([-z4[:, HEAD_DIM // 2:, :], z4[:, :HEAD_DIM // 2, :]], axis=1)
        return (z4 * cosT_ref[...][None] + rot * sinT_ref[...][None]).reshape(4 * HEAD_DIM, tm)

    scale = HEAD_DIM ** -0.5 * LOG2_E
    qaT_ref[0] = (head_rms_t(tr(0, 256), gqaT_ref[...]) * scale).astype(BF16)
    vaT_ref[0] = tr(256, 512).astype(BF16)
    qbT_ref[0] = (rope_t(head_rms_t(tr(512, 768), gqbT_ref[...])) * scale).astype(BF16)
    vbT_ref[0] = tr(768, 1024).astype(BF16)
    qiT_ref[0] = rope_t(tr(1024, 1280)).astype(BF16)
    iwT_ref[0] = tr(1280, 1296)[0:SUBLANES] * (IDX_HEADS ** -0.5 * IDX_DIM ** -0.5)


def _project(layer, x, g, wrm, wt, bd, gka, gkb, gki, gqaT, gqbT, cos, sina, sinb, cosT, sinT, tm):
    b, s, d = x.shape
    grid = (b, s // tm)
    row = lambda w: pl.BlockSpec((1, tm, w), lambda bi, si: (bi, si, 0))
    col = lambda r: pl.BlockSpec((1, r, tm), lambda bi, si: (bi, 0, si))
    tab_rm = pl.BlockSpec((tm, 256), lambda bi, si: (si, 0))
    tab_t = pl.BlockSpec((HEAD_DIM, tm), lambda bi, si: (0, si))
    per_layer = lambda a: _layer_spec(a, layer)
    in_specs = [row(d), per_layer(g), per_layer(wrm), per_layer(wt), _const_spec(bd.shape),
                per_layer(gka), per_layer(gkb), per_layer(gki), per_layer(gqaT), per_layer(gqbT),
                tab_rm, tab_rm, tab_rm, tab_t, tab_t]
    out_shape = (jax.ShapeDtypeStruct((b, s, 256), BF16), jax.ShapeDtypeStruct((b, s, 256), BF16),
                 jax.ShapeDtypeStruct((b, s, 128), BF16),
                 jax.ShapeDtypeStruct((b, s, C_WIDTH), F32), jax.ShapeDtypeStruct((b, s, C_WIDTH), F32),
                 jax.ShapeDtypeStruct((b, 256, s), BF16), jax.ShapeDtypeStruct((b, 256, s), BF16),
                 jax.ShapeDtypeStruct((b, 256, s), BF16), jax.ShapeDtypeStruct((b, 256, s), BF16),
                 jax.ShapeDtypeStruct((b, 256, s), BF16), jax.ShapeDtypeStruct((b, 8, s), F32))
    out_specs = (row(256), row(256), row(128), row(C_WIDTH), row(C_WIDTH),
                 col(256), col(256), col(256), col(256), col(256), col(8))
    return pl.pallas_call(_proj_kernel, grid=grid, in_specs=in_specs, out_specs=out_specs, out_shape=out_shape,
                          compiler_params=_cparams("parallel", "parallel"), name="proj")(
        x, g, wrm, wt, bd, gka, gkb, gki, gqaT, gqbT, cos, sina, sinb, cosT, sinT)


O_AQ, O_AK, O_AV, O_BQ, O_BK, O_BV = 0, 256, 512, 768, 1024, 1280
O_IQ, O_IK, O_IW, O_CX, O_CY, O_GT = 1536, 1792, 1856, 1860, 2372, 2884
W_ROWMAJOR_COLS = 256 + 256 + 128 + 2 * C_WIDTH
W_TRANSPOSED_ROWS = 5 * 256 + LANES
W_PREP_ROWS = 256


def _weight_prep_kernel(w_ref, wrm_ref, wg_ref, wt_ref):
    rows = w_ref.shape[0]
    cols = lambda o, n: w_ref[:, o:o + n]
    wrm_ref[...] = jnp.concatenate(
        [cols(O_AK, 256), cols(O_BK, 256), cols(O_IK, IDX_DIM), jnp.zeros((rows, LANES - IDX_DIM), F32),
         cols(O_CX, C_WIDTH), cols(O_CY, C_WIDTH)], axis=1).astype(BF16)
    wg_ref[...] = w_ref[:, O_GT:].astype(BF16)
    wt = jnp.concatenate([cols(O_AQ, 256), cols(O_AV, 256), cols(O_BQ, 256), cols(O_BV, 256), cols(O_IQ, 256),
                          cols(O_IW, IDX_HEADS), jnp.zeros((rows, LANES - IDX_HEADS), F32)], axis=1)
    wt_ref[...] = wt.T.astype(BF16)


def _weight_prep(w_in):
    depth, d, n = w_in.shape
    n_gate = n - O_GT
    rb = W_PREP_ROWS
    return pl.pallas_call(
        _weight_prep_kernel, grid=(depth, d // rb),
        in_specs=[pl.BlockSpec((None, rb, n), lambda l, i: (l, i, 0))],
        out_specs=(pl.BlockSpec((None, rb, W_ROWMAJOR_COLS), lambda l, i: (l, i, 0)),
                   pl.BlockSpec((None, rb, n_gate), lambda l, i: (l, i, 0)),
                   pl.BlockSpec((None, W_TRANSPOSED_ROWS, rb), lambda l, i: (l, 0, i))),
        out_shape=(jax.ShapeDtypeStruct((depth, d, W_ROWMAJOR_COLS), BF16),
                   jax.ShapeDtypeStruct((depth, d, n_gate), BF16),
                   jax.ShapeDtypeStruct((depth, W_TRANSPOSED_ROWS, d), BF16)),
        compiler_params=_cparams("parallel", "parallel"), name="weight_prep")(w_in)


def _block_diag_q(qT):
    rows = lax.broadcasted_iota(I32, qT.shape, 0) // HEAD_DIM
    zero = jnp.zeros_like(qT)
    return jnp.concatenate([jnp.where(rows == hh, qT, zero) for hh in range(4)], axis=1)


def _chunk_attn_kernel(qT_ref, k_ref, vT_ref, bias_ref, o_ref):
    for sub in range(A_TILES_PER_STEP):
        _chunk_attn_tile(pl.program_id(1) * A_TILES_PER_STEP + sub,
                         qT_ref.at[0, :, sub * Q_TILE:(sub + 1) * Q_TILE], k_ref, vT_ref, bias_ref,
                         o_ref.at[0, sub * Q_TILE:(sub + 1) * Q_TILE, :])


def _chunk_attn_tile(i, qT_ref, k_ref, vT_ref, bias_ref, o_ref):
    qbd = _block_diag_q(qT_ref[...])
    ks, vs, pens = [], [], []
    for t in range(A_KEY_TILES):
        kt = i - (A_KEY_TILES - 1) + t
        r0 = pl.multiple_of(jnp.maximum(kt, 0) * Q_TILE, Q_TILE)
        ks.append(k_ref[0, pl.ds(r0, Q_TILE), :])
        vs.append(vT_ref[0, :, pl.ds(r0, Q_TILE)])
        pens.append(jnp.where(kt >= 0, 0.0, -jnp.inf).astype(F32))
    kwin = jnp.concatenate(ks, axis=0)
    vwin = jnp.concatenate(vs, axis=1)
    s = jnp.dot(kwin, qbd, preferred_element_type=F32) + bias_ref[...]
    s = jnp.concatenate([s[t * Q_TILE:(t + 1) * Q_TILE] + pens[t] for t in range(A_KEY_TILES)], axis=0)
    m = jnp.max(s, axis=0, keepdims=True)
    p = jnp.exp2(s - m)
    l = jnp.sum(p, axis=0, keepdims=True)
    pb = p.astype(BF16)
    outs = []
    for hh in range(A_HEADS):
        lo, hi = hh * Q_TILE, (hh + 1) * Q_TILE
        o = jnp.dot(vwin[hh * HEAD_DIM:(hh + 1) * HEAD_DIM, :], pb[:, lo:hi], preferred_element_type=F32)
        outs.append(o / l[:, lo:hi])
    o_ref[...] = jnp.concatenate(outs, axis=0).T.astype(BF16)


def _chunk_attention(layer, qaT, ka, vaT, biasT):
    b, _, s = qaT.shape
    step = A_TILES_PER_STEP * Q_TILE
    grid = (b, s // step)
    return pl.pallas_call(
        _chunk_attn_kernel, grid=grid,
        in_specs=[pl.BlockSpec((1, 256, step), lambda bi, i: (bi, 0, i)),
                  pl.BlockSpec((1, s, 256), lambda bi, i: (bi, 0, 0)),
                  pl.BlockSpec((1, 256, s), lambda bi, i: (bi, 0, 0)),
                  _layer_spec(biasT, layer)],
        out_specs=pl.BlockSpec((1, step, 256), lambda bi, i: (bi, i, 0)),
        out_shape=jax.ShapeDtypeStruct((b, s, 256), BF16),
        compiler_params=_cparams("parallel", "parallel"), name="chunk_attn")(qaT, ka, vaT, biasT)


def _bit_transpose32(words):
    a = list(words)
    mask, j = 0x0000FFFF, 16
    while j:
        k = 0
        while k < 32:
            t = (a[k] ^ (a[k + j] >> jnp.uint32(j))) & jnp.uint32(mask)
            a[k] = a[k] ^ t
            a[k + j] = a[k + j] ^ (t << jnp.uint32(j))
            k = (k + j + 1) & ~j
        j >>= 1
        mask = (mask ^ (mask << j)) & 0xFFFFFFFF
    return a


def _dsa_kernel(qiT_ref, iwT_ref, qbT_ref, ki_ref, kb_ref, vbT_ref, o_ref, scores_ref, planes_ref, *, topk):
    tk = B_KEY_TILE
    i = pl.program_id(1)
    n_small = lax.shift_right_logical(i + 2, 1)
    n_wide = lax.shift_right_logical(n_small, 2)
    has_mid, has_small = (n_small & 2) != 0, (n_small & 1) != 0
    qiT = qiT_ref[0]
    qi = jnp.concatenate([qiT[hh * IDX_DIM:(hh + 1) * IDX_DIM] for hh in range(IDX_HEADS)], axis=1)
    qi = jnp.concatenate([qi, jnp.zeros_like(qi)], axis=0)
    w = iwT_ref[0]
    lane = lax.broadcasted_iota(I32, (1, Q_TILE), 1)
    key_limit = i * Q_TILE + CHUNK + jnp.where(lane >= CHUNK, CHUNK, 0)

    def over_key_tiles(span, carry):
        wide, small = 2 * tk, tk // 2
        carry = lax.fori_loop(0, n_wide, lambda u, c: span(pl.multiple_of(u * wide, wide), wide, c), carry)
        r_mid = pl.multiple_of(n_wide * wide, wide)
        carry = lax.cond(has_mid, lambda c: span(r_mid, tk, c), lambda c: c, carry)
        r_small = pl.multiple_of(r_mid + jnp.where(has_mid, tk, 0), small)
        return lax.cond(has_small, lambda c: span(r_small, small, c), lambda c: c, carry)

    def score_span(r0, rows, c):
        dots = jnp.dot(ki_ref[0, pl.ds(r0, rows), :], qi, preferred_element_type=F32)
        sc = jnp.maximum(dots[:, 0:Q_TILE], 0.0) * w[0:1, :]
        for hh in range(1, IDX_HEADS):
            sc = sc + jnp.maximum(dots[:, hh * Q_TILE:(hh + 1) * Q_TILE], 0.0) * w[hh:hh + 1, :]
        admissible = lax.broadcasted_iota(I32, (rows, Q_TILE), 0) < key_limit - r0
        sc = jnp.where(admissible, sc, -jnp.inf)
        scores_ref[pl.ds(r0, rows), :] = sc
        bits = lax.bitcast_convert_type(sc, U32)
        for g0 in range(0, rows, PLANE_GROUP):
            words = [bits[g0 + j * SUBLANES:g0 + (j + 1) * SUBLANES] for j in range(32)]
            planes = _bit_transpose32(words)
            magnitude = planes[1]
            for p in planes[2:]:
                magnitude = magnitude | p
            negative = planes[0] & magnitude
            planes = [~negative] + [p ^ negative for p in planes[1:]]
            planes_ref[pl.ds(r0 + g0, PLANE_GROUP), :] = jnp.concatenate(planes, axis=0)
        return c

    @pl.when(i == 0)
    def _():
        planes_ref[...] = jnp.zeros_like(planes_ref)

    over_key_tiles(score_span, 0)

    max_groups = planes_ref.shape[0] // PLANE_GROUP
    full = jnp.full((SUBLANES, Q_TILE), 0xFFFFFFFF, U32)
    zero = jnp.zeros((1, Q_TILE), I32)

    def search(n_groups):
        live0 = tuple(jnp.where(g < n_small, full, jnp.zeros_like(full)) for g in range(n_groups))

        def members(masks):
            parts = [lax.population_count(m) for m in masks]
            while len(parts) > 1:
                parts = [parts[j] + parts[j + 1] for j in range(0, len(parts) - 1, 2)] + parts[len(parts) & ~1:]
            return jnp.sum(parts[0].astype(I32), axis=0, keepdims=True)

        def two_bits(it, carry):
            live, n_gt, thr_bits = carry
            row = pl.multiple_of(it * (2 * SUBLANES), 2 * SUBLANES)
            p_hi = [planes_ref[pl.ds(g * PLANE_GROUP + row, SUBLANES), :] for g in range(n_groups)]
            p_lo = [planes_ref[pl.ds(g * PLANE_GROUP + row + SUBLANES, SUBLANES), :] for g in range(n_groups)]
            set_hi = [live[g] & p_hi[g] for g in range(n_groups)]
            clr_hi = [live[g] ^ set_hi[g] for g in range(n_groups)]
            set_both = [set_hi[g] & p_lo[g] for g in range(n_groups)]
            clr_set = [clr_hi[g] & p_lo[g] for g in range(n_groups)]
            n_hi = n_gt + members(set_hi)
            take_hi = n_hi >= topk
            n_gt = jnp.where(take_hi, n_gt, n_hi)
            n_lo = n_gt + jnp.where(take_hi, members(set_both), members(clr_set))
            take_lo = n_lo >= topk
            new_live = []
            for g in range(n_groups):
                kept = jnp.where(take_hi, set_hi[g], clr_hi[g])
                low = jnp.where(take_hi, set_both[g], clr_set[g])
                new_live.append(jnp.where(take_lo, low, kept ^ low))
            bit_hi = jnp.int32(1) << (31 - 2 * it)
            bit_lo = lax.shift_right_logical(bit_hi, 1)
            thr_bits = thr_bits | jnp.where(take_hi, bit_hi, 0) | jnp.where(take_lo, bit_lo, 0)
            return tuple(new_live), jnp.where(take_lo, n_gt, n_lo), thr_bits

        _, n_gt, thr_bits = lax.fori_loop(0, 16, two_bits, (live0, zero, zero))
        return n_gt, thr_bits

    if max_groups % 2 == 0:
        n_gt, thr_bits = lax.cond(n_small <= max_groups // 2, lambda: search(max_groups // 2), lambda: search(max_groups))
    else:
        n_gt, thr_bits = search(max_groups)
    thr = lax.bitcast_convert_type(jnp.where(thr_bits < 0, thr_bits ^ jnp.int32(INT_MIN), ~thr_bits), F32)

    need = jnp.where(thr == -jnp.inf, 0, topk - n_gt).astype(F32)
    hk = tk // 2
    tril = (lax.broadcasted_iota(I32, (hk, hk), 0) >= lax.broadcasted_iota(I32, (hk, hk), 1)).astype(BF16)

    qbd = _block_diag_q(qbT_ref[0])

    def attn_span(r0, rows, carry):
        ms, ls, accs, seen = carry
        key = scores_ref[pl.ds(r0, rows), :]
        tied = key == thr
        tied_b = jnp.where(tied, 1.0, 0.0).astype(BF16)
        ranks = []
        for c0 in range(0, rows, hk):
            ranks.append(jnp.dot(tril, tied_b[c0:c0 + hk], preferred_element_type=F32) + seen)
            seen = ranks[-1][hk - 1:hk, :]
        sel = jnp.logical_or(key > thr, jnp.logical_and(tied, jnp.concatenate(ranks, axis=0) <= need))
        s_all = jnp.dot(kb_ref[0, pl.ds(r0, rows), :], qbd, preferred_element_type=F32)
        new_ms, new_ls, new_accs = [], [], []
        for hh in range(B_HEADS):
            s = jnp.where(sel, s_all[:, hh * Q_TILE:(hh + 1) * Q_TILE], -jnp.inf)
            m_new = jnp.maximum(ms[hh], jnp.max(s, axis=0, keepdims=True))
            alpha = jnp.exp2(ms[hh] - m_new)
            p = jnp.exp2(s - m_new)
            new_ls.append(alpha * ls[hh] + jnp.sum(p, axis=0, keepdims=True))
            pv = jnp.dot(vbT_ref[0, hh * HEAD_DIM:(hh + 1) * HEAD_DIM, pl.ds(r0, rows)], p.astype(BF16),
                         preferred_element_type=F32)
            new_accs.append(alpha * accs[hh] + pv)
            new_ms.append(m_new)
        return tuple(new_ms), tuple(new_ls), tuple(new_accs), seen

    init = (tuple(jnp.full((1, Q_TILE), -1e30, F32) for _ in range(B_HEADS)),
            tuple(jnp.zeros((1, Q_TILE), F32) for _ in range(B_HEADS)),
            tuple(jnp.zeros((HEAD_DIM, Q_TILE), F32) for _ in range(B_HEADS)),
            jnp.zeros((1, Q_TILE), F32))
    _, ls, accs, _ = over_key_tiles(attn_span, init)
    out = jnp.concatenate([accs[hh] / ls[hh] for hh in range(B_HEADS)], axis=0)
    o_ref[0] = out.T.astype(BF16)


def _sparse_attention(qiT, iwT, qbT, ki, kb, vbT):
    b, _, s = qbT.shape
    s_pad = -(-s // B_KEY_TILE) * B_KEY_TILE
    topk = min(TOPK_MAX, s // 4)
    grid = (b, s // Q_TILE)
    qspec = lambda r: pl.BlockSpec((1, r, Q_TILE), lambda bi, i: (bi, 0, i))
    kernel = functools.partial(_dsa_kernel, topk=topk)
    return pl.pallas_call(
        kernel, grid=grid,
        in_specs=[qspec(256), qspec(8), qspec(256),
                  pl.BlockSpec((1, s, 128), lambda bi, i: (bi, 0, 0)),
                  pl.BlockSpec((1, s, 256), lambda bi, i: (bi, 0, 0)),
                  pl.BlockSpec((1, 256, s), lambda bi, i: (bi, 0, 0))],
        out_specs=pl.BlockSpec((1, Q_TILE, 256), lambda bi, i: (bi, i, 0)),
        out_shape=jax.ShapeDtypeStruct((b, s, 256), BF16),
        scratch_shapes=[pltpu.VMEM((s_pad, Q_TILE), F32), pltpu.VMEM((s_pad, Q_TILE), U32)],
        compiler_params=_cparams("parallel", "arbitrary"), name="sparse_attn")(qiT, iwT, qbT, ki, kb, vbT)


def _rglru_kernel(cx_ref, cy_ref, cw_ref, cb_ref, wax_ref, ba_ref, bx_ref, lam_ref, o_ref,
                  tail_ref, h_ref, a_ref, g_ref):
    ts = cx_ref.shape[1]

    @pl.when(pl.program_id(1) == 0)
    def _():
        tail_ref[...] = jnp.zeros_like(tail_ref)
        h_ref[...] = jnp.zeros_like(h_ref)

    x = cx_ref[0]
    tail = tail_ref[...]
    row8 = lax.broadcasted_iota(I32, (SUBLANES, C_WIDTH), 0)
    u = cb_ref[...] + x * cw_ref[C_CONV - 1:C_CONV, :]
    for k in range(1, C_CONV):
        xr = pltpu.roll(x, k, axis=0)
        head = jnp.where(row8 < k, pltpu.roll(tail, k, axis=0), xr[0:SUBLANES])
        xk = jnp.concatenate([head, xr[SUBLANES:]], axis=0)
        u = u + xk * cw_ref[C_CONV - 1 - k:C_CONV - k, :]
    tail_ref[...] = x[ts - SUBLANES:ts]

    gates = jnp.dot(u.astype(BF16), wax_ref[...], preferred_element_type=F32)
    r = jax.nn.sigmoid(gates[:, :C_WIDTH] + ba_ref[...])
    ig = jax.nn.sigmoid(gates[:, C_WIDTH:] + bx_ref[...])
    nlam = -lam_ref[...]
    softplus = jnp.maximum(nlam, 0.0) + jnp.log1p(jnp.exp(-jnp.abs(nlam)))
    a = jnp.exp(-LRU_C * r * softplus)
    a_ref[...] = a
    g_ref[...] = jnp.sqrt(1.0 - a * a) * (ig * u)

    def group(gi, hprev):
        r0 = pl.multiple_of(gi * SUBLANES, SUBLANES)
        av = a_ref[pl.ds(r0, SUBLANES), :]
        bv = g_ref[pl.ds(r0, SUBLANES), :]
        for d in (1, 2, 4):
            keep = row8 >= d
            a_sh = jnp.where(keep, pltpu.roll(av, d, axis=0), 1.0)
            b_sh = jnp.where(keep, pltpu.roll(bv, d, axis=0), 0.0)
            bv = av * b_sh + bv
            av = av * a_sh
        hs = av * hprev + bv
        g_ref[pl.ds(r0, SUBLANES), :] = hs
        return jnp.broadcast_to(hs[SUBLANES - 1:SUBLANES, :], hs.shape)

    h_ref[...] = lax.fori_loop(0, ts // SUBLANES, group, h_ref[...], unroll=8)
    o_ref[0] = (g_ref[...] * jax.nn.gelu(cy_ref[0])).astype(BF16)


def _rglru(layer, cx, cy, cw, cb, wax, ba, bx, lam, ts):
    b, s, c = cx.shape
    blk = pl.BlockSpec((1, ts, c), lambda bi, si: (bi, si, 0))
    return pl.pallas_call(
        _rglru_kernel, grid=(b, s // ts),
        in_specs=[blk, blk] + [_layer_spec(a, layer) for a in (cw, cb, wax, ba, bx, lam)],
        out_specs=blk, out_shape=jax.ShapeDtypeStruct((b, s, c), BF16),
        scratch_shapes=[pltpu.VMEM((SUBLANES, c), F32), pltpu.VMEM((SUBLANES, c), F32),
                        pltpu.VMEM((ts, c), F32), pltpu.VMEM((ts, c), F32)],
        compiler_params=_cparams("parallel", "arbitrary"), name="rglru")(cx, cy, cw, cb, wax, ba, bx, lam)


def _merge_kernel(x_ref, ya_ref, yb_ref, yc_ref, g_ref, wg_ref, bg_ref, wb_ref, wo_ref, o_ref):
    x = x_ref[...]
    d = x.shape[1]
    h = _rms(x, g_ref[...]).astype(BF16)
    ys = (ya_ref[...], yb_ref[...], yc_ref[...])
    offs = (0, A_WIDTH, A_WIDTH + B_WIDTH, A_WIDTH + B_WIDTH + C_WIDTH)
    merged = None
    for j in range(N_BRANCH):
        gate = jax.nn.sigmoid(jnp.dot(h, wg_ref[:, j * d:(j + 1) * d], preferred_element_type=F32)
                              + bg_ref[:, j * d:(j + 1) * d])
        term = gate * jnp.dot(ys[j], wb_ref[offs[j]:offs[j + 1], :], preferred_element_type=F32)
        merged = term if merged is None else merged + term
    o_ref[...] = x + jnp.dot(merged.astype(BF16), wo_ref[...], preferred_element_type=F32)


def _merge(layer, x2, ya, yb, yc, g, wg, bg, wb, wo, tm):
    n, d = x2.shape
    row = lambda w: pl.BlockSpec((tm, w), lambda i: (i, 0))
    return pl.pallas_call(
        _merge_kernel, grid=(n // tm,),
        in_specs=[row(d), row(A_WIDTH), row(B_WIDTH), row(C_WIDTH)]
        + [_layer_spec(a, layer) for a in (g, wg, bg, wb, wo)],
        out_specs=row(d), out_shape=jax.ShapeDtypeStruct((n, d), F32),
        compiler_params=_cparams("parallel"), name="merge")(x2, ya, yb, yc, g, wg, bg, wb, wo)


def _ffn_kernel(x_ref, g_ref, wi_ref, wo_ref, o_ref, *, d_ff, chunk):
    x = x_ref[...]
    h = _rms(x, g_ref[...]).astype(BF16)
    acc = x
    for c0 in range(0, d_ff, chunk):
        c1 = min(c0 + chunk, d_ff)
        gp = jnp.dot(h, wi_ref[:, c0:c1], preferred_element_type=F32)
        up = jnp.dot(h, wi_ref[:, d_ff + c0:d_ff + c1], preferred_element_type=F32)
        act = (jax.nn.silu(gp) * up).astype(BF16)
        acc = acc + jnp.dot(act, wo_ref[c0:c1, :], preferred_element_type=F32)
    o_ref[...] = acc


def _ffn(layer, x2, g, wi, wo, tm):
    n, d = x2.shape
    d_ff = wo.shape[1]
    row = pl.BlockSpec((tm, d), lambda i: (i, 0))
    kernel = functools.partial(_ffn_kernel, d_ff=d_ff, chunk=1024)
    return pl.pallas_call(
        kernel, grid=(n // tm,),
        in_specs=[row] + [_layer_spec(a, layer) for a in (g, wi, wo)],
        out_specs=row, out_shape=jax.ShapeDtypeStruct((n, d), F32),
        compiler_params=_cparams("parallel"), name="ffn")(x2, g, wi, wo)


def _rope_tables(s):
    pos = jnp.arange(s, dtype=F32)
    inv = 1.0 / (ROPE_THETA ** (jnp.arange(0, HEAD_DIM, 2, dtype=F32) / HEAD_DIM))
    ang = pos[:, None] * inv[None, :]
    cos, sin = jnp.cos(ang), jnp.sin(ang)
    return jnp.concatenate([cos, cos], axis=-1), jnp.concatenate([sin, sin], axis=-1)


def _bias_table_kernel(vec_ref, o_ref):
    nk, q_tile = o_ref.shape
    rolled = pltpu.roll(jnp.broadcast_to(vec_ref[...], (nk, nk + q_tile)), 0, axis=1, stride=1, stride_axis=0)
    j = lax.broadcasted_iota(I32, (nk, q_tile), 0)
    q = lax.broadcasted_iota(I32, (nk, q_tile), 1)
    shift = CHUNK.bit_length() - 1
    dchunk = A_LEFT_CHUNKS + (q >> shift) - (j >> shift)
    valid = jnp.logical_and(dchunk >= 0, dchunk <= A_LEFT_CHUNKS)
    o_ref[...] = jnp.where(valid, rolled[:, nk:] * LOG2_E, -jnp.inf)


def _rel_bias_tables(rel_bias):
    depth = rel_bias.shape[0]
    nk = A_KEY_TILES * Q_TILE
    assert REL_CLIP == Q_TILE and CHUNK & (CHUNK - 1) == 0
    rb = rel_bias.astype(F32)
    vec = jnp.concatenate([rb, jnp.tile(rb[..., -1:], (1, 1, nk + Q_TILE - rb.shape[-1]))], axis=-1)
    return pl.pallas_call(
        _bias_table_kernel, grid=(depth, A_HEADS),
        in_specs=[pl.BlockSpec((None, None, 1, nk + Q_TILE), lambda l, h: (l, h, 0, 0))],
        out_specs=pl.BlockSpec((None, nk, Q_TILE), lambda l, h: (l, 0, h)),
        out_shape=jax.ShapeDtypeStruct((depth, nk, A_HEADS * Q_TILE), F32),
        compiler_params=_cparams("parallel", "parallel"), name="bias_table")(vec[:, :, None, :])


def kernel(x, g_mix, w_in, qk_gain_a, rel_bias, qk_gain_b, g_idx_k, conv_w, conv_b, lru_wa, lru_ba, lru_wx,
           lru_bx, lru_lambda, b_gate, w_branch, w_out, g_ffn, w_ffn_in, w_ffn_out):
    b, s, d = x.shape
    depth = g_mix.shape[0]
    assert s % B_KEY_TILE == 0 and d % LANES == 0
    tm = 512
    cos, sin = _rope_tables(s)
    half = (jnp.arange(HEAD_DIM) < HEAD_DIM // 2).astype(F32)
    cos_rm = jnp.tile(cos, (1, 4))
    sina_rm = jnp.tile(-sin * half[None, :], (1, 4))
    sinb_rm = jnp.tile(sin * (1.0 - half)[None, :], (1, 4))
    cosT, sinT = cos.T, sin.T
    hid = jnp.arange(256) // HEAD_DIM
    bd = (hid[:, None] == hid[None, :]).astype(BF16)
    rows = lambda v: v.reshape(depth, 1, -1).astype(F32)
    blocks = jnp.arange(C_WIDTH) // (C_WIDTH // C_BLOCKS)
    bmask = (blocks[:, None] == blocks[None, :])

    def dense_blocks(wblk):
        return jnp.where(bmask, jnp.tile(wblk.reshape(depth, C_WIDTH, C_WIDTH // C_BLOCKS), (1, 1, C_BLOCKS)), 0.0)

    wrm, w_gate, wt = _weight_prep(w_in)
    gka, gkb = rows(jnp.tile(qk_gain_a[:, 1], (1, 4))), rows(jnp.tile(qk_gain_b[:, 1], (1, 4)))
    gki = rows(jnp.concatenate([g_idx_k, jnp.zeros((depth, 64), F32)], axis=1))
    gqaT = jnp.tile(qk_gain_a[:, 0], (1, 4)).reshape(depth, 256, 1)
    gqbT = jnp.tile(qk_gain_b[:, 0], (1, 4)).reshape(depth, 256, 1)
    bias_tabs = _rel_bias_tables(rel_bias)
    wax = jnp.concatenate([dense_blocks(lru_wa), dense_blocks(lru_wx)], axis=2).astype(BF16)
    g_mix3, g_ffn3, b_gate3 = rows(g_mix), rows(g_ffn), rows(b_gate)
    cb3, ba3, bx3, lam3 = rows(conv_b), rows(lru_ba), rows(lru_bx), rows(lru_lambda)
    wb, wo = w_branch.astype(BF16), w_out.astype(BF16)
    wfi, wfo = w_ffn_in.astype(BF16), w_ffn_out.astype(BF16)

    n = b * s
    for l in range(depth):
        ka, kb, ki, cx, cy, qaT, vaT, qbT, vbT, qiT, iwT = _project(
            l, x, g_mix3, wrm, wt, bd, gka, gkb, gki, gqaT, gqbT, cos_rm, sina_rm, sinb_rm, cosT, sinT, tm)
        ya = _chunk_attention(l, qaT, ka, vaT, bias_tabs)
        yb = _sparse_attention(qiT, iwT, qbT, ki, kb, vbT)
        yc = _rglru(l, cx, cy, conv_w, cb3, wax, ba3, bx3, lam3, 512)
        x1 = _merge(l, x.reshape(n, d), ya.reshape(n, A_WIDTH), yb.reshape(n, B_WIDTH), yc.reshape(n, C_WIDTH),
                    g_mix3, w_gate, b_gate3, wb, wo, tm)
        x = _ffn(l, x1, g_ffn3, wfi, wfo, tm).reshape(b, s, d)
    return x
```

```python
import functools
import math

import jax
import jax.numpy as jnp
from jax import lax
from jax.experimental import pallas as pl
from jax.experimental.pallas import tpu as pltpu

F32 = jnp.float32
BF16 = jnp.bfloat16
I32 = jnp.int32
U32 = jnp.uint32

CHUNK = 64
HEAD_DIM = 64
EPS = 1e-6
ROPE_THETA = 10000.0
A_HEADS = 4
A_LEFT_CHUNKS = 8
REL_CLIP = 128
B_HEADS = 4
IDX_HEADS = 4
IDX_DIM = 64
TOPK_MAX = 256
C_WIDTH = 512
C_BLOCKS = 8
C_CONV = 4
LRU_C = 8.0
N_BRANCH = 3
A_WIDTH = A_HEADS * HEAD_DIM
B_WIDTH = B_HEADS * HEAD_DIM

LANES = 128
SUBLANES = 8
Q_TILE = 128
A_KEY_TILES = A_LEFT_CHUNKS * CHUNK // Q_TILE + 1
A_TILES_PER_STEP = 4
B_TILES_PER_STEP = 2
B_KEY_TILE = 512
PLANE_GROUP = 32 * SUBLANES
LOG2_E = math.log2(math.e)
INT_MIN = -2 ** 31
VMEM_LIMIT = 56 * 1024 * 1024

NT_DIMS = (((1,), (1,)), ((), ()))


def _cparams(*sem):
    return pltpu.CompilerParams(dimension_semantics=sem, vmem_limit_bytes=VMEM_LIMIT)


def _const_spec(shape):
    nd = len(shape)
    return pl.BlockSpec(shape, lambda *_: (0,) * nd, pipeline_mode=pl.Buffered(1))


def _layer_spec(stacked, layer):
    nd = stacked.ndim - 1
    return pl.BlockSpec((None,) + stacked.shape[1:], lambda *_: (layer,) + (0,) * nd, pipeline_mode=pl.Buffered(1))


def _rms(x, gain):
    ms = jnp.mean(x * x, axis=-1, keepdims=True)
    return x * lax.rsqrt(ms + EPS) * gain


def _proj_kernel(x_ref, g_ref, wrm_ref, wt_ref, bd_ref, gka_ref, gkb_ref, gki_ref, gqaT_ref, gqbT_ref,
                 cos_ref, sina_ref, sinb_ref, cosT_ref, sinT_ref,
                 ka_ref, kb_ref, ki_ref, cx_ref, cy_ref,
                 qaT_ref, vaT_ref, qbT_ref, vbT_ref, qiT_ref, iwT_ref):
    tm = x_ref.shape[1]
    h = _rms(x_ref[0], g_ref[...]).astype(BF16)

    def rm(c0, c1):
        return jnp.dot(h, wrm_ref[:, c0:c1], preferred_element_type=F32)

    def head_rms_rm(z, gain):
        ssq = jnp.dot((z * z).astype(BF16), bd_ref[...], preferred_element_type=F32)
        return z * lax.rsqrt(ssq * (1.0 / HEAD_DIM) + EPS) * gain

    def rope_rm(z):
        w = z.shape[1]
        zm = pltpu.roll(z, w - HEAD_DIM // 2, axis=1)
        zp = pltpu.roll(z, HEAD_DIM // 2, axis=1)
        return z * cos_ref[:, :w] + zm * sina_ref[:, :w] + zp * sinb_ref[:, :w]

    ka_ref[0] = head_rms_rm(rm(0, 256), gka_ref[...]).astype(BF16)
    kb_ref[0] = rope_rm(head_rms_rm(rm(256, 512), gkb_ref[...])).astype(BF16)
    zi = rm(512, 640)
    ms = jnp.sum(zi * zi, axis=-1, keepdims=True) * (1.0 / IDX_DIM)
    ki_ref[0] = rope_rm(zi * lax.rsqrt(ms + EPS) * gki_ref[...]).astype(BF16)
    cx_ref[0] = rm(640, 1152)
    cy_ref[0] = rm(1152, 1664)

    def tr(r0, r1):
        return lax.dot_general(wt_ref[r0:r1, :], h, NT_DIMS, preferred_element_type=F32)

    def head_rms_t(z, gain):
        z4 = z.reshape(4, HEAD_DIM, tm)
        ms4 = jnp.mean(z4 * z4, axis=1, keepdims=True)
        return (z4 * lax.rsqrt(ms4 + EPS)).reshape(4 * HEAD_DIM, tm) * gain

    def rope_t(z):
        z4 = z.reshape(4, HEAD_DIM, tm)
        rot = jnp.concatenate([-z4[:, HEAD_DIM // 2:, :], z4[:, :HEAD_DIM // 2, :]], axis=1)
        return (z4 * cosT_ref[...][None] + rot * sinT_ref[...][None]).reshape(4 * HEAD_DIM, tm)

    scale = HEAD_DIM ** -0.5 * LOG2_E
    qaT_ref[0] = (head_rms_t(tr(0, 256), gqaT_ref[...]) * scale).astype(BF16)
    vaT_ref[0] = tr(256, 512).astype(BF16)
    qbT_ref[0] = (rope_t(head_rms_t(tr(512, 768), gqbT_ref[...])) * scale).astype(BF16)
    vbT_ref[0] = tr(768, 1024).astype(BF16)
    qiT_ref[0] = rope_t(tr(1024, 1280)).astype(BF16)
    iwT_ref[0] = tr(1280, 1296)[0:SUBLANES] * (IDX_HEADS ** -0.5 * IDX_DIM ** -0.5)


def _project(layer, x, g, wrm, wt, bd, gka, gkb, gki, gqaT, gqbT, cos, sina, sinb, cosT, sinT, tm):
    b, s, d = x.shape
    grid = (b, s // tm)
    row = lambda w: pl.BlockSpec((1, tm, w), lambda bi, si: (bi, si, 0))
    col = lambda r: pl.BlockSpec((1, r, tm), lambda bi, si: (bi, 0, si))
    tab_rm = pl.BlockSpec((tm, 256), lambda bi, si: (si, 0))
    tab_t = pl.BlockSpec((HEAD_DIM, tm), lambda bi, si: (0, si))
    per_layer = lambda a: _layer_spec(a, layer)
    in_specs = [row(d), per_layer(g), per_layer(wrm), per_layer(wt), _const_spec(bd.shape),
                per_layer(gka), per_layer(gkb), per_layer(gki), per_layer(gqaT), per_layer(gqbT),
                tab_rm, tab_rm, tab_rm, tab_t, tab_t]
    out_shape = (jax.ShapeDtypeStruct((b, s, 256), BF16), jax.ShapeDtypeStruct((b, s, 256), BF16),
                 jax.ShapeDtypeStruct((b, s, 128), BF16),
                 jax.ShapeDtypeStruct((b, s, C_WIDTH), F32), jax.ShapeDtypeStruct((b, s, C_WIDTH), F32),
                 jax.ShapeDtypeStruct((b, 256, s), BF16), jax.ShapeDtypeStruct((b, 256, s), BF16),
                 jax.ShapeDtypeStruct((b, 256, s), BF16), jax.ShapeDtypeStruct((b, 256, s), BF16),
                 jax.ShapeDtypeStruct((b, 256, s), BF16), jax.ShapeDtypeStruct((b, 8, s), F32))
    out_specs = (row(256), row(256), row(128), row(C_WIDTH), row(C_WIDTH),
                 col(256), col(256), col(256), col(256), col(256), col(8))
    return pl.pallas_call(_proj_kernel, grid=grid, in_specs=in_specs, out_specs=out_specs, out_shape=out_shape,
                          compiler_params=_cparams("parallel", "parallel"), name="proj")(
        x, g, wrm, wt, bd, gka, gkb, gki, gqaT, gqbT, cos, sina, sinb, cosT, sinT)


O_AQ, O_AK, O_AV, O_BQ, O_BK, O_BV = 0, 256, 512, 768, 1024, 1280
O_IQ, O_IK, O_IW, O_CX, O_CY, O_GT = 1536, 1792, 1856, 1860, 2372, 2884
W_ROWMAJOR_COLS = 256 + 256 + 128 + 2 * C_WIDTH
W_TRANSPOSED_ROWS = 5 * 256 + LANES
W_PREP_ROWS = 256


def _weight_prep_kernel(w_ref, wrm_ref, wg_ref, wt_ref):
    rows = w_ref.shape[0]
    cols = lambda o, n: w_ref[:, o:o + n]
    wrm_ref[...] = jnp.concatenate(
        [cols(O_AK, 256), cols(O_BK, 256), cols(O_IK, IDX_DIM), jnp.zeros((rows, LANES - IDX_DIM), F32),
         cols(O_CX, C_WIDTH), cols(O_CY, C_WIDTH)], axis=1).astype(BF16)
    wg_ref[...] = w_ref[:, O_GT:].astype(BF16)
    wt = jnp.concatenate([cols(O_AQ, 256), cols(O_AV, 256), cols(O_BQ, 256), cols(O_BV, 256), cols(O_IQ, 256),
                          cols(O_IW, IDX_HEADS), jnp.zeros((rows, LANES - IDX_HEADS), F32)], axis=1)
    wt_ref[...] = wt.T.astype(BF16)


def _weight_prep(w_in):
    depth, d, n = w_in.shape
    n_gate = n - O_GT
    rb = W_PREP_ROWS
    return pl.pallas_call(
        _weight_prep_kernel, grid=(depth, d // rb),
        in_specs=[pl.BlockSpec((None, rb, n), lambda l, i: (l, i, 0))],
        out_specs=(pl.BlockSpec((None, rb, W_ROWMAJOR_COLS), lambda l, i: (l, i, 0)),
                   pl.BlockSpec((None, rb, n_gate), lambda l, i: (l, i, 0)),
                   pl.BlockSpec((None, W_TRANSPOSED_ROWS, rb), lambda l, i: (l, 0, i))),
        out_shape=(jax.ShapeDtypeStruct((depth, d, W_ROWMAJOR_COLS), BF16),
                   jax.ShapeDtypeStruct((depth, d, n_gate), BF16),
                   jax.ShapeDtypeStruct((depth, W_TRANSPOSED_ROWS, d), BF16)),
        compiler_params=_cparams("parallel", "parallel"), name="weight_prep")(w_in)


def _block_diag_q(qT):
    rows = lax.broadcasted_iota(I32, qT.shape, 0) // HEAD_DIM
    zero = jnp.zeros_like(qT)
    return jnp.concatenate([jnp.where(rows == hh, qT, zero) for hh in range(4)], axis=1)


def _chunk_attn_kernel(qT_ref, k_ref, vT_ref, bias_ref, o_ref):
    for sub in range(A_TILES_PER_STEP):
        _chunk_attn_tile(pl.program_id(1) * A_TILES_PER_STEP + sub,
                         qT_ref.at[0, :, sub * Q_TILE:(sub + 1) * Q_TILE], k_ref, vT_ref, bias_ref,
                         o_ref.at[0, sub * Q_TILE:(sub + 1) * Q_TILE, :])


def _chunk_attn_tile(i, qT_ref, k_ref, vT_ref, bias_ref, o_ref):
    qbd = _block_diag_q(qT_ref[...])
    ks, vs, pens = [], [], []
    for t in range(A_KEY_TILES):
        kt = i - (A_KEY_TILES - 1) + t
        r0 = pl.multiple_of(jnp.maximum(kt, 0) * Q_TILE, Q_TILE)
        ks.append(k_ref[0, pl.ds(r0, Q_TILE), :])
        vs.append(vT_ref[0, :, pl.ds(r0, Q_TILE)])
        pens.append(jnp.where(kt >= 0, 0.0, -jnp.inf).astype(F32))
    kwin = jnp.concatenate(ks, axis=0)
    vwin = jnp.concatenate(vs, axis=1)
    s = jnp.dot(kwin, qbd, preferred_element_type=F32) + bias_ref[...]
    s = jnp.concatenate([s[t * Q_TILE:(t + 1) * Q_TILE] + pens[t] for t in range(A_KEY_TILES)], axis=0)
    m = jnp.max(s, axis=0, keepdims=True)
    p = jnp.exp2(s - m)
    l = jnp.sum(p, axis=0, keepdims=True)
    pb = p.astype(BF16)
    outs = []
    for hh in range(A_HEADS):
        lo, hi = hh * Q_TILE, (hh + 1) * Q_TILE
        o = jnp.dot(vwin[hh * HEAD_DIM:(hh + 1) * HEAD_DIM, :], pb[:, lo:hi], preferred_element_type=F32)
        outs.append(o / l[:, lo:hi])
    o_ref[...] = jnp.concatenate(outs, axis=0).T.astype(BF16)


def _chunk_attention(layer, qaT, ka, vaT, biasT):
    b, _, s = qaT.shape
    step = A_TILES_PER_STEP * Q_TILE
    grid = (b, s // step)
    return pl.pallas_call(
        _chunk_attn_kernel, grid=grid,
        in_specs=[pl.BlockSpec((1, 256, step), lambda bi, i: (bi, 0, i)),
                  pl.BlockSpec((1, s, 256), lambda bi, i: (bi, 0, 0)),
                  pl.BlockSpec((1, 256, s), lambda bi, i: (bi, 0, 0)),
                  _layer_spec(biasT, layer)],
        out_specs=pl.BlockSpec((1, step, 256), lambda bi, i: (bi, i, 0)),
        out_shape=jax.ShapeDtypeStruct((b, s, 256), BF16),
        compiler_params=_cparams("parallel", "parallel"), name="chunk_attn")(qaT, ka, vaT, biasT)


def _bit_transpose32(words):
    a = list(words)
    mask, j = 0x0000FFFF, 16
    while j:
        k = 0
        while k < 32:
            t = (a[k] ^ (a[k + j] >> jnp.uint32(j))) & jnp.uint32(mask)
            a[k] = a[k] ^ t
            a[k + j] = a[k + j] ^ (t << jnp.uint32(j))
            k = (k + j + 1) & ~j
        j >>= 1
        mask = (mask ^ (mask << j)) & 0xFFFFFFFF
    return a


def _dsa_kernel(qiT_ref, iwT_ref, qbT_ref, ki_ref, kb_ref, vbT_ref, o_ref, scores_ref, planes_ref, *, topk):
    tk = B_KEY_TILE
    pair = pl.program_id(1)
    tiles = range(B_TILES_PER_STEP)
    n_small = pair + 1
    n_wide = lax.shift_right_logical(n_small, 2)
    has_mid, has_small = (n_small & 2) != 0, (n_small & 1) != 0
    lane = lax.broadcasted_iota(I32, (1, Q_TILE), 1)
    qis, ws, key_limits = [], [], []
    for tq in tiles:
        qiT = qiT_ref[0, :, tq * Q_TILE:(tq + 1) * Q_TILE]
        qi = jnp.concatenate([qiT[hh * IDX_DIM:(hh + 1) * IDX_DIM] for hh in range(IDX_HEADS)], axis=1)
        qis.append(jnp.concatenate([qi, jnp.zeros_like(qi)], axis=0))
        ws.append(iwT_ref[0, :, tq * Q_TILE:(tq + 1) * Q_TILE])
        i = pair * B_TILES_PER_STEP + tq
        key_limits.append(i * Q_TILE + CHUNK + jnp.where(lane >= CHUNK, CHUNK, 0))

    def over_key_tiles(span, carry):
        wide, small = 2 * tk, tk // 2
        carry = lax.fori_loop(0, n_wide, lambda u, c: span(pl.multiple_of(u * wide, wide), wide, c), carry)
        r_mid = pl.multiple_of(n_wide * wide, wide)
        carry = lax.cond(has_mid, lambda c: span(r_mid, tk, c), lambda c: c, carry)
        r_small = pl.multiple_of(r_mid + jnp.where(has_mid, tk, 0), small)
        return lax.cond(has_small, lambda c: span(r_small, small, c), lambda c: c, carry)

    def score_span(r0, rows, c):
        ki = ki_ref[0, pl.ds(r0, rows), :]
        row_iota = lax.broadcasted_iota(I32, (rows, Q_TILE), 0)
        for tq in tiles:
            dots = jnp.dot(ki, qis[tq], preferred_element_type=F32)
            sc = jnp.maximum(dots[:, 0:Q_TILE], 0.0) * ws[tq][0:1, :]
            for hh in range(1, IDX_HEADS):
                sc = sc + jnp.maximum(dots[:, hh * Q_TILE:(hh + 1) * Q_TILE], 0.0) * ws[tq][hh:hh + 1, :]
            sc = jnp.where(row_iota < key_limits[tq] - r0, sc, -jnp.inf)
            scores_ref[tq, pl.ds(r0, rows), :] = sc
            bits = lax.bitcast_convert_type(sc, U32)
            for g0 in range(0, rows, PLANE_GROUP):
                words = [bits[g0 + j * SUBLANES:g0 + (j + 1) * SUBLANES] for j in range(32)]
                planes = _bit_transpose32(words)
                magnitude = planes[1]
                for p in planes[2:]:
                    magnitude = magnitude | p
                negative = planes[0] & magnitude
                planes = [~negative] + [p ^ negative for p in planes[1:]]
                planes_ref[tq, pl.ds(r0 + g0, PLANE_GROUP), :] = jnp.concatenate(planes, axis=0)
        return c

    @pl.when(pair == 0)
    def _():
        planes_ref[...] = jnp.zeros_like(planes_ref)

    over_key_tiles(score_span, 0)

    max_groups = planes_ref.shape[1] // PLANE_GROUP
    full = jnp.full((SUBLANES, Q_TILE), 0xFFFFFFFF, U32)
    zero = jnp.zeros((1, Q_TILE), I32)

    def members(masks):
        parts = [lax.population_count(m) for m in masks]
        while len(parts) > 1:
            parts = [parts[j] + parts[j + 1] for j in range(0, len(parts) - 1, 2)] + parts[len(parts) & ~1:]
        return jnp.sum(parts[0].astype(I32), axis=0, keepdims=True)

    def search(n_groups):
        live0 = tuple(jnp.where(g < n_small, full, jnp.zeros_like(full)) for g in range(n_groups))

        def two_bits_tile(tq, row, bit_hi, state):
            live, n_gt, thr_bits = state
            p_hi = [planes_ref[tq, pl.ds(g * PLANE_GROUP + row, SUBLANES), :] for g in range(n_groups)]
            p_lo = [planes_ref[tq, pl.ds(g * PLANE_GROUP + row + SUBLANES, SUBLANES), :] for g in range(n_groups)]
            set_hi = [live[g] & p_hi[g] for g in range(n_groups)]
            clr_hi = [live[g] ^ set_hi[g] for g in range(n_groups)]
            set_both = [set_hi[g] & p_lo[g] for g in range(n_groups)]
            clr_set = [clr_hi[g] & p_lo[g] for g in range(n_groups)]
            n_hi = n_gt + members(set_hi)
            take_hi = n_hi >= topk
            n_gt = jnp.where(take_hi, n_gt, n_hi)
            n_lo = n_gt + jnp.where(take_hi, members(set_both), members(clr_set))
            take_lo = n_lo >= topk
            new_live = []
            for g in range(n_groups):
                kept = jnp.where(take_hi, set_hi[g], clr_hi[g])
                low = jnp.where(take_hi, set_both[g], clr_set[g])
                new_live.append(jnp.where(take_lo, low, kept ^ low))
            bit_lo = lax.shift_right_logical(bit_hi, 1)
            thr_bits = thr_bits | jnp.where(take_hi, bit_hi, 0) | jnp.where(take_lo, bit_lo, 0)
            return tuple(new_live), jnp.where(take_lo, n_gt, n_lo), thr_bits

        def two_bits(it, states):
            row = pl.multiple_of(it * (2 * SUBLANES), 2 * SUBLANES)
            bit_hi = jnp.int32(1) << (31 - 2 * it)
            return tuple(two_bits_tile(tq, row, bit_hi, states[tq]) for tq in tiles)

        states = lax.fori_loop(0, 16, two_bits, tuple((live0, zero, zero) for _ in tiles))
        return tuple((n_gt, thr_bits) for _, n_gt, thr_bits in states)

    if max_groups % 2 == 0:
        found = lax.cond(n_small <= max_groups // 2, lambda: search(max_groups // 2), lambda: search(max_groups))
    else:
        found = search(max_groups)
    thrs, needs = [], []
    for n_gt, thr_bits in found:
        thr = lax.bitcast_convert_type(jnp.where(thr_bits < 0, thr_bits ^ jnp.int32(INT_MIN), ~thr_bits), F32)
        thrs.append(thr)
        needs.append(jnp.where(thr == -jnp.inf, 0, topk - n_gt).astype(F32))
    hk = tk // 2
    tril = (lax.broadcasted_iota(I32, (hk, hk), 0) >= lax.broadcasted_iota(I32, (hk, hk), 1)).astype(BF16)

    qbds = [_block_diag_q(qbT_ref[0, :, tq * Q_TILE:(tq + 1) * Q_TILE]) for tq in tiles]

    def attn_span_tile(tq, r0, rows, kb, carry):
        ms, ls, accs, seen = carry
        key = scores_ref[tq, pl.ds(r0, rows), :]
        tied = key == thrs[tq]
        tied_b = jnp.where(tied, 1.0, 0.0).astype(BF16)
        ranks = []
        for c0 in range(0, rows, hk):
            ranks.append(jnp.dot(tril, tied_b[c0:c0 + hk], preferred_element_type=F32) + seen)
            seen = ranks[-1][hk - 1:hk, :]
        sel = jnp.logical_or(key > thrs[tq], jnp.logical_and(tied, jnp.concatenate(ranks, axis=0) <= needs[tq]))
        s_all = jnp.dot(kb, qbds[tq], preferred_element_type=F32)
        new_ms, new_ls, new_accs = [], [], []
        for hh in range(B_HEADS):
            s = jnp.where(sel, s_all[:, hh * Q_TILE:(hh + 1) * Q_TILE], -jnp.inf)
            m_new = jnp.maximum(ms[hh], jnp.max(s, axis=0, keepdims=True))
            alpha = jnp.exp2(ms[hh] - m_new)
            p = jnp.exp2(s - m_new)
            new_ls.append(alpha * ls[hh] + jnp.sum(p, axis=0, keepdims=True))
            pv = jnp.dot(vbT_ref[0, hh * HEAD_DIM:(hh + 1) * HEAD_DIM, pl.ds(r0, rows)], p.astype(BF16),
                         preferred_element_type=F32)
            new_accs.append(alpha * accs[hh] + pv)
            new_ms.append(m_new)
        return tuple(new_ms), tuple(new_ls), tuple(new_accs), seen

    def attn_span(r0, rows, carries):
        kb = kb_ref[0, pl.ds(r0, rows), :]
        return tuple(attn_span_tile(tq, r0, rows, kb, carries[tq]) for tq in tiles)

    init = (tuple(jnp.full((1, Q_TILE), -1e30, F32) for _ in range(B_HEADS)),
            tuple(jnp.zeros((1, Q_TILE), F32) for _ in range(B_HEADS)),
            tuple(jnp.zeros((HEAD_DIM, Q_TILE), F32) for _ in range(B_HEADS)),
            jnp.zeros((1, Q_TILE), F32))
    finals = over_key_tiles(attn_span, tuple(init for _ in tiles))
    for tq in tiles:
        _, ls, accs, _ = finals[tq]
        out = jnp.concatenate([accs[hh] / ls[hh] for hh in range(B_HEADS)], axis=0)
        o_ref[0, tq * Q_TILE:(tq + 1) * Q_TILE, :] = out.T.astype(BF16)


def _sparse_attention(qiT, iwT, qbT, ki, kb, vbT):
    b, _, s = qbT.shape
    s_pad = -(-s // B_KEY_TILE) * B_KEY_TILE
    topk = min(TOPK_MAX, s // 4)
    step = B_TILES_PER_STEP * Q_TILE
    grid = (b, s // step)
    qspec = lambda r: pl.BlockSpec((1, r, step), lambda bi, i: (bi, 0, i))
    kernel = functools.partial(_dsa_kernel, topk=topk)
    return pl.pallas_call(
        kernel, grid=grid,
        in_specs=[qspec(256), qspec(8), qspec(256),
                  pl.BlockSpec((1, s, 128), lambda bi, i: (bi, 0, 0)),
                  pl.BlockSpec((1, s, 256), lambda bi, i: (bi, 0, 0)),
                  pl.BlockSpec((1, 256, s), lambda bi, i: (bi, 0, 0))],
        out_specs=pl.BlockSpec((1, step, 256), lambda bi, i: (bi, i, 0)),
        out_shape=jax.ShapeDtypeStruct((b, s, 256), BF16),
        scratch_shapes=[pltpu.VMEM((B_TILES_PER_STEP, s_pad, Q_TILE), F32),
                        pltpu.VMEM((B_TILES_PER_STEP, s_pad, Q_TILE), U32)],
        compiler_params=_cparams("parallel", "arbitrary"), name="sparse_attn")(qiT, iwT, qbT, ki, kb, vbT)


def _rglru_kernel(cx_ref, cy_ref, cw_ref, cb_ref, wax_ref, ba_ref, bx_ref, lam_ref, o_ref,
                  tail_ref, h_ref, a_ref, g_ref):
    ts = cx_ref.shape[1]

    @pl.when(pl.program_id(1) == 0)
    def _():
        tail_ref[...] = jnp.zeros_like(tail_ref)
        h_ref[...] = jnp.zeros_like(h_ref)

    x = cx_ref[0]
    tail = tail_ref[...]
    row8 = lax.broadcasted_iota(I32, (SUBLANES, C_WIDTH), 0)
    u = cb_ref[...] + x * cw_ref[C_CONV - 1:C_CONV, :]
    for k in range(1, C_CONV):
        xr = pltpu.roll(x, k, axis=0)
        head = jnp.where(row8 < k, pltpu.roll(tail, k, axis=0), xr[0:SUBLANES])
        xk = jnp.concatenate([head, xr[SUBLANES:]], axis=0)
        u = u + xk * cw_ref[C_CONV - 1 - k:C_CONV - k, :]
    tail_ref[...] = x[ts - SUBLANES:ts]

    gates = jnp.dot(u.astype(BF16), wax_ref[...], preferred_element_type=F32)
    r = jax.nn.sigmoid(gates[:, :C_WIDTH] + ba_ref[...])
    ig = jax.nn.sigmoid(gates[:, C_WIDTH:] + bx_ref[...])
    nlam = -lam_ref[...]
    softplus = jnp.maximum(nlam, 0.0) + jnp.log1p(jnp.exp(-jnp.abs(nlam)))
    a = jnp.exp(-LRU_C * r * softplus)
    a_ref[...] = a
    g_ref[...] = jnp.sqrt(1.0 - a * a) * (ig * u)

    def group(gi, hprev):
        r0 = pl.multiple_of(gi * SUBLANES, SUBLANES)
        av = a_ref[pl.ds(r0, SUBLANES), :]
        bv = g_ref[pl.ds(r0, SUBLANES), :]
        for d in (1, 2, 4):
            keep = row8 >= d
            a_sh = jnp.where(keep, pltpu.roll(av, d, axis=0), 1.0)
            b_sh = jnp.where(keep, pltpu.roll(bv, d, axis=0), 0.0)
            bv = av * b_sh + bv
            av = av * a_sh
        hs = av * hprev + bv
        g_ref[pl.ds(r0, SUBLANES), :] = hs
        return jnp.broadcast_to(hs[SUBLANES - 1:SUBLANES, :], hs.shape)

    h_ref[...] = lax.fori_loop(0, ts // SUBLANES, group, h_ref[...], unroll=8)
    o_ref[0] = (g_ref[...] * jax.nn.gelu(cy_ref[0])).astype(BF16)


def _rglru(layer, cx, cy, cw, cb, wax, ba, bx, lam, ts):
    b, s, c = cx.shape
    blk = pl.BlockSpec((1, ts, c), lambda bi, si: (bi, si, 0))
    return pl.pallas_call(
        _rglru_kernel, grid=(b, s // ts),
        in_specs=[blk, blk] + [_layer_spec(a, layer) for a in (cw, cb, wax, ba, bx, lam)],
        out_specs=blk, out_shape=jax.ShapeDtypeStruct((b, s, c), BF16),
        scratch_shapes=[pltpu.VMEM((SUBLANES, c), F32), pltpu.VMEM((SUBLANES, c), F32),
                        pltpu.VMEM((ts, c), F32), pltpu.VMEM((ts, c), F32)],
        compiler_params=_cparams("parallel", "arbitrary"), name="rglru")(cx, cy, cw, cb, wax, ba, bx, lam)


def _merge_kernel(x_ref, ya_ref, yb_ref, yc_ref, g_ref, wg_ref, bg_ref, wb_ref, wo_ref, o_ref):
    x = x_ref[...]
    d = x.shape[1]
    h = _rms(x, g_ref[...]).astype(BF16)
    ys = (ya_ref[...], yb_ref[...], yc_ref[...])
    offs = (0, A_WIDTH, A_WIDTH + B_WIDTH, A_WIDTH + B_WIDTH + C_WIDTH)
    merged = None
    for j in range(N_BRANCH):
        gate = jax.nn.sigmoid(jnp.dot(h, wg_ref[:, j * d:(j + 1) * d], preferred_element_type=F32)
                              + bg_ref[:, j * d:(j + 1) * d])
        term = gate * jnp.dot(ys[j], wb_ref[offs[j]:offs[j + 1], :], preferred_element_type=F32)
        merged = term if merged is None else merged + term
    o_ref[...] = x + jnp.dot(merged.astype(BF16), wo_ref[...], preferred_element_type=F32)


def _merge(layer, x2, ya, yb, yc, g, wg, bg, wb, wo, tm):
    n, d = x2.shape
    row = lambda w: pl.BlockSpec((tm, w), lambda i: (i, 0))
    return pl.pallas_call(
        _merge_kernel, grid=(n // tm,),
        in_specs=[row(d), row(A_WIDTH), row(B_WIDTH), row(C_WIDTH)]
        + [_layer_spec(a, layer) for a in (g, wg, bg, wb, wo)],
        out_specs=row(d), out_shape=jax.ShapeDtypeStruct((n, d), F32),
        compiler_params=_cparams("parallel"), name="merge")(x2, ya, yb, yc, g, wg, bg, wb, wo)


def _ffn_kernel(x_ref, g_ref, wi_ref, wo_ref, o_ref, *, d_ff, chunk):
    x = x_ref[...]
    h = _rms(x, g_ref[...]).astype(BF16)
    acc = x
    for c0 in range(0, d_ff, chunk):
        c1 = min(c0 + chunk, d_ff)
        gp = jnp.dot(h, wi_ref[:, c0:c1], preferred_element_type=F32)
        up = jnp.dot(h, wi_ref[:, d_ff + c0:d_ff + c1], preferred_element_type=F32)
        act = (jax.nn.silu(gp) * up).astype(BF16)
        acc = acc + jnp.dot(act, wo_ref[c0:c1, :], preferred_element_type=F32)
    o_ref[...] = acc


def _ffn(layer, x2, g, wi, wo, tm):
    n, d = x2.shape
    d_ff = wo.shape[1]
    row = pl.BlockSpec((tm, d), lambda i: (i, 0))
    kernel = functools.partial(_ffn_kernel, d_ff=d_ff, chunk=1024)
    return pl.pallas_call(
        kernel, grid=(n // tm,),
        in_specs=[row] + [_layer_spec(a, layer) for a in (g, wi, wo)],
        out_specs=row, out_shape=jax.ShapeDtypeStruct((n, d), F32),
        compiler_params=_cparams("parallel"), name="ffn")(x2, g, wi, wo)


def _rope_tables(s):
    pos = jnp.arange(s, dtype=F32)
    inv = 1.0 / (ROPE_THETA ** (jnp.arange(0, HEAD_DIM, 2, dtype=F32) / HEAD_DIM))
    ang = pos[:, None] * inv[None, :]
    cos, sin = jnp.cos(ang), jnp.sin(ang)
    return jnp.concatenate([cos, cos], axis=-1), jnp.concatenate([sin, sin], axis=-1)


def _bias_table_kernel(vec_ref, o_ref):
    nk, q_tile = o_ref.shape
    rolled = pltpu.roll(jnp.broadcast_to(vec_ref[...], (nk, nk + q_tile)), 0, axis=1, stride=1, stride_axis=0)
    j = lax.broadcasted_iota(I32, (nk, q_tile), 0)
    q = lax.broadcasted_iota(I32, (nk, q_tile), 1)
    shift = CHUNK.bit_length() - 1
    dchunk = A_LEFT_CHUNKS + (q >> shift) - (j >> shift)
    valid = jnp.logical_and(dchunk >= 0, dchunk <= A_LEFT_CHUNKS)
    o_ref[...] = jnp.where(valid, rolled[:, nk:] * LOG2_E, -jnp.inf)


def _rel_bias_tables(rel_bias):
    depth = rel_bias.shape[0]
    nk = A_KEY_TILES * Q_TILE
    assert REL_CLIP == Q_TILE and CHUNK & (CHUNK - 1) == 0
    rb = rel_bias.astype(F32)
    vec = jnp.concatenate([rb, jnp.tile(rb[..., -1:], (1, 1, nk + Q_TILE - rb.shape[-1]))], axis=-1)
    return pl.pallas_call(
        _bias_table_kernel, grid=(depth, A_HEADS),
        in_specs=[pl.BlockSpec((None, None, 1, nk + Q_TILE), lambda l, h: (l, h, 0, 0))],
        out_specs=pl.BlockSpec((None, nk, Q_TILE), lambda l, h: (l, 0, h)),
        out_shape=jax.ShapeDtypeStruct((depth, nk, A_HEADS * Q_TILE), F32),
        compiler_params=_cparams("parallel", "parallel"), name="bias_table")(vec[:, :, None, :])


def kernel(x, g_mix, w_in, qk_gain_a, rel_bias, qk_gain_b, g_idx_k, conv_w, conv_b, lru_wa, lru_ba, lru_wx,
           lru_bx, lru_lambda, b_gate, w_branch, w_out, g_ffn, w_ffn_in, w_ffn_out):
    b, s, d = x.shape
    depth = g_mix.shape[0]
    assert s % B_KEY_TILE == 0 and d % LANES == 0
    tm = 512
    cos, sin = _rope_tables(s)
    half = (jnp.arange(HEAD_DIM) < HEAD_DIM // 2).astype(F32)
    cos_rm = jnp.tile(cos, (1, 4))
    sina_rm = jnp.tile(-sin * half[None, :], (1, 4))
    sinb_rm = jnp.tile(sin * (1.0 - half)[None, :], (1, 4))
    cosT, sinT = cos.T, sin.T
    hid = jnp.arange(256) // HEAD_DIM
    bd = (hid[:, None] == hid[None, :]).astype(BF16)
    rows = lambda v: v.reshape(depth, 1, -1).astype(F32)
    blocks = jnp.arange(C_WIDTH) // (C_WIDTH // C_BLOCKS)
    bmask = (blocks[:, None] == blocks[None, :])

    def dense_blocks(wblk):
        return jnp.where(bmask, jnp.tile(wblk.reshape(depth, C_WIDTH, C_WIDTH // C_BLOCKS), (1, 1, C_BLOCKS)), 0.0)

    wrm, w_gate, wt = _weight_prep(w_in)
    gka, gkb = rows(jnp.tile(qk_gain_a[:, 1], (1, 4))), rows(jnp.tile(qk_gain_b[:, 1], (1, 4)))
    gki = rows(jnp.concatenate([g_idx_k, jnp.zeros((depth, 64), F32)], axis=1))
    gqaT = jnp.tile(qk_gain_a[:, 0], (1, 4)).reshape(depth, 256, 1)
    gqbT = jnp.tile(qk_gain_b[:, 0], (1, 4)).reshape(depth, 256, 1)
    bias_tabs = _rel_bias_tables(rel_bias)
    wax = jnp.concatenate([dense_blocks(lru_wa), dense_blocks(lru_wx)], axis=2).astype(BF16)
    g_mix3, g_ffn3, b_gate3 = rows(g_mix), rows(g_ffn), rows(b_gate)
    cb3, ba3, bx3, lam3 = rows(conv_b), rows(lru_ba), rows(lru_bx), rows(lru_lambda)
    wb, wo = w_branch.astype(BF16), w_out.astype(BF16)
    wfi, wfo = w_ffn_in.astype(BF16), w_ffn_out.astype(BF16)

    n = b * s
    for l in range(depth):
        ka, kb, ki, cx, cy, qaT, vaT, qbT, vbT, qiT, iwT = _project(
            l, x, g_mix3, wrm, wt, bd, gka, gkb, gki, gqaT, gqbT, cos_rm, sina_rm, sinb_rm, cosT, sinT, tm)
        ya = _chunk_attention(l, qaT, ka, vaT, bias_tabs)
        yb = _sparse_attention(qiT, iwT, qbT, ki, kb, vbT)
        yc = _rglru(l, cx, cy, conv_w, cb3, wax, ba3, bx3, lam3, 512)
        x1 = _merge(l, x.reshape(n, d), ya.reshape(n, A_WIDTH), yb.reshape(n, B_WIDTH), yc.reshape(n, C_WIDTH),
                    g_mix3, w_gate, b_gate3, wb, wo, tm)
        x = _ffn(l, x1, g_ffn3, wfi, wfo, tm).reshape(b, s, d)
    return x
```

```python
import functools
import math

import jax
import jax.numpy as jnp
from jax import lax
from jax.experimental import pallas as pl
from jax.experimental.pallas import tpu as pltpu

F32 = jnp.float32
BF16 = jnp.bfloat16
I32 = jnp.int32
U32 = jnp.uint32

CHUNK = 64
HEAD_DIM = 64
EPS = 1e-6
ROPE_THETA = 10000.0
A_HEADS = 4
A_LEFT_CHUNKS = 8
REL_CLIP = 128
B_HEADS = 4
IDX_HEADS = 4
IDX_DIM = 64
TOPK_MAX = 256
C_WIDTH = 512
C_BLOCKS = 8
C_CONV = 4
LRU_C = 8.0
N_BRANCH = 3
A_WIDTH = A_HEADS * HEAD_DIM
B_WIDTH = B_HEADS * HEAD_DIM

LANES = 128
SUBLANES = 8
Q_TILE = 128
A_KEY_TILES = A_LEFT_CHUNKS * CHUNK // Q_TILE + 1
A_TILES_PER_STEP = 4
B_TILES_PER_STEP = 2
B_KEY_TILE = 512
PLANE_GROUP = 32 * SUBLANES
LOG2_E = math.log2(math.e)
INT_MIN = -2 ** 31
VMEM_LIMIT = 56 * 1024 * 1024

NT_DIMS = (((1,), (1,)), ((), ()))


def _cparams(*sem):
    return pltpu.CompilerParams(dimension_semantics=sem, vmem_limit_bytes=VMEM_LIMIT)


def _const_spec(shape):
    nd = len(shape)
    return pl.BlockSpec(shape, lambda *_: (0,) * nd, pipeline_mode=pl.Buffered(1))


def _layer_spec(stacked, layer):
    nd = stacked.ndim - 1
    return pl.BlockSpec((None,) + stacked.shape[1:], lambda *_: (layer,) + (0,) * nd, pipeline_mode=pl.Buffered(1))


def _sigmoid(x):
    return 0.5 * jnp.tanh(0.5 * x) + 0.5


def _rms(x, gain):
    ms = jnp.mean(x * x, axis=-1, keepdims=True)
    return x * lax.rsqrt(ms + EPS) * gain


def _proj_kernel(x_ref, g_ref, wrm_ref, wt_ref, bd_ref, gka_ref, gkb_ref, gki_ref, gqaT_ref, gqbT_ref,
                 cos_ref, sina_ref, sinb_ref, cosT_ref, sinT_ref,
                 ka_ref, kb_ref, ki_ref, cx_ref, cy_ref,
                 qaT_ref, vaT_ref, qbT_ref, vbT_ref, qiT_ref, iwT_ref):
    tm = x_ref.shape[1]
    h = _rms(x_ref[0], g_ref[...]).astype(BF16)

    def rm(c0, c1):
        return jnp.dot(h, wrm_ref[:, c0:c1], preferred_element_type=F32)

    def head_rms_rm(z, gain):
        ssq = jnp.dot((z * z).astype(BF16), bd_ref[...], preferred_element_type=F32)
        return z * lax.rsqrt(ssq * (1.0 / HEAD_DIM) + EPS) * gain

    def rope_rm(z):
        w = z.shape[1]
        zm = pltpu.roll(z, w - HEAD_DIM // 2, axis=1)
        zp = pltpu.roll(z, HEAD_DIM // 2, axis=1)
        return z * cos_ref[:, :w] + zm * sina_ref[:, :w] + zp * sinb_ref[:, :w]

    ka_ref[0] = head_rms_rm(rm(0, 256), gka_ref[...]).astype(BF16)
    kb_ref[0] = rope_rm(head_rms_rm(rm(256, 512), gkb_ref[...])).astype(BF16)
    zi = rm(512, 640)
    ms = jnp.sum(zi * zi, axis=-1, keepdims=True) * (1.0 / IDX_DIM)
    ki_ref[0] = rope_rm(zi * lax.rsqrt(ms + EPS) * gki_ref[...]).astype(BF16)
    cx_ref[0] = rm(640, 1152)
    cy_ref[0] = rm(1152, 1664)

    def tr(r0, r1):
        return lax.dot_general(wt_ref[r0:r1, :], h, NT_DIMS, preferred_element_type=F32)

    def head_rms_t(z, gain):
        z4 = z.reshape(4, HEAD_DIM, tm)
        ms4 = jnp.mean(z4 * z4, axis=1, keepdims=True)
        return (z4 * lax.rsqrt(ms4 + EPS)).reshape(4 * HEAD_DIM, tm) * gain

    def rope_t(z):
        z4 = z.reshape(4, HEAD_DIM, tm)
        rot = jnp.concatenate([-z4[:, HEAD_DIM // 2:, :], z4[:, :HEAD_DIM // 2, :]], axis=1)
        return (z4 * cosT_ref[...][None] + rot * sinT_ref[...][None]).reshape(4 * HEAD_DIM, tm)

    scale = HEAD_DIM ** -0.5 * LOG2_E
    qaT_ref[0] = (head_rms_t(tr(0, 256), gqaT_ref[...]) * scale).astype(BF16)
    vaT_ref[0] = tr(256, 512).astype(BF16)
    qbT_ref[0] = (rope_t(head_rms_t(tr(512, 768), gqbT_ref[...])) * scale).astype(BF16)
    vbT_ref[0] = tr(768, 1024).astype(BF16)
    qiT_ref[0] = rope_t(tr(1024, 1280)).astype(BF16)
    iwT_ref[0] = tr(1280, 1296)[0:SUBLANES] * (IDX_HEADS ** -0.5 * IDX_DIM ** -0.5)


def _project(layer, x, g, wrm, wt, bd, gka, gkb, gki, gqaT, gqbT, cos, sina, sinb, cosT, sinT, tm):
    b, s, d = x.shape
    grid = (b, s // tm)
    row = lambda w: pl.BlockSpec((1, tm, w), lambda bi, si: (bi, si, 0))
    col = lambda r: pl.BlockSpec((1, r, tm), lambda bi, si: (bi, 0, si))
    tab_rm = pl.BlockSpec((tm, 256), lambda bi, si: (si, 0))
    tab_t = pl.BlockSpec((HEAD_DIM, tm), lambda bi, si: (0, si))
    per_layer = lambda a: _layer_spec(a, layer)
    in_specs = [row(d), per_layer(g), per_layer(wrm), per_layer(wt), _const_spec(bd.shape),
                per_layer(gka), per_layer(gkb), per_layer(gki), per_layer(gqaT), per_layer(gqbT),
                tab_rm, tab_rm, tab_rm, tab_t, tab_t]
    out_shape = (jax.ShapeDtypeStruct((b, s, 256), BF16), jax.ShapeDtypeStruct((b, s, 256), BF16),
                 jax.ShapeDtypeStruct((b, s, 128), BF16),
                 jax.ShapeDtypeStruct((b, s, C_WIDTH), F32), jax.ShapeDtypeStruct((b, s, C_WIDTH), F32),
                 jax.ShapeDtypeStruct((b, 256, s), BF16), jax.ShapeDtypeStruct((b, 256, s), BF16),
                 jax.ShapeDtypeStruct((b, 256, s), BF16), jax.ShapeDtypeStruct((b, 256, s), BF16),
                 jax.ShapeDtypeStruct((b, 256, s), BF16), jax.ShapeDtypeStruct((b, 8, s), F32))
    out_specs = (row(256), row(256), row(128), row(C_WIDTH), row(C_WIDTH),
                 col(256), col(256), col(256), col(256), col(256), col(8))
    return pl.pallas_call(_proj_kernel, grid=grid, in_specs=in_specs, out_specs=out_specs, out_shape=out_shape,
                          compiler_params=_cparams("parallel", "parallel"), name="proj")(
        x, g, wrm, wt, bd, gka, gkb, gki, gqaT, gqbT, cos, sina, sinb, cosT, sinT)


O_AQ, O_AK, O_AV, O_BQ, O_BK, O_BV = 0, 256, 512, 768, 1024, 1280
O_IQ, O_IK, O_IW, O_CX, O_CY, O_GT = 1536, 1792, 1856, 1860, 2372, 2884
W_ROWMAJOR_COLS = 256 + 256 + 128 + 2 * C_WIDTH
W_TRANSPOSED_ROWS = 5 * 256 + LANES
W_PREP_ROWS = 256


def _weight_prep_kernel(w_ref, wrm_ref, wg_ref, wt_ref):
    rows = w_ref.shape[0]
    cols = lambda o, n: w_ref[:, o:o + n]
    wrm_ref[...] = jnp.concatenate(
        [cols(O_AK, 256), cols(O_BK, 256), cols(O_IK, IDX_DIM), jnp.zeros((rows, LANES - IDX_DIM), F32),
         cols(O_CX, C_WIDTH), cols(O_CY, C_WIDTH)], axis=1).astype(BF16)
    wg_ref[...] = w_ref[:, O_GT:].astype(BF16)
    wt = jnp.concatenate([cols(O_AQ, 256), cols(O_AV, 256), cols(O_BQ, 256), cols(O_BV, 256), cols(O_IQ, 256),
                          cols(O_IW, IDX_HEADS), jnp.zeros((rows, LANES - IDX_HEADS), F32)], axis=1)
    wt_ref[...] = wt.T.astype(BF16)


def _weight_prep(w_in):
    depth, d, n = w_in.shape
    n_gate = n - O_GT
    rb = W_PREP_ROWS
    return pl.pallas_call(
        _weight_prep_kernel, grid=(depth, d // rb),
        in_specs=[pl.BlockSpec((None, rb, n), lambda l, i: (l, i, 0))],
        out_specs=(pl.BlockSpec((None, rb, W_ROWMAJOR_COLS), lambda l, i: (l, i, 0)),
                   pl.BlockSpec((None, rb, n_gate), lambda l, i: (l, i, 0)),
                   pl.BlockSpec((None, W_TRANSPOSED_ROWS, rb), lambda l, i: (l, 0, i))),
        out_shape=(jax.ShapeDtypeStruct((depth, d, W_ROWMAJOR_COLS), BF16),
                   jax.ShapeDtypeStruct((depth, d, n_gate), BF16),
                   jax.ShapeDtypeStruct((depth, W_TRANSPOSED_ROWS, d), BF16)),
        compiler_params=_cparams("parallel", "parallel"), name="weight_prep")(w_in)


def _block_diag_q(qT):
    rows = lax.broadcasted_iota(I32, qT.shape, 0) // HEAD_DIM
    zero = jnp.zeros_like(qT)
    return jnp.concatenate([jnp.where(rows == hh, qT, zero) for hh in range(4)], axis=1)


def _chunk_attn_kernel(qT_ref, k_ref, vT_ref, bias_ref, o_ref):
    for sub in range(A_TILES_PER_STEP):
        _chunk_attn_tile(pl.program_id(1) * A_TILES_PER_STEP + sub,
                         qT_ref.at[0, :, sub * Q_TILE:(sub + 1) * Q_TILE], k_ref, vT_ref, bias_ref,
                         o_ref.at[0, sub * Q_TILE:(sub + 1) * Q_TILE, :])


def _chunk_attn_tile(i, qT_ref, k_ref, vT_ref, bias_ref, o_ref):
    qbd = _block_diag_q(qT_ref[...])
    ks, vs, pens = [], [], []
    for t in range(A_KEY_TILES):
        kt = i - (A_KEY_TILES - 1) + t
        r0 = pl.multiple_of(jnp.maximum(kt, 0) * Q_TILE, Q_TILE)
        ks.append(k_ref[0, pl.ds(r0, Q_TILE), :])
        vs.append(vT_ref[0, :, pl.ds(r0, Q_TILE)])
        pens.append(jnp.where(kt >= 0, 0.0, -jnp.inf).astype(F32))
    kwin = jnp.concatenate(ks, axis=0)
    vwin = jnp.concatenate(vs, axis=1)
    s = jnp.dot(kwin, qbd, preferred_element_type=F32) + bias_ref[...]
    s = jnp.concatenate([s[t * Q_TILE:(t + 1) * Q_TILE] + pens[t] for t in range(A_KEY_TILES)], axis=0)
    m = jnp.max(s, axis=0, keepdims=True)
    p = jnp.exp2(s - m)
    l = jnp.sum(p, axis=0, keepdims=True)
    pb = p.astype(BF16)
    outs = []
    for hh in range(A_HEADS):
        lo, hi = hh * Q_TILE, (hh + 1) * Q_TILE
        o = jnp.dot(vwin[hh * HEAD_DIM:(hh + 1) * HEAD_DIM, :], pb[:, lo:hi], preferred_element_type=F32)
        outs.append(o / l[:, lo:hi])
    o_ref[...] = jnp.concatenate(outs, axis=0).T.astype(BF16)


def _chunk_attention(layer, qaT, ka, vaT, biasT):
    b, _, s = qaT.shape
    step = A_TILES_PER_STEP * Q_TILE
    grid = (b, s // step)
    return pl.pallas_call(
        _chunk_attn_kernel, grid=grid,
        in_specs=[pl.BlockSpec((1, 256, step), lambda bi, i: (bi, 0, i)),
                  pl.BlockSpec((1, s, 256), lambda bi, i: (bi, 0, 0)),
                  pl.BlockSpec((1, 256, s), lambda bi, i: (bi, 0, 0)),
                  _layer_spec(biasT, layer)],
        out_specs=pl.BlockSpec((1, step, 256), lambda bi, i: (bi, i, 0)),
        out_shape=jax.ShapeDtypeStruct((b, s, 256), BF16),
        compiler_params=_cparams("parallel", "parallel"), name="chunk_attn")(qaT, ka, vaT, biasT)


def _bit_transpose32(words):
    a = list(words)
    mask, j = 0x0000FFFF, 16
    while j:
        k = 0
        while k < 32:
            t = (a[k] ^ (a[k + j] >> jnp.uint32(j))) & jnp.uint32(mask)
            a[k] = a[k] ^ t
            a[k + j] = a[k + j] ^ (t << jnp.uint32(j))
            k = (k + j + 1) & ~j
        j >>= 1
        mask = (mask ^ (mask << j)) & 0xFFFFFFFF
    return a


def _dsa_kernel(qiT_ref, iwT_ref, qbT_ref, ki_ref, kb_ref, vbT_ref, o_ref, scores_ref, planes_ref, *, topk):
    tk = B_KEY_TILE
    pair = pl.program_id(1)
    tiles = range(B_TILES_PER_STEP)
    n_small = pair + 1
    n_wide = lax.shift_right_logical(n_small, 2)
    has_mid, has_small = (n_small & 2) != 0, (n_small & 1) != 0
    lane = lax.broadcasted_iota(I32, (1, Q_TILE), 1)
    qis, ws, key_limits = [], [], []
    for tq in tiles:
        qiT = qiT_ref[0, :, tq * Q_TILE:(tq + 1) * Q_TILE]
        qi = jnp.concatenate([qiT[hh * IDX_DIM:(hh + 1) * IDX_DIM] for hh in range(IDX_HEADS)], axis=1)
        qis.append(jnp.concatenate([qi, jnp.zeros_like(qi)], axis=0))
        ws.append(iwT_ref[0, :, tq * Q_TILE:(tq + 1) * Q_TILE])
        i = pair * B_TILES_PER_STEP + tq
        key_limits.append(i * Q_TILE + CHUNK + jnp.where(lane >= CHUNK, CHUNK, 0))

    def over_key_tiles(span, carry):
        wide, small = 2 * tk, tk // 2
        carry = lax.fori_loop(0, n_wide, lambda u, c: span(pl.multiple_of(u * wide, wide), wide, c), carry)
        r_mid = pl.multiple_of(n_wide * wide, wide)
        carry = lax.cond(has_mid, lambda c: span(r_mid, tk, c), lambda c: c, carry)
        r_small = pl.multiple_of(r_mid + jnp.where(has_mid, tk, 0), small)
        return lax.cond(has_small, lambda c: span(r_small, small, c), lambda c: c, carry)

    def score_span(r0, rows, c):
        ki = ki_ref[0, pl.ds(r0, rows), :]
        row_iota = lax.broadcasted_iota(I32, (rows, Q_TILE), 0)
        for tq in tiles:
            dots = jnp.dot(ki, qis[tq], preferred_element_type=F32)
            sc = jnp.maximum(dots[:, 0:Q_TILE], 0.0) * ws[tq][0:1, :]
            for hh in range(1, IDX_HEADS):
                sc = sc + jnp.maximum(dots[:, hh * Q_TILE:(hh + 1) * Q_TILE], 0.0) * ws[tq][hh:hh + 1, :]
            sc = jnp.where(row_iota < key_limits[tq] - r0, sc, -jnp.inf)
            scores_ref[tq, pl.ds(r0, rows), :] = sc
            bits = lax.bitcast_convert_type(sc, U32)
            for g0 in range(0, rows, PLANE_GROUP):
                words = [bits[g0 + j * SUBLANES:g0 + (j + 1) * SUBLANES] for j in range(32)]
                planes = _bit_transpose32(words)
                magnitude = planes[1]
                for p in planes[2:]:
                    magnitude = magnitude | p
                negative = planes[0] & magnitude
                planes = [~negative] + [p ^ negative for p in planes[1:]]
                planes_ref[tq, pl.ds(r0 + g0, PLANE_GROUP), :] = jnp.concatenate(planes, axis=0)
        return c

    @pl.when(pair == 0)
    def _():
        planes_ref[...] = jnp.zeros_like(planes_ref)

    over_key_tiles(score_span, 0)

    max_groups = planes_ref.shape[1] // PLANE_GROUP
    full = jnp.full((SUBLANES, Q_TILE), 0xFFFFFFFF, U32)
    zero = jnp.zeros((1, Q_TILE), I32)

    def members(masks):
        parts = [lax.population_count(m) for m in masks]
        while len(parts) > 1:
            parts = [parts[j] + parts[j + 1] for j in range(0, len(parts) - 1, 2)] + parts[len(parts) & ~1:]
        return jnp.sum(parts[0].astype(I32), axis=0, keepdims=True)

    def search(n_groups):
        live0 = tuple(jnp.where(g < n_small, full, jnp.zeros_like(full)) for g in range(n_groups))

        def two_bits_tile(tq, row, bit_hi, state):
            live, n_gt, thr_bits = state
            p_hi = [planes_ref[tq, pl.ds(g * PLANE_GROUP + row, SUBLANES), :] for g in range(n_groups)]
            p_lo = [planes_ref[tq, pl.ds(g * PLANE_GROUP + row + SUBLANES, SUBLANES), :] for g in range(n_groups)]
            set_hi = [live[g] & p_hi[g] for g in range(n_groups)]
            clr_hi = [live[g] ^ set_hi[g] for g in range(n_groups)]
            set_both = [set_hi[g] & p_lo[g] for g in range(n_groups)]
            clr_set = [clr_hi[g] & p_lo[g] for g in range(n_groups)]
            n_hi = n_gt + members(set_hi)
            take_hi = n_hi >= topk
            n_gt = jnp.where(take_hi, n_gt, n_hi)
            n_lo = n_gt + jnp.where(take_hi, members(set_both), members(clr_set))
            take_lo = n_lo >= topk
            new_live = []
            for g in range(n_groups):
                kept = jnp.where(take_hi, set_hi[g], clr_hi[g])
                low = jnp.where(take_hi, set_both[g], clr_set[g])
                new_live.append(jnp.where(take_lo, low, kept ^ low))
            bit_lo = lax.shift_right_logical(bit_hi, 1)
            thr_bits = thr_bits | jnp.where(take_hi, bit_hi, 0) | jnp.where(take_lo, bit_lo, 0)
            return tuple(new_live), jnp.where(take_lo, n_gt, n_lo), thr_bits

        def two_bits(it, states):
            row = pl.multiple_of(it * (2 * SUBLANES), 2 * SUBLANES)
            bit_hi = jnp.int32(1) << (31 - 2 * it)
            return tuple(two_bits_tile(tq, row, bit_hi, states[tq]) for tq in tiles)

        states = lax.fori_loop(0, 16, two_bits, tuple((live0, zero, zero) for _ in tiles))
        return tuple((n_gt, thr_bits) for _, n_gt, thr_bits in states)

    if max_groups % 2 == 0:
        found = lax.cond(n_small <= max_groups // 2, lambda: search(max_groups // 2), lambda: search(max_groups))
    else:
        found = search(max_groups)
    thrs, needs = [], []
    for n_gt, thr_bits in found:
        thr = lax.bitcast_convert_type(jnp.where(thr_bits < 0, thr_bits ^ jnp.int32(INT_MIN), ~thr_bits), F32)
        thrs.append(thr)
        needs.append(jnp.where(thr == -jnp.inf, 0, topk - n_gt).astype(F32))
    hk = tk // 2
    tril = (lax.broadcasted_iota(I32, (hk, hk), 0) >= lax.broadcasted_iota(I32, (hk, hk), 1)).astype(BF16)

    qbds = [_block_diag_q(qbT_ref[0, :, tq * Q_TILE:(tq + 1) * Q_TILE]) for tq in tiles]

    def attn_span_tile(tq, r0, rows, kb, carry):
        ms, ls, accs, seen = carry
        key = scores_ref[tq, pl.ds(r0, rows), :]
        tied = key == thrs[tq]
        tied_b = jnp.where(tied, 1.0, 0.0).astype(BF16)
        ranks = []
        for c0 in range(0, rows, hk):
            ranks.append(jnp.dot(tril, tied_b[c0:c0 + hk], preferred_element_type=F32) + seen)
            seen = ranks[-1][hk - 1:hk, :]
        sel = jnp.logical_or(key > thrs[tq], jnp.logical_and(tied, jnp.concatenate(ranks, axis=0) <= needs[tq]))
        s_all = jnp.dot(kb, qbds[tq], preferred_element_type=F32)
        new_ms, new_ls, new_accs = [], [], []
        for hh in range(B_HEADS):
            s = jnp.where(sel, s_all[:, hh * Q_TILE:(hh + 1) * Q_TILE], -jnp.inf)
            m_new = jnp.maximum(ms[hh], jnp.max(s, axis=0, keepdims=True))
            alpha = jnp.exp2(ms[hh] - m_new)
            p = jnp.exp2(s - m_new)
            new_ls.append(alpha * ls[hh] + jnp.sum(p, axis=0, keepdims=True))
            pv = jnp.dot(vbT_ref[0, hh * HEAD_DIM:(hh + 1) * HEAD_DIM, pl.ds(r0, rows)], p.astype(BF16),
                         preferred_element_type=F32)
            new_accs.append(alpha * accs[hh] + pv)
            new_ms.append(m_new)
        return tuple(new_ms), tuple(new_ls), tuple(new_accs), seen

    def attn_span(r0, rows, carries):
        kb = kb_ref[0, pl.ds(r0, rows), :]
        return tuple(attn_span_tile(tq, r0, rows, kb, carries[tq]) for tq in tiles)

    init = (tuple(jnp.full((1, Q_TILE), -1e30, F32) for _ in range(B_HEADS)),
            tuple(jnp.zeros((1, Q_TILE), F32) for _ in range(B_HEADS)),
            tuple(jnp.zeros((HEAD_DIM, Q_TILE), F32) for _ in range(B_HEADS)),
            jnp.zeros((1, Q_TILE), F32))
    finals = over_key_tiles(attn_span, tuple(init for _ in tiles))
    for tq in tiles:
        _, ls, accs, _ = finals[tq]
        out = jnp.concatenate([accs[hh] / ls[hh] for hh in range(B_HEADS)], axis=0)
        o_ref[0, tq * Q_TILE:(tq + 1) * Q_TILE, :] = out.T.astype(BF16)


def _sparse_attention(qiT, iwT, qbT, ki, kb, vbT):
    b, _, s = qbT.shape
    s_pad = -(-s // B_KEY_TILE) * B_KEY_TILE
    topk = min(TOPK_MAX, s // 4)
    step = B_TILES_PER_STEP * Q_TILE
    grid = (b, s // step)
    qspec = lambda r: pl.BlockSpec((1, r, step), lambda bi, i: (bi, 0, i))
    kernel = functools.partial(_dsa_kernel, topk=topk)
    return pl.pallas_call(
        kernel, grid=grid,
        in_specs=[qspec(256), qspec(8), qspec(256),
                  pl.BlockSpec((1, s, 128), lambda bi, i: (bi, 0, 0)),
                  pl.BlockSpec((1, s, 256), lambda bi, i: (bi, 0, 0)),
                  pl.BlockSpec((1, 256, s), lambda bi, i: (bi, 0, 0))],
        out_specs=pl.BlockSpec((1, step, 256), lambda bi, i: (bi, i, 0)),
        out_shape=jax.ShapeDtypeStruct((b, s, 256), BF16),
        scratch_shapes=[pltpu.VMEM((B_TILES_PER_STEP, s_pad, Q_TILE), F32),
                        pltpu.VMEM((B_TILES_PER_STEP, s_pad, Q_TILE), U32)],
        compiler_params=_cparams("parallel", "arbitrary"), name="sparse_attn")(qiT, iwT, qbT, ki, kb, vbT)


def _rglru_kernel(cx_ref, cy_ref, cw_ref, cb_ref, wax_ref, ba_ref, bx_ref, lam_ref, o_ref,
                  tail_ref, h_ref, a_ref, g_ref):
    ts = cx_ref.shape[1]

    @pl.when(pl.program_id(1) == 0)
    def _():
        tail_ref[...] = jnp.zeros_like(tail_ref)
        h_ref[...] = jnp.zeros_like(h_ref)

    x = cx_ref[0]
    tail = tail_ref[...]
    row8 = lax.broadcasted_iota(I32, (SUBLANES, C_WIDTH), 0)
    u = cb_ref[...] + x * cw_ref[C_CONV - 1:C_CONV, :]
    for k in range(1, C_CONV):
        xr = pltpu.roll(x, k, axis=0)
        head = jnp.where(row8 < k, pltpu.roll(tail, k, axis=0), xr[0:SUBLANES])
        xk = jnp.concatenate([head, xr[SUBLANES:]], axis=0)
        u = u + xk * cw_ref[C_CONV - 1 - k:C_CONV - k, :]
    tail_ref[...] = x[ts - SUBLANES:ts]

    gates = jnp.dot(u.astype(BF16), wax_ref[...], preferred_element_type=F32)
    r = _sigmoid(gates[:, :C_WIDTH] + ba_ref[...])
    ig = _sigmoid(gates[:, C_WIDTH:] + bx_ref[...])
    nlam = -lam_ref[...]
    softplus = jnp.maximum(nlam, 0.0) + jnp.log1p(jnp.exp(-jnp.abs(nlam)))
    a = jnp.exp2(r * (softplus * (-LRU_C * LOG2_E)))
    a_ref[...] = a
    y = 1.0 - a * a
    root = jnp.where(y > 0.0, y * lax.rsqrt(y), 0.0)
    g_ref[...] = root * (ig * u)

    def group(gi, hprev):
        r0 = pl.multiple_of(gi * SUBLANES, SUBLANES)
        av = a_ref[pl.ds(r0, SUBLANES), :]
        bv = g_ref[pl.ds(r0, SUBLANES), :]
        for d in (1, 2, 4):
            keep = row8 >= d
            a_sh = jnp.where(keep, pltpu.roll(av, d, axis=0), 1.0)
            b_sh = jnp.where(keep, pltpu.roll(bv, d, axis=0), 0.0)
            bv = av * b_sh + bv
            av = av * a_sh
        hs = av * hprev + bv
        g_ref[pl.ds(r0, SUBLANES), :] = hs
        return jnp.broadcast_to(hs[SUBLANES - 1:SUBLANES, :], hs.shape)

    h_ref[...] = lax.fori_loop(0, ts // SUBLANES, group, h_ref[...], unroll=8)
    o_ref[0] = (g_ref[...] * jax.nn.gelu(cy_ref[0])).astype(BF16)


def _rglru(layer, cx, cy, cw, cb, wax, ba, bx, lam, ts):
    b, s, c = cx.shape
    blk = pl.BlockSpec((1, ts, c), lambda bi, si: (bi, si, 0))
    return pl.pallas_call(
        _rglru_kernel, grid=(b, s // ts),
        in_specs=[blk, blk] + [_layer_spec(a, layer) for a in (cw, cb, wax, ba, bx, lam)],
        out_specs=blk, out_shape=jax.ShapeDtypeStruct((b, s, c), BF16),
        scratch_shapes=[pltpu.VMEM((SUBLANES, c), F32), pltpu.VMEM((SUBLANES, c), F32),
                        pltpu.VMEM((ts, c), F32), pltpu.VMEM((ts, c), F32)],
        compiler_params=_cparams("parallel", "arbitrary"), name="rglru")(cx, cy, cw, cb, wax, ba, bx, lam)


def _merge_kernel(x_ref, ya_ref, yb_ref, yc_ref, g_ref, wg_ref, bg_ref, wb_ref, wo_ref, o_ref):
    x = x_ref[...]
    d = x.shape[1]
    h = _rms(x, g_ref[...]).astype(BF16)
    ys = (ya_ref[...], yb_ref[...], yc_ref[...])
    offs = (0, A_WIDTH, A_WIDTH + B_WIDTH, A_WIDTH + B_WIDTH + C_WIDTH)
    merged = None
    for j in range(N_BRANCH):
        gate = _sigmoid(jnp.dot(h, wg_ref[:, j * d:(j + 1) * d], preferred_element_type=F32)
                              + bg_ref[:, j * d:(j + 1) * d])
        term = gate * jnp.dot(ys[j], wb_ref[offs[j]:offs[j + 1], :], preferred_element_type=F32)
        merged = term if merged is None else merged + term
    o_ref[...] = x + jnp.dot(merged.astype(BF16), wo_ref[...], preferred_element_type=F32)


def _merge(layer, x2, ya, yb, yc, g, wg, bg, wb, wo, tm):
    n, d = x2.shape
    row = lambda w: pl.BlockSpec((tm, w), lambda i: (i, 0))
    return pl.pallas_call(
        _merge_kernel, grid=(n // tm,),
        in_specs=[row(d), row(A_WIDTH), row(B_WIDTH), row(C_WIDTH)]
        + [_layer_spec(a, layer) for a in (g, wg, bg, wb, wo)],
        out_specs=row(d), out_shape=jax.ShapeDtypeStruct((n, d), F32),
        compiler_params=_cparams("parallel"), name="merge")(x2, ya, yb, yc, g, wg, bg, wb, wo)


def _ffn_kernel(x_ref, g_ref, wi_ref, wo_ref, o_ref, *, d_ff, chunk):
    x = x_ref[...]
    h = _rms(x, g_ref[...]).astype(BF16)
    acc = x
    for c0 in range(0, d_ff, chunk):
        c1 = min(c0 + chunk, d_ff)
        gp = jnp.dot(h, wi_ref[:, c0:c1], preferred_element_type=F32)
        up = jnp.dot(h, wi_ref[:, d_ff + c0:d_ff + c1], preferred_element_type=F32)
        act = (gp * _sigmoid(gp) * up).astype(BF16)
        acc = acc + jnp.dot(act, wo_ref[c0:c1, :], preferred_element_type=F32)
    o_ref[...] = acc


def _ffn(layer, x2, g, wi, wo, tm):
    n, d = x2.shape
    d_ff = wo.shape[1]
    row = pl.BlockSpec((tm, d), lambda i: (i, 0))
    kernel = functools.partial(_ffn_kernel, d_ff=d_ff, chunk=1024)
    return pl.pallas_call(
        kernel, grid=(n // tm,),
        in_specs=[row] + [_layer_spec(a, layer) for a in (g, wi, wo)],
        out_specs=row, out_shape=jax.ShapeDtypeStruct((n, d), F32),
        compiler_params=_cparams("parallel"), name="ffn")(x2, g, wi, wo)


def _rope_tables(s):
    pos = jnp.arange(s, dtype=F32)
    inv = 1.0 / (ROPE_THETA ** (jnp.arange(0, HEAD_DIM, 2, dtype=F32) / HEAD_DIM))
    ang = pos[:, None] * inv[None, :]
    cos, sin = jnp.cos(ang), jnp.sin(ang)
    return jnp.concatenate([cos, cos], axis=-1), jnp.concatenate([sin, sin], axis=-1)


def _bias_table_kernel(vec_ref, o_ref):
    nk, q_tile = o_ref.shape
    rolled = pltpu.roll(jnp.broadcast_to(vec_ref[...], (nk, nk + q_tile)), 0, axis=1, stride=1, stride_axis=0)
    j = lax.broadcasted_iota(I32, (nk, q_tile), 0)
    q = lax.broadcasted_iota(I32, (nk, q_tile), 1)
    shift = CHUNK.bit_length() - 1
    dchunk = A_LEFT_CHUNKS + (q >> shift) - (j >> shift)
    valid = jnp.logical_and(dchunk >= 0, dchunk <= A_LEFT_CHUNKS)
    o_ref[...] = jnp.where(valid, rolled[:, nk:] * LOG2_E, -jnp.inf)


def _rel_bias_tables(rel_bias):
    depth = rel_bias.shape[0]
    nk = A_KEY_TILES * Q_TILE
    assert REL_CLIP == Q_TILE and CHUNK & (CHUNK - 1) == 0
    rb = rel_bias.astype(F32)
    vec = jnp.concatenate([rb, jnp.tile(rb[..., -1:], (1, 1, nk + Q_TILE - rb.shape[-1]))], axis=-1)
    return pl.pallas_call(
        _bias_table_kernel, grid=(depth, A_HEADS),
        in_specs=[pl.BlockSpec((None, None, 1, nk + Q_TILE), lambda l, h: (l, h, 0, 0))],
        out_specs=pl.BlockSpec((None, nk, Q_TILE), lambda l, h: (l, 0, h)),
        out_shape=jax.ShapeDtypeStruct((depth, nk, A_HEADS * Q_TILE), F32),
        compiler_params=_cparams("parallel", "parallel"), name="bias_table")(vec[:, :, None, :])


def kernel(x, g_mix, w_in, qk_gain_a, rel_bias, qk_gain_b, g_idx_k, conv_w, conv_b, lru_wa, lru_ba, lru_wx,
           lru_bx, lru_lambda, b_gate, w_branch, w_out, g_ffn, w_ffn_in, w_ffn_out):
    b, s, d = x.shape
    depth = g_mix.shape[0]
    assert s % B_KEY_TILE == 0 and d % LANES == 0
    tm = 512
    cos, sin = _rope_tables(s)
    half = (jnp.arange(HEAD_DIM) < HEAD_DIM // 2).astype(F32)
    cos_rm = jnp.tile(cos, (1, 4))
    sina_rm = jnp.tile(-sin * half[None, :], (1, 4))
    sinb_rm = jnp.tile(sin * (1.0 - half)[None, :], (1, 4))
    cosT, sinT = cos.T, sin.T
    hid = jnp.arange(256) // HEAD_DIM
    bd = (hid[:, None] == hid[None, :]).astype(BF16)
    rows = lambda v: v.reshape(depth, 1, -1).astype(F32)
    blocks = jnp.arange(C_WIDTH) // (C_WIDTH // C_BLOCKS)
    bmask = (blocks[:, None] == blocks[None, :])

    def dense_blocks(wblk):
        return jnp.where(bmask, jnp.tile(wblk.reshape(depth, C_WIDTH, C_WIDTH // C_BLOCKS), (1, 1, C_BLOCKS)), 0.0)

    wrm, w_gate, wt = _weight_prep(w_in)
    gka, gkb = rows(jnp.tile(qk_gain_a[:, 1], (1, 4))), rows(jnp.tile(qk_gain_b[:, 1], (1, 4)))
    gki = rows(jnp.concatenate([g_idx_k, jnp.zeros((depth, 64), F32)], axis=1))
    gqaT = jnp.tile(qk_gain_a[:, 0], (1, 4)).reshape(depth, 256, 1)
    gqbT = jnp.tile(qk_gain_b[:, 0], (1, 4)).reshape(depth, 256, 1)
    bias_tabs = _rel_bias_tables(rel_bias)
    wax = jnp.concatenate([dense_blocks(lru_wa), dense_blocks(lru_wx)], axis=2).astype(BF16)
    g_mix3, g_ffn3, b_gate3 = rows(g_mix), rows(g_ffn), rows(b_gate)
    cb3, ba3, bx3, lam3 = rows(conv_b), rows(lru_ba), rows(lru_bx), rows(lru_lambda)
    wb, wo = w_branch.astype(BF16), w_out.astype(BF16)
    wfi, wfo = w_ffn_in.astype(BF16), w_ffn_out.astype(BF16)

    n = b * s
    for l in range(depth):
        ka, kb, ki, cx, cy, qaT, vaT, qbT, vbT, qiT, iwT = _project(
            l, x, g_mix3, wrm, wt, bd, gka, gkb, gki, gqaT, gqbT, cos_rm, sina_rm, sinb_rm, cosT, sinT, tm)
        ya = _chunk_attention(l, qaT, ka, vaT, bias_tabs)
        yb = _sparse_attention(qiT, iwT, qbT, ki, kb, vbT)
        yc = _rglru(l, cx, cy, conv_w, cb3, wax, ba3, bx3, lam3, 512)
        x1 = _merge(l, x.reshape(n, d), ya.reshape(n, A_WIDTH), yb.reshape(n, B_WIDTH), yc.reshape(n, C_WIDTH),
                    g_mix3, w_gate, b_gate3, wb, wo, tm)
        x = _ffn(l, x1, g_ffn3, wfi, wfo, tm).reshape(b, s, d)
    return x
```

```python
import functools
import math

import jax
import jax.numpy as jnp
from jax import lax
from jax.experimental import pallas as pl
from jax.experimental.pallas import tpu as pltpu

F32 = jnp.float32
BF16 = jnp.bfloat16
I32 = jnp.int32
U32 = jnp.uint32

CHUNK = 64
HEAD_DIM = 64
EPS = 1e-6
ROPE_THETA = 10000.0
A_HEADS = 4
A_LEFT_CHUNKS = 8
REL_CLIP = 128
B_HEADS = 4
IDX_HEADS = 4
IDX_DIM = 64
TOPK_MAX = 256
C_WIDTH = 512
C_BLOCKS = 8
C_CONV = 4
LRU_C = 8.0
N_BRANCH = 3
A_WIDTH = A_HEADS * HEAD_DIM
B_WIDTH = B_HEADS * HEAD_DIM

LANES = 128
SUBLANES = 8
Q_TILE = 128
A_KEY_TILES = A_LEFT_CHUNKS * CHUNK // Q_TILE + 1
PROJ_ROWS = 1024
DENSE_ROWS = 512
A_TILES_PER_STEP = 4
B_TILES_PER_STEP = 2
B_KEY_TILE = 512
PLANE_GROUP = 32 * SUBLANES
LOG2_E = math.log2(math.e)
INT_MIN = -2 ** 31
VMEM_LIMIT = 56 * 1024 * 1024

NT_DIMS = (((1,), (1,)), ((), ()))


def _cparams(*sem):
    return pltpu.CompilerParams(dimension_semantics=sem, vmem_limit_bytes=VMEM_LIMIT)


def _const_spec(shape):
    nd = len(shape)
    return pl.BlockSpec(shape, lambda *_: (0,) * nd, pipeline_mode=pl.Buffered(1))


def _layer_spec(stacked, layer):
    nd = stacked.ndim - 1
    return pl.BlockSpec((None,) + stacked.shape[1:], lambda *_: (layer,) + (0,) * nd, pipeline_mode=pl.Buffered(1))


def _sigmoid(x):
    return 0.5 * jnp.tanh(0.5 * x) + 0.5


def _rms(x, gain):
    ms = jnp.mean(x * x, axis=-1, keepdims=True)
    return x * lax.rsqrt(ms + EPS) * gain


def _proj_kernel(x_ref, g_ref, wrm_ref, wt_ref, gkaT_ref, gkbT_ref, gki_ref, gqaT_ref, gqbT_ref,
                 cos_ref, sina_ref, sinb_ref, cosT_ref, sinT_ref,
                 ka_ref, kb_ref, ki_ref, cx_ref, cy_ref,
                 qaT_ref, vaT_ref, qbT_ref, vbT_ref, qiT_ref, iwT_ref):
    tm = x_ref.shape[1]
    h = _rms(x_ref[0], g_ref[...]).astype(BF16)

    def rm(c0, c1):
        return jnp.dot(h, wrm_ref[:, c0:c1], preferred_element_type=F32)

    def rope_rm(z):
        zm = pltpu.roll(z, LANES - HEAD_DIM // 2, axis=1)
        zp = pltpu.roll(z, HEAD_DIM // 2, axis=1)
        return z * cos_ref[...] + zm * sina_ref[...] + zp * sinb_ref[...]

    zi = rm(0, LANES)
    ms = jnp.sum(zi * zi, axis=-1, keepdims=True) * (1.0 / IDX_DIM)
    ki_ref[0] = rope_rm(zi * lax.rsqrt(ms + EPS) * gki_ref[...]).astype(BF16)
    cx_ref[0] = rm(LANES, LANES + C_WIDTH)
    cy_ref[0] = rm(LANES + C_WIDTH, LANES + 2 * C_WIDTH)

    def tr(r0, r1):
        return lax.dot_general(wt_ref[r0:r1, :], h, NT_DIMS, preferred_element_type=F32)

    def head_rms_t(z, gain):
        z4 = z.reshape(4, HEAD_DIM, tm)
        ms4 = jnp.mean(z4 * z4, axis=1, keepdims=True)
        return (z4 * lax.rsqrt(ms4 + EPS)).reshape(4 * HEAD_DIM, tm) * gain

    def rope_t(z):
        z4 = z.reshape(4, HEAD_DIM, tm)
        rot = jnp.concatenate([-z4[:, HEAD_DIM // 2:, :], z4[:, :HEAD_DIM // 2, :]], axis=1)
        return (z4 * cosT_ref[...][None] + rot * sinT_ref[...][None]).reshape(4 * HEAD_DIM, tm)

    scale = HEAD_DIM ** -0.5 * LOG2_E
    qaT_ref[0] = (head_rms_t(tr(0, 256), gqaT_ref[...]) * scale).astype(BF16)
    vaT_ref[0] = tr(256, 512).astype(BF16)
    qbT_ref[0] = (rope_t(head_rms_t(tr(512, 768), gqbT_ref[...])) * scale).astype(BF16)
    vbT_ref[0] = tr(768, 1024).astype(BF16)
    qiT_ref[0] = rope_t(tr(1024, 1280)).astype(BF16)
    ka_ref[0] = head_rms_t(tr(1280, 1536), gkaT_ref[...]).T.astype(BF16)
    kb_ref[0] = rope_t(head_rms_t(tr(1536, 1792), gkbT_ref[...])).T.astype(BF16)
    iwT_ref[0] = tr(1792, 1808)[0:SUBLANES] * (IDX_HEADS ** -0.5 * IDX_DIM ** -0.5)


def _project(layer, x, g, wrm, wt, gkaT, gkbT, gki, gqaT, gqbT, cos, sina, sinb, cosT, sinT, tm):
    b, s, d = x.shape
    grid = (b, s // tm)
    row = lambda w: pl.BlockSpec((1, tm, w), lambda bi, si: (bi, si, 0))
    col = lambda r: pl.BlockSpec((1, r, tm), lambda bi, si: (bi, 0, si))
    tab_rm = pl.BlockSpec((tm, LANES), lambda bi, si: (si, 0))
    tab_t = pl.BlockSpec((HEAD_DIM, tm), lambda bi, si: (0, si))
    per_layer = lambda a: _layer_spec(a, layer)
    in_specs = [row(d), per_layer(g), per_layer(wrm), per_layer(wt),
                per_layer(gkaT), per_layer(gkbT), per_layer(gki), per_layer(gqaT), per_layer(gqbT),
                tab_rm, tab_rm, tab_rm, tab_t, tab_t]
    out_shape = (jax.ShapeDtypeStruct((b, s, 256), BF16), jax.ShapeDtypeStruct((b, s, 256), BF16),
                 jax.ShapeDtypeStruct((b, s, 128), BF16),
                 jax.ShapeDtypeStruct((b, s, C_WIDTH), F32), jax.ShapeDtypeStruct((b, s, C_WIDTH), F32),
                 jax.ShapeDtypeStruct((b, 256, s), BF16), jax.ShapeDtypeStruct((b, 256, s), BF16),
                 jax.ShapeDtypeStruct((b, 256, s), BF16), jax.ShapeDtypeStruct((b, 256, s), BF16),
                 jax.ShapeDtypeStruct((b, 256, s), BF16), jax.ShapeDtypeStruct((b, 8, s), F32))
    out_specs = (row(256), row(256), row(128), row(C_WIDTH), row(C_WIDTH),
                 col(256), col(256), col(256), col(256), col(256), col(8))
    return pl.pallas_call(_proj_kernel, grid=grid, in_specs=in_specs, out_specs=out_specs, out_shape=out_shape,
                          compiler_params=_cparams("parallel", "parallel"), name="proj")(
        x, g, wrm, wt, gkaT, gkbT, gki, gqaT, gqbT, cos, sina, sinb, cosT, sinT)


O_AQ, O_AK, O_AV, O_BQ, O_BK, O_BV = 0, 256, 512, 768, 1024, 1280
O_IQ, O_IK, O_IW, O_CX, O_CY, O_GT = 1536, 1792, 1856, 1860, 2372, 2884
W_ROWMAJOR_COLS = LANES + 2 * C_WIDTH
W_TRANSPOSED_ROWS = 7 * 256 + LANES
W_PREP_ROWS = 256


def _weight_prep_kernel(w_ref, wrm_ref, wg_ref, wt_ref):
    rows = w_ref.shape[0]
    cols = lambda o, n: w_ref[:, o:o + n]
    wrm_ref[...] = jnp.concatenate(
        [cols(O_IK, IDX_DIM), jnp.zeros((rows, LANES - IDX_DIM), F32), cols(O_CX, C_WIDTH), cols(O_CY, C_WIDTH)],
        axis=1).astype(BF16)
    wg_ref[...] = w_ref[:, O_GT:].astype(BF16)
    wt = jnp.concatenate([cols(O_AQ, 256), cols(O_AV, 256), cols(O_BQ, 256), cols(O_BV, 256), cols(O_IQ, 256),
                          cols(O_AK, 256), cols(O_BK, 256),
                          cols(O_IW, IDX_HEADS), jnp.zeros((rows, LANES - IDX_HEADS), F32)], axis=1)
    wt_ref[...] = wt.T.astype(BF16)


def _weight_prep(w_in):
    depth, d, n = w_in.shape
    n_gate = n - O_GT
    rb = W_PREP_ROWS
    return pl.pallas_call(
        _weight_prep_kernel, grid=(depth, d // rb),
        in_specs=[pl.BlockSpec((None, rb, n), lambda l, i: (l, i, 0))],
        out_specs=(pl.BlockSpec((None, rb, W_ROWMAJOR_COLS), lambda l, i: (l, i, 0)),
                   pl.BlockSpec((None, rb, n_gate), lambda l, i: (l, i, 0)),
                   pl.BlockSpec((None, W_TRANSPOSED_ROWS, rb), lambda l, i: (l, 0, i))),
        out_shape=(jax.ShapeDtypeStruct((depth, d, W_ROWMAJOR_COLS), BF16),
                   jax.ShapeDtypeStruct((depth, d, n_gate), BF16),
                   jax.ShapeDtypeStruct((depth, W_TRANSPOSED_ROWS, d), BF16)),
        compiler_params=_cparams("parallel", "parallel"), name="weight_prep")(w_in)


def _block_diag_q(qT):
    rows = lax.broadcasted_iota(I32, qT.shape, 0) // HEAD_DIM
    zero = jnp.zeros_like(qT)
    return jnp.concatenate([jnp.where(rows == hh, qT, zero) for hh in range(4)], axis=1)


def _chunk_attn_kernel(qT_ref, k_ref, vT_ref, bias_ref, o_ref):
    for sub in range(A_TILES_PER_STEP):
        _chunk_attn_tile(pl.program_id(1) * A_TILES_PER_STEP + sub,
                         qT_ref.at[0, :, sub * Q_TILE:(sub + 1) * Q_TILE], k_ref, vT_ref, bias_ref,
                         o_ref.at[0, sub * Q_TILE:(sub + 1) * Q_TILE, :])


def _chunk_attn_tile(i, qT_ref, k_ref, vT_ref, bias_ref, o_ref):
    qbd = _block_diag_q(qT_ref[...])
    ks, vs, pens = [], [], []
    for t in range(A_KEY_TILES):
        kt = i - (A_KEY_TILES - 1) + t
        r0 = pl.multiple_of(jnp.maximum(kt, 0) * Q_TILE, Q_TILE)
        ks.append(k_ref[0, pl.ds(r0, Q_TILE), :])
        vs.append(vT_ref[0, :, pl.ds(r0, Q_TILE)])
        pens.append(jnp.where(kt >= 0, 0.0, -jnp.inf).astype(F32))
    kwin = jnp.concatenate(ks, axis=0)
    vwin = jnp.concatenate(vs, axis=1)
    s = jnp.dot(kwin, qbd, preferred_element_type=F32) + bias_ref[...]
    s = jnp.concatenate([s[t * Q_TILE:(t + 1) * Q_TILE] + pens[t] for t in range(A_KEY_TILES)], axis=0)
    m = jnp.max(s, axis=0, keepdims=True)
    p = jnp.exp2(s - m)
    l = jnp.sum(p, axis=0, keepdims=True)
    pb = p.astype(BF16)
    outs = []
    for hh in range(A_HEADS):
        lo, hi = hh * Q_TILE, (hh + 1) * Q_TILE
        o = jnp.dot(vwin[hh * HEAD_DIM:(hh + 1) * HEAD_DIM, :], pb[:, lo:hi], preferred_element_type=F32)
        outs.append(o / l[:, lo:hi])
    o_ref[...] = jnp.concatenate(outs, axis=0).T.astype(BF16)


def _chunk_attention(layer, qaT, ka, vaT, biasT):
    b, _, s = qaT.shape
    step = A_TILES_PER_STEP * Q_TILE
    grid = (b, s // step)
    return pl.pallas_call(
        _chunk_attn_kernel, grid=grid,
        in_specs=[pl.BlockSpec((1, 256, step), lambda bi, i: (bi, 0, i)),
                  pl.BlockSpec((1, s, 256), lambda bi, i: (bi, 0, 0)),
                  pl.BlockSpec((1, 256, s), lambda bi, i: (bi, 0, 0)),
                  _layer_spec(biasT, layer)],
        out_specs=pl.BlockSpec((1, step, 256), lambda bi, i: (bi, i, 0)),
        out_shape=jax.ShapeDtypeStruct((b, s, 256), BF16),
        compiler_params=_cparams("parallel", "parallel"), name="chunk_attn")(qaT, ka, vaT, biasT)


def _bit_transpose32(words):
    a = list(words)
    mask, j = 0x0000FFFF, 16
    while j:
        k = 0
        while k < 32:
            t = (a[k] ^ (a[k + j] >> jnp.uint32(j))) & jnp.uint32(mask)
            a[k] = a[k] ^ t
            a[k + j] = a[k + j] ^ (t << jnp.uint32(j))
            k = (k + j + 1) & ~j
        j >>= 1
        mask = (mask ^ (mask << j)) & 0xFFFFFFFF
    return a


def _dsa_kernel(qiT_ref, iwT_ref, qbT_ref, ki_ref, kb_ref, vbT_ref, o_ref, scores_ref, planes_ref, *, topk):
    tk = B_KEY_TILE
    pair = pl.program_id(1)
    tiles = range(B_TILES_PER_STEP)
    n_small = pair + 1
    n_wide = lax.shift_right_logical(n_small, 2)
    has_mid, has_small = (n_small & 2) != 0, (n_small & 1) != 0
    lane = lax.broadcasted_iota(I32, (1, Q_TILE), 1)
    qis, ws, key_limits = [], [], []
    for tq in tiles:
        qiT = qiT_ref[0, :, tq * Q_TILE:(tq + 1) * Q_TILE]
        qi = jnp.concatenate([qiT[hh * IDX_DIM:(hh + 1) * IDX_DIM] for hh in range(IDX_HEADS)], axis=1)
        qis.append(jnp.concatenate([qi, jnp.zeros_like(qi)], axis=0))
        ws.append(iwT_ref[0, :, tq * Q_TILE:(tq + 1) * Q_TILE])
        i = pair * B_TILES_PER_STEP + tq
        key_limits.append(i * Q_TILE + CHUNK + jnp.where(lane >= CHUNK, CHUNK, 0))

    def over_key_tiles(span, carry):
        wide, small = 2 * tk, tk // 2
        carry = lax.fori_loop(0, n_wide, lambda u, c: span(pl.multiple_of(u * wide, wide), wide, c), carry)
        r_mid = pl.multiple_of(n_wide * wide, wide)
        carry = lax.cond(has_mid, lambda c: span(r_mid, tk, c), lambda c: c, carry)
        r_small = pl.multiple_of(r_mid + jnp.where(has_mid, tk, 0), small)
        return lax.cond(has_small, lambda c: span(r_small, small, c), lambda c: c, carry)

    def score_span(r0, rows, c):
        ki = ki_ref[0, pl.ds(r0, rows), :]
        row_iota = lax.broadcasted_iota(I32, (rows, Q_TILE), 0)
        for tq in tiles:
            dots = jnp.dot(ki, qis[tq], preferred_element_type=F32)
            sc = jnp.maximum(dots[:, 0:Q_TILE], 0.0) * ws[tq][0:1, :]
            for hh in range(1, IDX_HEADS):
                sc = sc + jnp.maximum(dots[:, hh * Q_TILE:(hh + 1) * Q_TILE], 0.0) * ws[tq][hh:hh + 1, :]
            sc = jnp.where(row_iota < key_limits[tq] - r0, sc, -jnp.inf)
            scores_ref[tq, pl.ds(r0, rows), :] = sc
            bits = lax.bitcast_convert_type(sc, U32)
            for g0 in range(0, rows, PLANE_GROUP):
                words = [bits[g0 + j * SUBLANES:g0 + (j + 1) * SUBLANES] for j in range(32)]
                planes = _bit_transpose32(words)
                magnitude = planes[1]
                for p in planes[2:]:
                    magnitude = magnitude | p
                negative = planes[0] & magnitude
                planes = [~negative] + [p ^ negative for p in planes[1:]]
                planes_ref[tq, pl.ds(r0 + g0, PLANE_GROUP), :] = jnp.concatenate(planes, axis=0)
        return c

    @pl.when(pair == 0)
    def _():
        planes_ref[...] = jnp.zeros_like(planes_ref)

    over_key_tiles(score_span, 0)

    max_groups = planes_ref.shape[1] // PLANE_GROUP
    full = jnp.full((SUBLANES, Q_TILE), 0xFFFFFFFF, U32)
    zero = jnp.zeros((1, Q_TILE), I32)

    def members(masks):
        parts = [lax.population_count(m) for m in masks]
        while len(parts) > 1:
            parts = [parts[j] + parts[j + 1] for j in range(0, len(parts) - 1, 2)] + parts[len(parts) & ~1:]
        return jnp.sum(parts[0].astype(I32), axis=0, keepdims=True)

    def search(n_groups):
        live0 = tuple(jnp.where(g < n_small, full, jnp.zeros_like(full)) for g in range(n_groups))

        def two_bits_tile(tq, row, bit_hi, state):
            live, n_gt, thr_bits = state
            p_hi = [planes_ref[tq, pl.ds(g * PLANE_GROUP + row, SUBLANES), :] for g in range(n_groups)]
            p_lo = [planes_ref[tq, pl.ds(g * PLANE_GROUP + row + SUBLANES, SUBLANES), :] for g in range(n_groups)]
            set_hi = [live[g] & p_hi[g] for g in range(n_groups)]
            clr_hi = [live[g] ^ set_hi[g] for g in range(n_groups)]
            set_both = [set_hi[g] & p_lo[g] for g in range(n_groups)]
            clr_set = [clr_hi[g] & p_lo[g] for g in range(n_groups)]
            n_hi = n_gt + members(set_hi)
            take_hi = n_hi >= topk
            n_gt = jnp.where(take_hi, n_gt, n_hi)
            n_lo = n_gt + jnp.where(take_hi, members(set_both), members(clr_set))
            take_lo = n_lo >= topk
            new_live = []
            for g in range(n_groups):
                kept = jnp.where(take_hi, set_hi[g], clr_hi[g])
                low = jnp.where(take_hi, set_both[g], clr_set[g])
                new_live.append(jnp.where(take_lo, low, kept ^ low))
            bit_lo = lax.shift_right_logical(bit_hi, 1)
            thr_bits = thr_bits | jnp.where(take_hi, bit_hi, 0) | jnp.where(take_lo, bit_lo, 0)
            return tuple(new_live), jnp.where(take_lo, n_gt, n_lo), thr_bits

        def two_bits(it, states):
            row = pl.multiple_of(it * (2 * SUBLANES), 2 * SUBLANES)
            bit_hi = jnp.int32(1) << (31 - 2 * it)
            return tuple(two_bits_tile(tq, row, bit_hi, states[tq]) for tq in tiles)

        states = lax.fori_loop(0, 16, two_bits, tuple((live0, zero, zero) for _ in tiles))
        return tuple((n_gt, thr_bits) for _, n_gt, thr_bits in states)

    if max_groups % 2 == 0:
        found = lax.cond(n_small <= max_groups // 2, lambda: search(max_groups // 2), lambda: search(max_groups))
    else:
        found = search(max_groups)
    thrs, needs = [], []
    for n_gt, thr_bits in found:
        thr = lax.bitcast_convert_type(jnp.where(thr_bits < 0, thr_bits ^ jnp.int32(INT_MIN), ~thr_bits), F32)
        thrs.append(thr)
        needs.append(jnp.where(thr == -jnp.inf, 0, topk - n_gt).astype(F32))
    hk = tk // 2
    tril = (lax.broadcasted_iota(I32, (hk, hk), 0) >= lax.broadcasted_iota(I32, (hk, hk), 1)).astype(BF16)

    qbds = [_block_diag_q(qbT_ref[0, :, tq * Q_TILE:(tq + 1) * Q_TILE]) for tq in tiles]

    def attn_span_tile(tq, r0, rows, kb, carry):
        ms, ls, accs, seen = carry
        key = scores_ref[tq, pl.ds(r0, rows), :]
        tied = key == thrs[tq]
        tied_b = jnp.where(tied, 1.0, 0.0).astype(BF16)
        ranks = []
        for c0 in range(0, rows, hk):
            ranks.append(jnp.dot(tril, tied_b[c0:c0 + hk], preferred_element_type=F32) + seen)
            seen = ranks[-1][hk - 1:hk, :]
        sel = jnp.logical_or(key > thrs[tq], jnp.logical_and(tied, jnp.concatenate(ranks, axis=0) <= needs[tq]))
        s_all = jnp.dot(kb, qbds[tq], preferred_element_type=F32)
        new_ms, new_ls, new_accs = [], [], []
        for hh in range(B_HEADS):
            s = jnp.where(sel, s_all[:, hh * Q_TILE:(hh + 1) * Q_TILE], -jnp.inf)
            m_new = jnp.maximum(ms[hh], jnp.max(s, axis=0, keepdims=True))
            alpha = jnp.exp2(ms[hh] - m_new)
            p = jnp.exp2(s - m_new)
            new_ls.append(alpha * ls[hh] + jnp.sum(p, axis=0, keepdims=True))
            pv = jnp.dot(vbT_ref[0, hh * HEAD_DIM:(hh + 1) * HEAD_DIM, pl.ds(r0, rows)], p.astype(BF16),
                         preferred_element_type=F32)
            new_accs.append(alpha * accs[hh] + pv)
            new_ms.append(m_new)
        return tuple(new_ms), tuple(new_ls), tuple(new_accs), seen

    def attn_span(r0, rows, carries):
        kb = kb_ref[0, pl.ds(r0, rows), :]
        return tuple(attn_span_tile(tq, r0, rows, kb, carries[tq]) for tq in tiles)

    init = (tuple(jnp.full((1, Q_TILE), -1e30, F32) for _ in range(B_HEADS)),
            tuple(jnp.zeros((1, Q_TILE), F32) for _ in range(B_HEADS)),
            tuple(jnp.zeros((HEAD_DIM, Q_TILE), F32) for _ in range(B_HEADS)),
            jnp.zeros((1, Q_TILE), F32))
    finals = over_key_tiles(attn_span, tuple(init for _ in tiles))
    for tq in tiles:
        _, ls, accs, _ = finals[tq]
        out = jnp.concatenate([accs[hh] / ls[hh] for hh in range(B_HEADS)], axis=0)
        o_ref[0, tq * Q_TILE:(tq + 1) * Q_TILE, :] = out.T.astype(BF16)


def _sparse_attention(qiT, iwT, qbT, ki, kb, vbT):
    b, _, s = qbT.shape
    s_pad = -(-s // B_KEY_TILE) * B_KEY_TILE
    topk = min(TOPK_MAX, s // 4)
    step = B_TILES_PER_STEP * Q_TILE
    grid = (b, s // step)
    qspec = lambda r: pl.BlockSpec((1, r, step), lambda bi, i: (bi, 0, i))
    kernel = functools.partial(_dsa_kernel, topk=topk)
    return pl.pallas_call(
        kernel, grid=grid,
        in_specs=[qspec(256), qspec(8), qspec(256),
                  pl.BlockSpec((1, s, 128), lambda bi, i: (bi, 0, 0)),
                  pl.BlockSpec((1, s, 256), lambda bi, i: (bi, 0, 0)),
                  pl.BlockSpec((1, 256, s), lambda bi, i: (bi, 0, 0))],
        out_specs=pl.BlockSpec((1, step, 256), lambda bi, i: (bi, i, 0)),
        out_shape=jax.ShapeDtypeStruct((b, s, 256), BF16),
        scratch_shapes=[pltpu.VMEM((B_TILES_PER_STEP, s_pad, Q_TILE), F32),
                        pltpu.VMEM((B_TILES_PER_STEP, s_pad, Q_TILE), U32)],
        compiler_params=_cparams("parallel", "arbitrary"), name="sparse_attn")(qiT, iwT, qbT, ki, kb, vbT)


def _rglru_kernel(cx_ref, cy_ref, cw_ref, cb_ref, wax_ref, ba_ref, bx_ref, lam_ref, o_ref,
                  tail_ref, h_ref, a_ref, g_ref):
    ts = cx_ref.shape[1]

    @pl.when(pl.program_id(1) == 0)
    def _():
        tail_ref[...] = jnp.zeros_like(tail_ref)
        h_ref[...] = jnp.zeros_like(h_ref)

    x = cx_ref[0]
    tail = tail_ref[...]
    row8 = lax.broadcasted_iota(I32, (SUBLANES, C_WIDTH), 0)
    u = cb_ref[...] + x * cw_ref[C_CONV - 1:C_CONV, :]
    for k in range(1, C_CONV):
        xr = pltpu.roll(x, k, axis=0)
        head = jnp.where(row8 < k, pltpu.roll(tail, k, axis=0), xr[0:SUBLANES])
        xk = jnp.concatenate([head, xr[SUBLANES:]], axis=0)
        u = u + xk * cw_ref[C_CONV - 1 - k:C_CONV - k, :]
    tail_ref[...] = x[ts - SUBLANES:ts]

    gates = jnp.dot(u.astype(BF16), wax_ref[...], preferred_element_type=F32)
    r = _sigmoid(gates[:, :C_WIDTH] + ba_ref[...])
    ig = _sigmoid(gates[:, C_WIDTH:] + bx_ref[...])
    nlam = -lam_ref[...]
    softplus = jnp.maximum(nlam, 0.0) + jnp.log1p(jnp.exp(-jnp.abs(nlam)))
    a = jnp.exp2(r * (softplus * (-LRU_C * LOG2_E)))
    a_ref[...] = a
    y = 1.0 - a * a
    root = jnp.where(y > 0.0, y * lax.rsqrt(y), 0.0)
    g_ref[...] = root * (ig * u)

    def group(gi, hprev):
        r0 = pl.multiple_of(gi * SUBLANES, SUBLANES)
        av = a_ref[pl.ds(r0, SUBLANES), :]
        bv = g_ref[pl.ds(r0, SUBLANES), :]
        for d in (1, 2, 4):
            keep = row8 >= d
            a_sh = jnp.where(keep, pltpu.roll(av, d, axis=0), 1.0)
            b_sh = jnp.where(keep, pltpu.roll(bv, d, axis=0), 0.0)
            bv = av * b_sh + bv
            av = av * a_sh
        hs = av * hprev + bv
        g_ref[pl.ds(r0, SUBLANES), :] = hs
        return jnp.broadcast_to(hs[SUBLANES - 1:SUBLANES, :], hs.shape)

    h_ref[...] = lax.fori_loop(0, ts // SUBLANES, group, h_ref[...], unroll=8)
    o_ref[0] = (g_ref[...] * jax.nn.gelu(cy_ref[0])).astype(BF16)


def _rglru(layer, cx, cy, cw, cb, wax, ba, bx, lam, ts):
    b, s, c = cx.shape
    blk = pl.BlockSpec((1, ts, c), lambda bi, si: (bi, si, 0))
    return pl.pallas_call(
        _rglru_kernel, grid=(b, s // ts),
        in_specs=[blk, blk] + [_layer_spec(a, layer) for a in (cw, cb, wax, ba, bx, lam)],
        out_specs=blk, out_shape=jax.ShapeDtypeStruct((b, s, c), BF16),
        scratch_shapes=[pltpu.VMEM((SUBLANES, c), F32), pltpu.VMEM((SUBLANES, c), F32),
                        pltpu.VMEM((ts, c), F32), pltpu.VMEM((ts, c), F32)],
        compiler_params=_cparams("parallel", "arbitrary"), name="rglru")(cx, cy, cw, cb, wax, ba, bx, lam)


def _merge_kernel(x_ref, ya_ref, yb_ref, yc_ref, g_ref, wg_ref, bg_ref, wb_ref, wo_ref, o_ref):
    x = x_ref[...]
    d = x.shape[1]
    h = _rms(x, g_ref[...]).astype(BF16)
    ys = (ya_ref[...], yb_ref[...], yc_ref[...])
    offs = (0, A_WIDTH, A_WIDTH + B_WIDTH, A_WIDTH + B_WIDTH + C_WIDTH)
    merged = None
    for j in range(N_BRANCH):
        gate = _sigmoid(jnp.dot(h, wg_ref[:, j * d:(j + 1) * d], preferred_element_type=F32)
                              + bg_ref[:, j * d:(j + 1) * d])
        term = gate * jnp.dot(ys[j], wb_ref[offs[j]:offs[j + 1], :], preferred_element_type=F32)
        merged = term if merged is None else merged + term
    o_ref[...] = x + jnp.dot(merged.astype(BF16), wo_ref[...], preferred_element_type=F32)


def _merge(layer, x2, ya, yb, yc, g, wg, bg, wb, wo, tm):
    n, d = x2.shape
    row = lambda w: pl.BlockSpec((tm, w), lambda i: (i, 0))
    return pl.pallas_call(
        _merge_kernel, grid=(n // tm,),
        in_specs=[row(d), row(A_WIDTH), row(B_WIDTH), row(C_WIDTH)]
        + [_layer_spec(a, layer) for a in (g, wg, bg, wb, wo)],
        out_specs=row(d), out_shape=jax.ShapeDtypeStruct((n, d), F32),
        compiler_params=_cparams("parallel"), name="merge")(x2, ya, yb, yc, g, wg, bg, wb, wo)


def _ffn_kernel(x_ref, g_ref, wi_ref, wo_ref, o_ref, *, d_ff, chunk):
    x = x_ref[...]
    h = _rms(x, g_ref[...]).astype(BF16)
    acc = x
    for c0 in range(0, d_ff, chunk):
        c1 = min(c0 + chunk, d_ff)
        gp = jnp.dot(h, wi_ref[:, c0:c1], preferred_element_type=F32)
        up = jnp.dot(h, wi_ref[:, d_ff + c0:d_ff + c1], preferred_element_type=F32)
        act = (gp * _sigmoid(gp) * up).astype(BF16)
        acc = acc + jnp.dot(act, wo_ref[c0:c1, :], preferred_element_type=F32)
    o_ref[...] = acc


def _ffn(layer, x2, g, wi, wo, tm):
    n, d = x2.shape
    d_ff = wo.shape[1]
    row = pl.BlockSpec((tm, d), lambda i: (i, 0))
    kernel = functools.partial(_ffn_kernel, d_ff=d_ff, chunk=1024)
    return pl.pallas_call(
        kernel, grid=(n // tm,),
        in_specs=[row] + [_layer_spec(a, layer) for a in (g, wi, wo)],
        out_specs=row, out_shape=jax.ShapeDtypeStruct((n, d), F32),
        compiler_params=_cparams("parallel"), name="ffn")(x2, g, wi, wo)


def _rope_tables(s):
    pos = jnp.arange(s, dtype=F32)
    inv = 1.0 / (ROPE_THETA ** (jnp.arange(0, HEAD_DIM, 2, dtype=F32) / HEAD_DIM))
    ang = pos[:, None] * inv[None, :]
    cos, sin = jnp.cos(ang), jnp.sin(ang)
    return jnp.concatenate([cos, cos], axis=-1), jnp.concatenate([sin, sin], axis=-1)


def _bias_table_kernel(vec_ref, o_ref):
    nk, q_tile = o_ref.shape
    rolled = pltpu.roll(jnp.broadcast_to(vec_ref[...], (nk, nk + q_tile)), 0, axis=1, stride=1, stride_axis=0)
    j = lax.broadcasted_iota(I32, (nk, q_tile), 0)
    q = lax.broadcasted_iota(I32, (nk, q_tile), 1)
    shift = CHUNK.bit_length() - 1
    dchunk = A_LEFT_CHUNKS + (q >> shift) - (j >> shift)
    valid = jnp.logical_and(dchunk >= 0, dchunk <= A_LEFT_CHUNKS)
    o_ref[...] = jnp.where(valid, rolled[:, nk:] * LOG2_E, -jnp.inf)


def _rel_bias_tables(rel_bias):
    depth = rel_bias.shape[0]
    nk = A_KEY_TILES * Q_TILE
    assert REL_CLIP == Q_TILE and CHUNK & (CHUNK - 1) == 0
    rb = rel_bias.astype(F32)
    vec = jnp.concatenate([rb, jnp.tile(rb[..., -1:], (1, 1, nk + Q_TILE - rb.shape[-1]))], axis=-1)
    return pl.pallas_call(
        _bias_table_kernel, grid=(depth, A_HEADS),
        in_specs=[pl.BlockSpec((None, None, 1, nk + Q_TILE), lambda l, h: (l, h, 0, 0))],
        out_specs=pl.BlockSpec((None, nk, Q_TILE), lambda l, h: (l, 0, h)),
        out_shape=jax.ShapeDtypeStruct((depth, nk, A_HEADS * Q_TILE), F32),
        compiler_params=_cparams("parallel", "parallel"), name="bias_table")(vec[:, :, None, :])


def kernel(x, g_mix, w_in, qk_gain_a, rel_bias, qk_gain_b, g_idx_k, conv_w, conv_b, lru_wa, lru_ba, lru_wx,
           lru_bx, lru_lambda, b_gate, w_branch, w_out, g_ffn, w_ffn_in, w_ffn_out):
    b, s, d = x.shape
    depth = g_mix.shape[0]
    assert s % PROJ_ROWS == 0 and d % LANES == 0
    tm = DENSE_ROWS
    cos, sin = _rope_tables(s)
    half = (jnp.arange(HEAD_DIM) < HEAD_DIM // 2).astype(F32)
    cos_rm = jnp.tile(cos, (1, LANES // HEAD_DIM))
    sina_rm = jnp.tile(-sin * half[None, :], (1, LANES // HEAD_DIM))
    sinb_rm = jnp.tile(sin * (1.0 - half)[None, :], (1, LANES // HEAD_DIM))
    cosT, sinT = cos.T, sin.T
    rows = lambda v: v.reshape(depth, 1, -1).astype(F32)
    blocks = jnp.arange(C_WIDTH) // (C_WIDTH // C_BLOCKS)
    bmask = (blocks[:, None] == blocks[None, :])

    def dense_blocks(wblk):
        return jnp.where(bmask, jnp.tile(wblk.reshape(depth, C_WIDTH, C_WIDTH // C_BLOCKS), (1, 1, C_BLOCKS)), 0.0)

    wrm, w_gate, wt = _weight_prep(w_in)
    head_gainT = lambda gain: jnp.tile(gain, (1, 4)).reshape(depth, 256, 1)
    gkaT, gkbT = head_gainT(qk_gain_a[:, 1]), head_gainT(qk_gain_b[:, 1])
    gqaT, gqbT = head_gainT(qk_gain_a[:, 0]), head_gainT(qk_gain_b[:, 0])
    gki = rows(jnp.concatenate([g_idx_k, jnp.zeros((depth, 64), F32)], axis=1))
    bias_tabs = _rel_bias_tables(rel_bias)
    wax = jnp.concatenate([dense_blocks(lru_wa), dense_blocks(lru_wx)], axis=2).astype(BF16)
    g_mix3, g_ffn3, b_gate3 = rows(g_mix), rows(g_ffn), rows(b_gate)
    cb3, ba3, bx3, lam3 = rows(conv_b), rows(lru_ba), rows(lru_bx), rows(lru_lambda)
    wb, wo = w_branch.astype(BF16), w_out.astype(BF16)
    wfi, wfo = w_ffn_in.astype(BF16), w_ffn_out.astype(BF16)

    n = b * s
    for l in range(depth):
        ka, kb, ki, cx, cy, qaT, vaT, qbT, vbT, qiT, iwT = _project(
            l, x, g_mix3, wrm, wt, gkaT, gkbT, gki, gqaT, gqbT, cos_rm, sina_rm, sinb_rm, cosT, sinT, PROJ_ROWS)
        ya = _chunk_attention(l, qaT, ka, vaT, bias_tabs)
        yb = _sparse_attention(qiT, iwT, qbT, ki, kb, vbT)
        yc = _rglru(l, cx, cy, conv_w, cb3, wax, ba3, bx3, lam3, 512)
        x1 = _merge(l, x.reshape(n, d), ya.reshape(n, A_WIDTH), yb.reshape(n, B_WIDTH), yc.reshape(n, C_WIDTH),
                    g_mix3, w_gate, b_gate3, wb, wo, tm)
        x = _ffn(l, x1, g_ffn3, wfi, wfo, tm).reshape(b, s, d)
    return x
```

```python
import functools
import math

import jax
import jax.numpy as jnp
from jax import lax
from jax.experimental import pallas as pl
from jax.experimental.pallas import tpu as pltpu

F32 = jnp.float32
BF16 = jnp.bfloat16
I32 = jnp.int32
U32 = jnp.uint32

CHUNK = 64
HEAD_DIM = 64
EPS = 1e-6
ROPE_THETA = 10000.0
A_HEADS = 4
A_LEFT_CHUNKS = 8
REL_CLIP = 128
B_HEADS = 4
IDX_HEADS = 4
IDX_DIM = 64
TOPK_MAX = 256
C_WIDTH = 512
C_BLOCKS = 8
C_CONV = 4
LRU_C = 8.0
N_BRANCH = 3
A_WIDTH = A_HEADS * HEAD_DIM
B_WIDTH = B_HEADS * HEAD_DIM

LANES = 128
SUBLANES = 8
Q_TILE = 128
A_KEY_TILES = A_LEFT_CHUNKS * CHUNK // Q_TILE + 1
PROJ_ROWS = 1024
DENSE_ROWS = 512
A_TILES_PER_STEP = 4
B_TILES_PER_STEP = 2
B_KEY_TILE = 512
PLANE_GROUP = 32 * SUBLANES
T_AQ, T_AV, T_BQ, T_BV, T_IQ, T_AK, T_BK, T_IW = range(8)
LOG2_E = math.log2(math.e)
INT_MIN = -2 ** 31
VMEM_LIMIT = 56 * 1024 * 1024

NT_DIMS = (((1,), (1,)), ((), ()))


def _cparams(*sem):
    return pltpu.CompilerParams(dimension_semantics=sem, vmem_limit_bytes=VMEM_LIMIT)


def _layer_spec(stacked, layer):
    nd = stacked.ndim - 1
    return pl.BlockSpec((None,) + stacked.shape[1:], lambda *_: (layer,) + (0,) * nd, pipeline_mode=pl.Buffered(1))


def _sigmoid(x):
    return 0.5 * jnp.tanh(0.5 * x) + 0.5


def _rms(x, gain):
    ms = jnp.mean(x * x, axis=-1, keepdims=True)
    return x * lax.rsqrt(ms + EPS) * gain


def _proj_kernel(x_ref, g_ref, wrm_ref, wt_ref, gkaT_ref, gkbT_ref, gki_ref, gqaT_ref, gqbT_ref,
                 cos_ref, sina_ref, sinb_ref, cosT_ref, sinT_ref,
                 ka_ref, kb_ref, ki_ref, cx_ref, cy_ref,
                 qaT_ref, vaT_ref, qbT_ref, vbT_ref, qiT_ref, iwT_ref):
    tm = x_ref.shape[1]
    h = _rms(x_ref[0], g_ref[...]).astype(BF16)

    def rm(c0, c1):
        return jnp.dot(h, wrm_ref[:, c0:c1], preferred_element_type=F32)

    def rope_rm(z):
        zm = pltpu.roll(z, LANES - HEAD_DIM // 2, axis=1)
        zp = pltpu.roll(z, HEAD_DIM // 2, axis=1)
        return z * cos_ref[...] + zm * sina_ref[...] + zp * sinb_ref[...]

    zi = rm(0, LANES)
    ms = jnp.sum(zi * zi, axis=-1, keepdims=True) * (1.0 / IDX_DIM)
    ki_ref[0] = rope_rm(zi * lax.rsqrt(ms + EPS) * gki_ref[...]).astype(BF16)
    cx_ref[0] = rm(LANES, LANES + C_WIDTH)
    cy_ref[0] = rm(LANES + C_WIDTH, LANES + 2 * C_WIDTH)

    def tr(r0, r1):
        return lax.dot_general(wt_ref[r0:r1, :], h, NT_DIMS, preferred_element_type=F32)

    def head_rms_t(z, gain):
        z4 = z.reshape(4, HEAD_DIM, tm)
        ms4 = jnp.mean(z4 * z4, axis=1, keepdims=True)
        return (z4 * lax.rsqrt(ms4 + EPS)).reshape(4 * HEAD_DIM, tm) * gain

    def rope_t(z):
        z4 = z.reshape(4, HEAD_DIM, tm)
        rot = jnp.concatenate([-z4[:, HEAD_DIM // 2:, :], z4[:, :HEAD_DIM // 2, :]], axis=1)
        return (z4 * cosT_ref[...][None] + rot * sinT_ref[...][None]).reshape(4 * HEAD_DIM, tm)

    def tr_heads(group):
        return tr(group * 4 * HEAD_DIM, (group + 1) * 4 * HEAD_DIM)

    scale = HEAD_DIM ** -0.5 * LOG2_E
    qaT_ref[0] = (head_rms_t(tr_heads(T_AQ), gqaT_ref[...]) * scale).astype(BF16)
    vaT_ref[0] = tr_heads(T_AV).astype(BF16)
    qbT_ref[0] = (rope_t(head_rms_t(tr_heads(T_BQ), gqbT_ref[...])) * scale).astype(BF16)
    vbT_ref[0] = tr_heads(T_BV).astype(BF16)
    qiT_ref[0] = rope_t(tr_heads(T_IQ)).astype(BF16)
    ka_ref[0] = head_rms_t(tr_heads(T_AK), gkaT_ref[...]).T.astype(BF16)
    kb_ref[0] = rope_t(head_rms_t(tr_heads(T_BK), gkbT_ref[...])).T.astype(BF16)
    iw0 = T_IW * 4 * HEAD_DIM
    iwT_ref[0] = tr(iw0, iw0 + 2 * SUBLANES)[0:SUBLANES] * (IDX_HEADS ** -0.5 * IDX_DIM ** -0.5)


def _project(layer, x, g, wrm, wt, gkaT, gkbT, gki, gqaT, gqbT, cos, sina, sinb, cosT, sinT, tm):
    b, s, d = x.shape
    grid = (b, s // tm)
    row = lambda w: pl.BlockSpec((1, tm, w), lambda bi, si: (bi, si, 0))
    col = lambda r: pl.BlockSpec((1, r, tm), lambda bi, si: (bi, 0, si))
    tab_rm = pl.BlockSpec((tm, LANES), lambda bi, si: (si, 0))
    tab_t = pl.BlockSpec((HEAD_DIM, tm), lambda bi, si: (0, si))
    per_layer = lambda a: _layer_spec(a, layer)
    in_specs = [row(d), per_layer(g), per_layer(wrm), per_layer(wt),
                per_layer(gkaT), per_layer(gkbT), per_layer(gki), per_layer(gqaT), per_layer(gqbT),
                tab_rm, tab_rm, tab_rm, tab_t, tab_t]
    out_shape = (jax.ShapeDtypeStruct((b, s, 256), BF16), jax.ShapeDtypeStruct((b, s, 256), BF16),
                 jax.ShapeDtypeStruct((b, s, 128), BF16),
                 jax.ShapeDtypeStruct((b, s, C_WIDTH), F32), jax.ShapeDtypeStruct((b, s, C_WIDTH), F32),
                 jax.ShapeDtypeStruct((b, 256, s), BF16), jax.ShapeDtypeStruct((b, 256, s), BF16),
                 jax.ShapeDtypeStruct((b, 256, s), BF16), jax.ShapeDtypeStruct((b, 256, s), BF16),
                 jax.ShapeDtypeStruct((b, 256, s), BF16), jax.ShapeDtypeStruct((b, 8, s), F32))
    out_specs = (row(256), row(256), row(128), row(C_WIDTH), row(C_WIDTH),
                 col(256), col(256), col(256), col(256), col(256), col(8))
    return pl.pallas_call(_proj_kernel, grid=grid, in_specs=in_specs, out_specs=out_specs, out_shape=out_shape,
                          compiler_params=_cparams("parallel", "parallel"), name="proj")(
        x, g, wrm, wt, gkaT, gkbT, gki, gqaT, gqbT, cos, sina, sinb, cosT, sinT)


O_AQ, O_AK, O_AV, O_BQ, O_BK, O_BV = 0, 256, 512, 768, 1024, 1280
O_IQ, O_IK, O_IW, O_CX, O_CY, O_GT = 1536, 1792, 1856, 1860, 2372, 2884
W_ROWMAJOR_COLS = LANES + 2 * C_WIDTH
W_TRANSPOSED_ROWS = 7 * 256 + LANES
W_PREP_ROWS = 256


def _weight_prep_kernel(w_ref, wrm_ref, wg_ref, wt_ref):
    rows = w_ref.shape[0]
    cols = lambda o, n: w_ref[:, o:o + n]
    wrm_ref[...] = jnp.concatenate(
        [cols(O_IK, IDX_DIM), jnp.zeros((rows, LANES - IDX_DIM), F32), cols(O_CX, C_WIDTH), cols(O_CY, C_WIDTH)],
        axis=1).astype(BF16)
    wg_ref[...] = w_ref[:, O_GT:].astype(BF16)
    groups = {T_AQ: O_AQ, T_AV: O_AV, T_BQ: O_BQ, T_BV: O_BV, T_IQ: O_IQ, T_AK: O_AK, T_BK: O_BK}
    wt = jnp.concatenate([cols(groups[t], 256) for t in range(T_IW)]
                         + [cols(O_IW, IDX_HEADS), jnp.zeros((rows, LANES - IDX_HEADS), F32)], axis=1)
    wt_ref[...] = wt.T.astype(BF16)


def _weight_prep(w_in):
    depth, d, n = w_in.shape
    n_gate = n - O_GT
    rb = W_PREP_ROWS
    return pl.pallas_call(
        _weight_prep_kernel, grid=(depth, d // rb),
        in_specs=[pl.BlockSpec((None, rb, n), lambda l, i: (l, i, 0))],
        out_specs=(pl.BlockSpec((None, rb, W_ROWMAJOR_COLS), lambda l, i: (l, i, 0)),
                   pl.BlockSpec((None, rb, n_gate), lambda l, i: (l, i, 0)),
                   pl.BlockSpec((None, W_TRANSPOSED_ROWS, rb), lambda l, i: (l, 0, i))),
        out_shape=(jax.ShapeDtypeStruct((depth, d, W_ROWMAJOR_COLS), BF16),
                   jax.ShapeDtypeStruct((depth, d, n_gate), BF16),
                   jax.ShapeDtypeStruct((depth, W_TRANSPOSED_ROWS, d), BF16)),
        compiler_params=_cparams("parallel", "parallel"), name="weight_prep")(w_in)


def _block_diag_q(qT):
    rows = lax.broadcasted_iota(I32, qT.shape, 0) // HEAD_DIM
    zero = jnp.zeros_like(qT)
    return jnp.concatenate([jnp.where(rows == hh, qT, zero) for hh in range(4)], axis=1)


def _chunk_attn_kernel(qT_ref, k_ref, vT_ref, bias_ref, o_ref):
    for sub in range(A_TILES_PER_STEP):
        _chunk_attn_tile(pl.program_id(1) * A_TILES_PER_STEP + sub,
                         qT_ref.at[0, :, sub * Q_TILE:(sub + 1) * Q_TILE], k_ref, vT_ref, bias_ref,
                         o_ref.at[0, sub * Q_TILE:(sub + 1) * Q_TILE, :])


def _chunk_attn_tile(i, qT_ref, k_ref, vT_ref, bias_ref, o_ref):
    qbd = _block_diag_q(qT_ref[...])
    ks, vs, pens = [], [], []
    for t in range(A_KEY_TILES):
        kt = i - (A_KEY_TILES - 1) + t
        r0 = pl.multiple_of(jnp.maximum(kt, 0) * Q_TILE, Q_TILE)
        ks.append(k_ref[0, pl.ds(r0, Q_TILE), :])
        vs.append(vT_ref[0, :, pl.ds(r0, Q_TILE)])
        pens.append(jnp.where(kt >= 0, 0.0, -jnp.inf).astype(F32))
    kwin = jnp.concatenate(ks, axis=0)
    vwin = jnp.concatenate(vs, axis=1)
    s = jnp.dot(kwin, qbd, preferred_element_type=F32) + bias_ref[...]
    s = jnp.concatenate([s[t * Q_TILE:(t + 1) * Q_TILE] + pens[t] for t in range(A_KEY_TILES)], axis=0)
    m = jnp.max(s, axis=0, keepdims=True)
    p = jnp.exp2(s - m)
    l = jnp.sum(p, axis=0, keepdims=True)
    pb = p.astype(BF16)
    outs = []
    for hh in range(A_HEADS):
        lo, hi = hh * Q_TILE, (hh + 1) * Q_TILE
        o = jnp.dot(vwin[hh * HEAD_DIM:(hh + 1) * HEAD_DIM, :], pb[:, lo:hi], preferred_element_type=F32)
        outs.append(o / l[:, lo:hi])
    o_ref[...] = jnp.concatenate(outs, axis=0).T.astype(BF16)


def _chunk_attention(layer, qaT, ka, vaT, biasT):
    b, _, s = qaT.shape
    step = A_TILES_PER_STEP * Q_TILE
    grid = (b, s // step)
    return pl.pallas_call(
        _chunk_attn_kernel, grid=grid,
        in_specs=[pl.BlockSpec((1, 256, step), lambda bi, i: (bi, 0, i)),
                  pl.BlockSpec((1, s, 256), lambda bi, i: (bi, 0, 0)),
                  pl.BlockSpec((1, 256, s), lambda bi, i: (bi, 0, 0)),
                  _layer_spec(biasT, layer)],
        out_specs=pl.BlockSpec((1, step, 256), lambda bi, i: (bi, i, 0)),
        out_shape=jax.ShapeDtypeStruct((b, s, 256), BF16),
        compiler_params=_cparams("parallel", "parallel"), name="chunk_attn")(qaT, ka, vaT, biasT)


def _bit_transpose32(words):
    a = list(words)
    mask, j = 0x0000FFFF, 16
    while j:
        k = 0
        while k < 32:
            t = (a[k] ^ (a[k + j] >> jnp.uint32(j))) & jnp.uint32(mask)
            a[k] = a[k] ^ t
            a[k + j] = a[k + j] ^ (t << jnp.uint32(j))
            k = (k + j + 1) & ~j
        j >>= 1
        mask = (mask ^ (mask << j)) & 0xFFFFFFFF
    return a


def _dsa_kernel(qiT_ref, iwT_ref, qbT_ref, ki_ref, kb_ref, vbT_ref, o_ref, scores_ref, planes_ref, *, topk):
    tk = B_KEY_TILE
    pair = pl.program_id(1)
    tiles = range(B_TILES_PER_STEP)
    n_small = pair + 1
    n_wide = lax.shift_right_logical(n_small, 2)
    has_mid, has_small = (n_small & 2) != 0, (n_small & 1) != 0
    lane = lax.broadcasted_iota(I32, (1, Q_TILE), 1)
    qis, ws, key_limits = [], [], []
    for tq in tiles:
        qiT = qiT_ref[0, :, tq * Q_TILE:(tq + 1) * Q_TILE]
        qi = jnp.concatenate([qiT[hh * IDX_DIM:(hh + 1) * IDX_DIM] for hh in range(IDX_HEADS)], axis=1)
        qis.append(jnp.concatenate([qi, jnp.zeros_like(qi)], axis=0))
        ws.append(iwT_ref[0, :, tq * Q_TILE:(tq + 1) * Q_TILE])
        i = pair * B_TILES_PER_STEP + tq
        key_limits.append(i * Q_TILE + CHUNK + jnp.where(lane >= CHUNK, CHUNK, 0))

    def over_key_tiles(span, carry):
        wide, small = 2 * tk, tk // 2
        carry = lax.fori_loop(0, n_wide, lambda u, c: span(pl.multiple_of(u * wide, wide), wide, c), carry)
        r_mid = pl.multiple_of(n_wide * wide, wide)
        carry = lax.cond(has_mid, lambda c: span(r_mid, tk, c), lambda c: c, carry)
        r_small = pl.multiple_of(r_mid + jnp.where(has_mid, tk, 0), small)
        return lax.cond(has_small, lambda c: span(r_small, small, c), lambda c: c, carry)

    def score_span(r0, rows, c):
        ki = ki_ref[0, pl.ds(r0, rows), :]
        row_iota = lax.broadcasted_iota(I32, (rows, Q_TILE), 0)
        for tq in tiles:
            dots = jnp.dot(ki, qis[tq], preferred_element_type=F32)
            sc = jnp.maximum(dots[:, 0:Q_TILE], 0.0) * ws[tq][0:1, :]
            for hh in range(1, IDX_HEADS):
                sc = sc + jnp.maximum(dots[:, hh * Q_TILE:(hh + 1) * Q_TILE], 0.0) * ws[tq][hh:hh + 1, :]
            sc = jnp.where(row_iota < key_limits[tq] - r0, sc, -jnp.inf)
            scores_ref[tq, pl.ds(r0, rows), :] = sc
            bits = lax.bitcast_convert_type(sc, U32)
            for g0 in range(0, rows, PLANE_GROUP):
                words = [bits[g0 + j * SUBLANES:g0 + (j + 1) * SUBLANES] for j in range(32)]
                planes = _bit_transpose32(words)
                magnitude = planes[1]
                for p in planes[2:]:
                    magnitude = magnitude | p
                negative = planes[0] & magnitude
                planes = [~negative] + [p ^ negative for p in planes[1:]]
                planes_ref[tq, pl.ds(r0 + g0, PLANE_GROUP), :] = jnp.concatenate(planes, axis=0)
        return c

    @pl.when(pair == 0)
    def _():
        planes_ref[...] = jnp.zeros_like(planes_ref)

    over_key_tiles(score_span, 0)

    max_groups = planes_ref.shape[1] // PLANE_GROUP
    full = jnp.full((SUBLANES, Q_TILE), 0xFFFFFFFF, U32)
    zero = jnp.zeros((1, Q_TILE), I32)

    def members(masks):
        parts = [lax.population_count(m) for m in masks]
        while len(parts) > 1:
            parts = [parts[j] + parts[j + 1] for j in range(0, len(parts) - 1, 2)] + parts[len(parts) & ~1:]
        return jnp.sum(parts[0].astype(I32), axis=0, keepdims=True)

    def search(n_groups):
        live0 = tuple(jnp.where(g < n_small, full, jnp.zeros_like(full)) for g in range(n_groups))

        def two_bits_tile(tq, row, bit_hi, state):
            live, n_gt, thr_bits = state
            p_hi = [planes_ref[tq, pl.ds(g * PLANE_GROUP + row, SUBLANES), :] for g in range(n_groups)]
            p_lo = [planes_ref[tq, pl.ds(g * PLANE_GROUP + row + SUBLANES, SUBLANES), :] for g in range(n_groups)]
            set_hi = [live[g] & p_hi[g] for g in range(n_groups)]
            clr_hi = [live[g] ^ set_hi[g] for g in range(n_groups)]
            set_both = [set_hi[g] & p_lo[g] for g in range(n_groups)]
            clr_set = [clr_hi[g] & p_lo[g] for g in range(n_groups)]
            n_hi = n_gt + members(set_hi)
            take_hi = n_hi >= topk
            n_gt = jnp.where(take_hi, n_gt, n_hi)
            n_lo = n_gt + jnp.where(take_hi, members(set_both), members(clr_set))
            take_lo = n_lo >= topk
            new_live = []
            for g in range(n_groups):
                kept = jnp.where(take_hi, set_hi[g], clr_hi[g])
                low = jnp.where(take_hi, set_both[g], clr_set[g])
                new_live.append(jnp.where(take_lo, low, kept ^ low))
            bit_lo = lax.shift_right_logical(bit_hi, 1)
            thr_bits = thr_bits | jnp.where(take_hi, bit_hi, 0) | jnp.where(take_lo, bit_lo, 0)
            return tuple(new_live), jnp.where(take_lo, n_gt, n_lo), thr_bits

        def two_bits(it, states):
            row = pl.multiple_of(it * (2 * SUBLANES), 2 * SUBLANES)
            bit_hi = jnp.int32(1) << (31 - 2 * it)
            return tuple(two_bits_tile(tq, row, bit_hi, states[tq]) for tq in tiles)

        states = lax.fori_loop(0, 16, two_bits, tuple((live0, zero, zero) for _ in tiles))
        return tuple((n_gt, thr_bits) for _, n_gt, thr_bits in states)

    if max_groups % 2 == 0:
        found = lax.cond(n_small <= max_groups // 2, lambda: search(max_groups // 2), lambda: search(max_groups))
    else:
        found = search(max_groups)
    thrs, needs = [], []
    for n_gt, thr_bits in found:
        thr = lax.bitcast_convert_type(jnp.where(thr_bits < 0, thr_bits ^ jnp.int32(INT_MIN), ~thr_bits), F32)
        thrs.append(thr)
        needs.append(jnp.where(thr == -jnp.inf, 0, topk - n_gt).astype(F32))
    hk = tk // 2
    tril = (lax.broadcasted_iota(I32, (hk, hk), 0) >= lax.broadcasted_iota(I32, (hk, hk), 1)).astype(BF16)

    qbds = [_block_diag_q(qbT_ref[0, :, tq * Q_TILE:(tq + 1) * Q_TILE]) for tq in tiles]

    def attn_span_tile(tq, r0, rows, kb, carry):
        ms, ls, accs, seen = carry
        key = scores_ref[tq, pl.ds(r0, rows), :]
        tied = key == thrs[tq]
        tied_b = jnp.where(tied, 1.0, 0.0).astype(BF16)
        ranks = []
        for c0 in range(0, rows, hk):
            ranks.append(jnp.dot(tril, tied_b[c0:c0 + hk], preferred_element_type=F32) + seen)
            seen = ranks[-1][hk - 1:hk, :]
        sel = jnp.logical_or(key > thrs[tq], jnp.logical_and(tied, jnp.concatenate(ranks, axis=0) <= needs[tq]))
        s_all = jnp.dot(kb, qbds[tq], preferred_element_type=F32)
        new_ms, new_ls, new_accs = [], [], []
        for hh in range(B_HEADS):
            s = jnp.where(sel, s_all[:, hh * Q_TILE:(hh + 1) * Q_TILE], -jnp.inf)
            m_new = jnp.maximum(ms[hh], jnp.max(s, axis=0, keepdims=True))
            alpha = jnp.exp2(ms[hh] - m_new)
            p = jnp.exp2(s - m_new)
            new_ls.append(alpha * ls[hh] + jnp.sum(p, axis=0, keepdims=True))
            pv = jnp.dot(vbT_ref[0, hh * HEAD_DIM:(hh + 1) * HEAD_DIM, pl.ds(r0, rows)], p.astype(BF16),
                         preferred_element_type=F32)
            new_accs.append(alpha * accs[hh] + pv)
            new_ms.append(m_new)
        return tuple(new_ms), tuple(new_ls), tuple(new_accs), seen

    def attn_span(r0, rows, carries):
        kb = kb_ref[0, pl.ds(r0, rows), :]
        return tuple(attn_span_tile(tq, r0, rows, kb, carries[tq]) for tq in tiles)

    init = (tuple(jnp.full((1, Q_TILE), -1e30, F32) for _ in range(B_HEADS)),
            tuple(jnp.zeros((1, Q_TILE), F32) for _ in range(B_HEADS)),
            tuple(jnp.zeros((HEAD_DIM, Q_TILE), F32) for _ in range(B_HEADS)),
            jnp.zeros((1, Q_TILE), F32))
    finals = over_key_tiles(attn_span, tuple(init for _ in tiles))
    for tq in tiles:
        _, ls, accs, _ = finals[tq]
        out = jnp.concatenate([accs[hh] / ls[hh] for hh in range(B_HEADS)], axis=0)
        o_ref[0, tq * Q_TILE:(tq + 1) * Q_TILE, :] = out.T.astype(BF16)


def _sparse_attention(qiT, iwT, qbT, ki, kb, vbT):
    b, _, s = qbT.shape
    s_pad = -(-s // B_KEY_TILE) * B_KEY_TILE
    topk = min(TOPK_MAX, s // 4)
    step = B_TILES_PER_STEP * Q_TILE
    grid = (b, s // step)
    qspec = lambda r: pl.BlockSpec((1, r, step), lambda bi, i: (bi, 0, i))
    kernel = functools.partial(_dsa_kernel, topk=topk)
    return pl.pallas_call(
        kernel, grid=grid,
        in_specs=[qspec(256), qspec(8), qspec(256),
                  pl.BlockSpec((1, s, 128), lambda bi, i: (bi, 0, 0)),
                  pl.BlockSpec((1, s, 256), lambda bi, i: (bi, 0, 0)),
                  pl.BlockSpec((1, 256, s), lambda bi, i: (bi, 0, 0))],
        out_specs=pl.BlockSpec((1, step, 256), lambda bi, i: (bi, i, 0)),
        out_shape=jax.ShapeDtypeStruct((b, s, 256), BF16),
        scratch_shapes=[pltpu.VMEM((B_TILES_PER_STEP, s_pad, Q_TILE), F32),
                        pltpu.VMEM((B_TILES_PER_STEP, s_pad, Q_TILE), U32)],
        compiler_params=_cparams("parallel", "arbitrary"), name="sparse_attn")(qiT, iwT, qbT, ki, kb, vbT)


def _rglru_kernel(cx_ref, cy_ref, cw_ref, cb_ref, wax_ref, ba_ref, bx_ref, lam_ref, o_ref,
                  tail_ref, h_ref, a_ref, g_ref):
    ts = cx_ref.shape[1]

    @pl.when(pl.program_id(1) == 0)
    def _():
        tail_ref[...] = jnp.zeros_like(tail_ref)
        h_ref[...] = jnp.zeros_like(h_ref)

    x = cx_ref[0]
    tail = tail_ref[...]
    row8 = lax.broadcasted_iota(I32, (SUBLANES, C_WIDTH), 0)
    u = cb_ref[...] + x * cw_ref[C_CONV - 1:C_CONV, :]
    for k in range(1, C_CONV):
        xr = pltpu.roll(x, k, axis=0)
        head = jnp.where(row8 < k, pltpu.roll(tail, k, axis=0), xr[0:SUBLANES])
        xk = jnp.concatenate([head, xr[SUBLANES:]], axis=0)
        u = u + xk * cw_ref[C_CONV - 1 - k:C_CONV - k, :]
    tail_ref[...] = x[ts - SUBLANES:ts]

    gates = jnp.dot(u.astype(BF16), wax_ref[...], preferred_element_type=F32)
    r = _sigmoid(gates[:, :C_WIDTH] + ba_ref[...])
    ig = _sigmoid(gates[:, C_WIDTH:] + bx_ref[...])
    nlam = -lam_ref[...]
    softplus = jnp.maximum(nlam, 0.0) + jnp.log1p(jnp.exp(-jnp.abs(nlam)))
    a = jnp.exp2(r * (softplus * (-LRU_C * LOG2_E)))
    a_ref[...] = a
    y = 1.0 - a * a
    root = jnp.where(y > 0.0, y * lax.rsqrt(y), 0.0)
    g_ref[...] = root * (ig * u)

    def group(gi, hprev):
        r0 = pl.multiple_of(gi * SUBLANES, SUBLANES)
        av = a_ref[pl.ds(r0, SUBLANES), :]
        bv = g_ref[pl.ds(r0, SUBLANES), :]
        for d in (1, 2, 4):
            keep = row8 >= d
            a_sh = jnp.where(keep, pltpu.roll(av, d, axis=0), 1.0)
            b_sh = jnp.where(keep, pltpu.roll(bv, d, axis=0), 0.0)
            bv = av * b_sh + bv
            av = av * a_sh
        hs = av * hprev + bv
        g_ref[pl.ds(r0, SUBLANES), :] = hs
        return jnp.broadcast_to(hs[SUBLANES - 1:SUBLANES, :], hs.shape)

    h_ref[...] = lax.fori_loop(0, ts // SUBLANES, group, h_ref[...], unroll=8)
    o_ref[0] = (g_ref[...] * jax.nn.gelu(cy_ref[0])).astype(BF16)


def _rglru(layer, cx, cy, cw, cb, wax, ba, bx, lam, ts):
    b, s, c = cx.shape
    blk = pl.BlockSpec((1, ts, c), lambda bi, si: (bi, si, 0))
    return pl.pallas_call(
        _rglru_kernel, grid=(b, s // ts),
        in_specs=[blk, blk] + [_layer_spec(a, layer) for a in (cw, cb, wax, ba, bx, lam)],
        out_specs=blk, out_shape=jax.ShapeDtypeStruct((b, s, c), BF16),
        scratch_shapes=[pltpu.VMEM((SUBLANES, c), F32), pltpu.VMEM((SUBLANES, c), F32),
                        pltpu.VMEM((ts, c), F32), pltpu.VMEM((ts, c), F32)],
        compiler_params=_cparams("parallel", "arbitrary"), name="rglru")(cx, cy, cw, cb, wax, ba, bx, lam)


def _merge_kernel(x_ref, ya_ref, yb_ref, yc_ref, g_ref, wg_ref, bg_ref, wb_ref, wo_ref, o_ref):
    x = x_ref[...]
    d = x.shape[1]
    h = _rms(x, g_ref[...]).astype(BF16)
    ys = (ya_ref[...], yb_ref[...], yc_ref[...])
    offs = (0, A_WIDTH, A_WIDTH + B_WIDTH, A_WIDTH + B_WIDTH + C_WIDTH)
    merged = None
    for j in range(N_BRANCH):
        gate = _sigmoid(jnp.dot(h, wg_ref[:, j * d:(j + 1) * d], preferred_element_type=F32)
                              + bg_ref[:, j * d:(j + 1) * d])
        term = gate * jnp.dot(ys[j], wb_ref[offs[j]:offs[j + 1], :], preferred_element_type=F32)
        merged = term if merged is None else merged + term
    o_ref[...] = x + jnp.dot(merged.astype(BF16), wo_ref[...], preferred_element_type=F32)


def _merge(layer, x2, ya, yb, yc, g, wg, bg, wb, wo, tm):
    n, d = x2.shape
    row = lambda w: pl.BlockSpec((tm, w), lambda i: (i, 0))
    return pl.pallas_call(
        _merge_kernel, grid=(n // tm,),
        in_specs=[row(d), row(A_WIDTH), row(B_WIDTH), row(C_WIDTH)]
        + [_layer_spec(a, layer) for a in (g, wg, bg, wb, wo)],
        out_specs=row(d), out_shape=jax.ShapeDtypeStruct((n, d), F32),
        compiler_params=_cparams("parallel"), name="merge")(x2, ya, yb, yc, g, wg, bg, wb, wo)


def _ffn_kernel(x_ref, g_ref, wi_ref, wo_ref, o_ref, *, d_ff, chunk):
    x = x_ref[...]
    h = _rms(x, g_ref[...]).astype(BF16)
    acc = x
    for c0 in range(0, d_ff, chunk):
        c1 = min(c0 + chunk, d_ff)
        gp = jnp.dot(h, wi_ref[:, c0:c1], preferred_element_type=F32)
        up = jnp.dot(h, wi_ref[:, d_ff + c0:d_ff + c1], preferred_element_type=F32)
        act = (gp * _sigmoid(gp) * up).astype(BF16)
        acc = acc + jnp.dot(act, wo_ref[c0:c1, :], preferred_element_type=F32)
    o_ref[...] = acc


def _ffn(layer, x2, g, wi, wo, tm):
    n, d = x2.shape
    d_ff = wo.shape[1]
    row = pl.BlockSpec((tm, d), lambda i: (i, 0))
    kernel = functools.partial(_ffn_kernel, d_ff=d_ff, chunk=1024)
    return pl.pallas_call(
        kernel, grid=(n // tm,),
        in_specs=[row] + [_layer_spec(a, layer) for a in (g, wi, wo)],
        out_specs=row, out_shape=jax.ShapeDtypeStruct((n, d), F32),
        compiler_params=_cparams("parallel"), name="ffn")(x2, g, wi, wo)


def _rope_tables(s):
    pos = jnp.arange(s, dtype=F32)
    inv = 1.0 / (ROPE_THETA ** (jnp.arange(0, HEAD_DIM, 2, dtype=F32) / HEAD_DIM))
    ang = pos[:, None] * inv[None, :]
    cos, sin = jnp.cos(ang), jnp.sin(ang)
    return jnp.concatenate([cos, cos], axis=-1), jnp.concatenate([sin, sin], axis=-1)


def _bias_table_kernel(vec_ref, o_ref):
    nk, q_tile = o_ref.shape
    rolled = pltpu.roll(jnp.broadcast_to(vec_ref[...], (nk, nk + q_tile)), 0, axis=1, stride=1, stride_axis=0)
    j = lax.broadcasted_iota(I32, (nk, q_tile), 0)
    q = lax.broadcasted_iota(I32, (nk, q_tile), 1)
    shift = CHUNK.bit_length() - 1
    dchunk = A_LEFT_CHUNKS + (q >> shift) - (j >> shift)
    valid = jnp.logical_and(dchunk >= 0, dchunk <= A_LEFT_CHUNKS)
    o_ref[...] = jnp.where(valid, rolled[:, nk:] * LOG2_E, -jnp.inf)


def _rel_bias_tables(rel_bias):
    depth = rel_bias.shape[0]
    nk = A_KEY_TILES * Q_TILE
    assert REL_CLIP == Q_TILE and CHUNK & (CHUNK - 1) == 0
    rb = rel_bias.astype(F32)
    vec = jnp.concatenate([rb, jnp.tile(rb[..., -1:], (1, 1, nk + Q_TILE - rb.shape[-1]))], axis=-1)
    return pl.pallas_call(
        _bias_table_kernel, grid=(depth, A_HEADS),
        in_specs=[pl.BlockSpec((None, None, 1, nk + Q_TILE), lambda l, h: (l, h, 0, 0))],
        out_specs=pl.BlockSpec((None, nk, Q_TILE), lambda l, h: (l, 0, h)),
        out_shape=jax.ShapeDtypeStruct((depth, nk, A_HEADS * Q_TILE), F32),
        compiler_params=_cparams("parallel", "parallel"), name="bias_table")(vec[:, :, None, :])


def kernel(x, g_mix, w_in, qk_gain_a, rel_bias, qk_gain_b, g_idx_k, conv_w, conv_b, lru_wa, lru_ba, lru_wx,
           lru_bx, lru_lambda, b_gate, w_branch, w_out, g_ffn, w_ffn_in, w_ffn_out):
    b, s, d = x.shape
    depth = g_mix.shape[0]
    assert s % PROJ_ROWS == 0 and d % LANES == 0
    tm = DENSE_ROWS
    cos, sin = _rope_tables(s)
    half = (jnp.arange(HEAD_DIM) < HEAD_DIM // 2).astype(F32)
    cos_rm = jnp.tile(cos, (1, LANES // HEAD_DIM))
    sina_rm = jnp.tile(-sin * half[None, :], (1, LANES // HEAD_DIM))
    sinb_rm = jnp.tile(sin * (1.0 - half)[None, :], (1, LANES // HEAD_DIM))
    cosT, sinT = cos.T, sin.T
    rows = lambda v: v.reshape(depth, 1, -1).astype(F32)
    blocks = jnp.arange(C_WIDTH) // (C_WIDTH // C_BLOCKS)
    bmask = (blocks[:, None] == blocks[None, :])

    def dense_blocks(wblk):
        return jnp.where(bmask, jnp.tile(wblk.reshape(depth, C_WIDTH, C_WIDTH // C_BLOCKS), (1, 1, C_BLOCKS)), 0.0)

    wrm, w_gate, wt = _weight_prep(w_in)
    head_gainT = lambda gain: jnp.tile(gain, (1, 4)).reshape(depth, 256, 1)
    gkaT, gkbT = head_gainT(qk_gain_a[:, 1]), head_gainT(qk_gain_b[:, 1])
    gqaT, gqbT = head_gainT(qk_gain_a[:, 0]), head_gainT(qk_gain_b[:, 0])
    gki = rows(jnp.concatenate([g_idx_k, jnp.zeros((depth, 64), F32)], axis=1))
    bias_tabs = _rel_bias_tables(rel_bias)
    wax = jnp.concatenate([dense_blocks(lru_wa), dense_blocks(lru_wx)], axis=2).astype(BF16)
    g_mix3, g_ffn3, b_gate3 = rows(g_mix), rows(g_ffn), rows(b_gate)
    cb3, ba3, bx3, lam3 = rows(conv_b), rows(lru_ba), rows(lru_bx), rows(lru_lambda)
    wb, wo = w_branch.astype(BF16), w_out.astype(BF16)
    wfi, wfo = w_ffn_in.astype(BF16), w_ffn_out.astype(BF16)

    n = b * s
    for l in range(depth):
        ka, kb, ki, cx, cy, qaT, vaT, qbT, vbT, qiT, iwT = _project(
            l, x, g_mix3, wrm, wt, gkaT, gkbT, gki, gqaT, gqbT, cos_rm, sina_rm, sinb_rm, cosT, sinT, PROJ_ROWS)
        ya = _chunk_attention(l, qaT, ka, vaT, bias_tabs)
        yb = _sparse_attention(qiT, iwT, qbT, ki, kb, vbT)
        yc = _rglru(l, cx, cy, conv_w, cb3, wax, ba3, bx3, lam3, 512)
        x1 = _merge(l, x.reshape(n, d), ya.reshape(n, A_WIDTH), yb.reshape(n, B_WIDTH), yc.reshape(n, C_WIDTH),
                    g_mix3, w_gate, b_gate3, wb, wo, tm)
        x = _ffn(l, x1, g_ffn3, wfi, wfo, tm).reshape(b, s, d)
    return x
```

```python
import functools
import math

import jax
import jax.numpy as jnp
from jax import lax
from jax.experimental import pallas as pl
from jax.experimental.pallas import tpu as pltpu

F32 = jnp.float32
BF16 = jnp.bfloat16
I32 = jnp.int32
U32 = jnp.uint32

CHUNK = 64
HEAD_DIM = 64
EPS = 1e-6
ROPE_THETA = 10000.0
A_HEADS = 4
A_LEFT_CHUNKS = 8
REL_CLIP = 128
B_HEADS = 4
IDX_HEADS = 4
IDX_DIM = 64
TOPK_MAX = 256
C_WIDTH = 512
C_BLOCKS = 8
C_CONV = 4
LRU_C = 8.0
N_BRANCH = 3
A_WIDTH = A_HEADS * HEAD_DIM
B_WIDTH = B_HEADS * HEAD_DIM

LANES = 128
SUBLANES = 8
Q_TILE = 128
A_KEY_TILES = A_LEFT_CHUNKS * CHUNK // Q_TILE + 1
PROJ_ROWS = 1024
DENSE_ROWS = 512
A_TILES_PER_STEP = 4
B_TILES_PER_STEP = 2
B_KEY_TILE = 512
PLANE_GROUP = 32 * SUBLANES
T_AQ, T_AV, T_BQ, T_BV, T_IQ, T_AK, T_BK, T_IW = range(8)
LOG2_E = math.log2(math.e)
INT_MIN = -2 ** 31
VMEM_LIMIT = 56 * 1024 * 1024

NT_DIMS = (((1,), (1,)), ((), ()))


def _cparams(*sem):
    return pltpu.CompilerParams(dimension_semantics=sem, vmem_limit_bytes=VMEM_LIMIT)


def _layer_spec(stacked, layer):
    nd = stacked.ndim - 1
    return pl.BlockSpec((None,) + stacked.shape[1:], lambda *_: (layer,) + (0,) * nd, pipeline_mode=pl.Buffered(1))


def _sigmoid(x):
    return 0.5 * jnp.tanh(0.5 * x) + 0.5


def _rms(x, gain):
    ms = jnp.mean(x * x, axis=-1, keepdims=True)
    return x * lax.rsqrt(ms + EPS) * gain


def _proj_kernel(x_ref, g_ref, wrm_ref, wt_ref, gkaT_ref, gkbT_ref, gki_ref, gqaT_ref, gqbT_ref,
                 cos_ref, sina_ref, sinb_ref, cosT_ref, sinT_ref,
                 ka_ref, kb_ref, ki_ref, cx_ref, cy_ref,
                 qaT_ref, vaT_ref, qbT_ref, vbT_ref, qiT_ref, iwT_ref):
    tm = x_ref.shape[1]
    h = _rms(x_ref[0], g_ref[...]).astype(BF16)

    def rm(c0, c1):
        return jnp.dot(h, wrm_ref[:, c0:c1], preferred_element_type=F32)

    def rope_rm(z):
        zm = pltpu.roll(z, LANES - HEAD_DIM // 2, axis=1)
        zp = pltpu.roll(z, HEAD_DIM // 2, axis=1)
        return z * cos_ref[...] + zm * sina_ref[...] + zp * sinb_ref[...]

    zi = rm(0, LANES)
    ms = jnp.sum(zi * zi, axis=-1, keepdims=True) * (1.0 / IDX_DIM)
    ki_ref[0] = rope_rm(zi * lax.rsqrt(ms + EPS) * gki_ref[...]).astype(BF16)
    cx_ref[0] = rm(LANES, LANES + C_WIDTH)
    cy_ref[0] = rm(LANES + C_WIDTH, LANES + 2 * C_WIDTH)

    def tr(r0, r1):
        return lax.dot_general(wt_ref[r0:r1, :], h, NT_DIMS, preferred_element_type=F32)

    def head_rms_t(z, gain):
        z4 = z.reshape(4, HEAD_DIM, tm)
        ms4 = jnp.mean(z4 * z4, axis=1, keepdims=True)
        return (z4 * lax.rsqrt(ms4 + EPS)).reshape(4 * HEAD_DIM, tm) * gain

    def rope_t(z):
        z4 = z.reshape(4, HEAD_DIM, tm)
        rot = jnp.concatenate([-z4[:, HEAD_DIM // 2:, :], z4[:, :HEAD_DIM // 2, :]], axis=1)
        return (z4 * cosT_ref[...][None] + rot * sinT_ref[...][None]).reshape(4 * HEAD_DIM, tm)

    def tr_heads(group):
        return tr(group * 4 * HEAD_DIM, (group + 1) * 4 * HEAD_DIM)

    scale = HEAD_DIM ** -0.5 * LOG2_E
    qaT_ref[0] = (head_rms_t(tr_heads(T_AQ), gqaT_ref[...]) * scale).astype(BF16)
    vaT_ref[0] = tr_heads(T_AV).astype(BF16)
    qbT_ref[0] = (rope_t(head_rms_t(tr_heads(T_BQ), gqbT_ref[...])) * scale).astype(BF16)
    vbT_ref[0] = tr_heads(T_BV).astype(BF16)
    qiT_ref[0] = rope_t(tr_heads(T_IQ)).astype(BF16)
    ka_ref[0] = head_rms_t(tr_heads(T_AK), gkaT_ref[...]).T.astype(BF16)
    kb_ref[0] = rope_t(head_rms_t(tr_heads(T_BK), gkbT_ref[...])).T.astype(BF16)
    iw0 = T_IW * 4 * HEAD_DIM
    iwT_ref[0] = tr(iw0, iw0 + 2 * SUBLANES)[0:SUBLANES] * (IDX_HEADS ** -0.5 * IDX_DIM ** -0.5)


def _project(layer, x, g, wrm, wt, gkaT, gkbT, gki, gqaT, gqbT, cos, sina, sinb, cosT, sinT, tm):
    b, s, d = x.shape
    grid = (b, s // tm)
    row = lambda w: pl.BlockSpec((1, tm, w), lambda bi, si: (bi, si, 0))
    col = lambda r: pl.BlockSpec((1, r, tm), lambda bi, si: (bi, 0, si))
    tab_rm = pl.BlockSpec((tm, LANES), lambda bi, si: (si, 0))
    tab_t = pl.BlockSpec((HEAD_DIM, tm), lambda bi, si: (0, si))
    per_layer = lambda a: _layer_spec(a, layer)
    in_specs = [row(d), per_layer(g), per_layer(wrm), per_layer(wt),
                per_layer(gkaT), per_layer(gkbT), per_layer(gki), per_layer(gqaT), per_layer(gqbT),
                tab_rm, tab_rm, tab_rm, tab_t, tab_t]
    out_shape = (jax.ShapeDtypeStruct((b, s, 256), BF16), jax.ShapeDtypeStruct((b, s, 256), BF16),
                 jax.ShapeDtypeStruct((b, s, 128), BF16),
                 jax.ShapeDtypeStruct((b, s, C_WIDTH), F32), jax.ShapeDtypeStruct((b, s, C_WIDTH), F32),
                 jax.ShapeDtypeStruct((b, 256, s), BF16), jax.ShapeDtypeStruct((b, 256, s), BF16),
                 jax.ShapeDtypeStruct((b, 256, s), BF16), jax.ShapeDtypeStruct((b, 256, s), BF16),
                 jax.ShapeDtypeStruct((b, 256, s), BF16), jax.ShapeDtypeStruct((b, 8, s), F32))
    out_specs = (row(256), row(256), row(128), row(C_WIDTH), row(C_WIDTH),
                 col(256), col(256), col(256), col(256), col(256), col(8))
    return pl.pallas_call(_proj_kernel, grid=grid, in_specs=in_specs, out_specs=out_specs, out_shape=out_shape,
                          compiler_params=_cparams("parallel", "parallel"), name="proj")(
        x, g, wrm, wt, gkaT, gkbT, gki, gqaT, gqbT, cos, sina, sinb, cosT, sinT)


O_AQ, O_AK, O_AV, O_BQ, O_BK, O_BV = 0, 256, 512, 768, 1024, 1280
O_IQ, O_IK, O_IW, O_CX, O_CY, O_GT = 1536, 1792, 1856, 1860, 2372, 2884
W_ROWMAJOR_COLS = LANES + 2 * C_WIDTH
W_TRANSPOSED_ROWS = 7 * 256 + LANES
W_PREP_ROWS = 256


def _weight_prep_kernel(w_ref, wrm_ref, wg_ref, wt_ref):
    rows = w_ref.shape[0]
    cols = lambda o, n: w_ref[:, o:o + n]
    wrm_ref[...] = jnp.concatenate(
        [cols(O_IK, IDX_DIM), jnp.zeros((rows, LANES - IDX_DIM), F32), cols(O_CX, C_WIDTH), cols(O_CY, C_WIDTH)],
        axis=1).astype(BF16)
    wg_ref[...] = w_ref[:, O_GT:].astype(BF16)
    groups = {T_AQ: O_AQ, T_AV: O_AV, T_BQ: O_BQ, T_BV: O_BV, T_IQ: O_IQ, T_AK: O_AK, T_BK: O_BK}
    wt = jnp.concatenate([cols(groups[t], 256) for t in range(T_IW)]
                         + [cols(O_IW, IDX_HEADS), jnp.zeros((rows, LANES - IDX_HEADS), F32)], axis=1)
    wt_ref[...] = wt.T.astype(BF16)


def _weight_prep(w_in):
    depth, d, n = w_in.shape
    n_gate = n - O_GT
    rb = W_PREP_ROWS
    return pl.pallas_call(
        _weight_prep_kernel, grid=(depth, d // rb),
        in_specs=[pl.BlockSpec((None, rb, n), lambda l, i: (l, i, 0))],
        out_specs=(pl.BlockSpec((None, rb, W_ROWMAJOR_COLS), lambda l, i: (l, i, 0)),
                   pl.BlockSpec((None, rb, n_gate), lambda l, i: (l, i, 0)),
                   pl.BlockSpec((None, W_TRANSPOSED_ROWS, rb), lambda l, i: (l, 0, i))),
        out_shape=(jax.ShapeDtypeStruct((depth, d, W_ROWMAJOR_COLS), BF16),
                   jax.ShapeDtypeStruct((depth, d, n_gate), BF16),
                   jax.ShapeDtypeStruct((depth, W_TRANSPOSED_ROWS, d), BF16)),
        compiler_params=_cparams("parallel", "parallel"), name="weight_prep")(w_in)


def _block_diag_q(qT):
    rows = lax.broadcasted_iota(I32, qT.shape, 0) // HEAD_DIM
    zero = jnp.zeros_like(qT)
    return jnp.concatenate([jnp.where(rows == hh, qT, zero) for hh in range(4)], axis=1)


def _chunk_attn_kernel(qT_ref, k_ref, vT_ref, bias_ref, o_ref):
    for sub in range(A_TILES_PER_STEP):
        _chunk_attn_tile(pl.program_id(1) * A_TILES_PER_STEP + sub,
                         qT_ref.at[0, :, sub * Q_TILE:(sub + 1) * Q_TILE], k_ref, vT_ref, bias_ref,
                         o_ref.at[0, sub * Q_TILE:(sub + 1) * Q_TILE, :])


def _chunk_attn_tile(i, qT_ref, k_ref, vT_ref, bias_ref, o_ref):
    qbd = _block_diag_q(qT_ref[...])
    ks, vs, pens = [], [], []
    for t in range(A_KEY_TILES):
        kt = i - (A_KEY_TILES - 1) + t
        r0 = pl.multiple_of(jnp.maximum(kt, 0) * Q_TILE, Q_TILE)
        ks.append(k_ref[0, pl.ds(r0, Q_TILE), :])
        vs.append(vT_ref[0, :, pl.ds(r0, Q_TILE)])
        pens.append(jnp.where(kt >= 0, 0.0, -jnp.inf).astype(F32))
    kwin = jnp.concatenate(ks, axis=0)
    vwin = jnp.concatenate(vs, axis=1)
    s = jnp.dot(kwin, qbd, preferred_element_type=F32) + bias_ref[...]
    s = jnp.concatenate([s[t * Q_TILE:(t + 1) * Q_TILE] + pens[t] for t in range(A_KEY_TILES)], axis=0)
    m = jnp.max(s, axis=0, keepdims=True)
    p = jnp.exp2(s - m)
    l = jnp.sum(p, axis=0, keepdims=True)
    pb = p.astype(BF16)
    outs = []
    for hh in range(A_HEADS):
        lo, hi = hh * Q_TILE, (hh + 1) * Q_TILE
        o = jnp.dot(vwin[hh * HEAD_DIM:(hh + 1) * HEAD_DIM, :], pb[:, lo:hi], preferred_element_type=F32)
        outs.append(o / l[:, lo:hi])
    o_ref[...] = jnp.concatenate(outs, axis=0).T.astype(BF16)


def _chunk_attention(layer, qaT, ka, vaT, biasT):
    b, _, s = qaT.shape
    step = A_TILES_PER_STEP * Q_TILE
    grid = (b, s // step)
    return pl.pallas_call(
        _chunk_attn_kernel, grid=grid,
        in_specs=[pl.BlockSpec((1, 256, step), lambda bi, i: (bi, 0, i)),
                  pl.BlockSpec((1, s, 256), lambda bi, i: (bi, 0, 0)),
                  pl.BlockSpec((1, 256, s), lambda bi, i: (bi, 0, 0)),
                  _layer_spec(biasT, layer)],
        out_specs=pl.BlockSpec((1, step, 256), lambda bi, i: (bi, i, 0)),
        out_shape=jax.ShapeDtypeStruct((b, s, 256), BF16),
        compiler_params=_cparams("parallel", "parallel"), name="chunk_attn")(qaT, ka, vaT, biasT)


def _bit_transpose32(words):
    a = list(words)
    mask, j = 0x0000FFFF, 16
    while j:
        k = 0
        while k < 32:
            t = (a[k] ^ (a[k + j] >> jnp.uint32(j))) & jnp.uint32(mask)
            a[k] = a[k] ^ t
            a[k + j] = a[k + j] ^ (t << jnp.uint32(j))
            k = (k + j + 1) & ~j
        j >>= 1
        mask = (mask ^ (mask << j)) & 0xFFFFFFFF
    return a


def _dsa_kernel(qiT_ref, iwT_ref, qbT_ref, ki_ref, kb_ref, vbT_ref, o_ref, scores_ref, planes_ref, *, topk):
    tk = B_KEY_TILE
    pair = pl.program_id(1)
    tiles = range(B_TILES_PER_STEP)
    n_small = pair + 1
    n_wide = lax.shift_right_logical(n_small, 2)
    has_mid, has_small = (n_small & 2) != 0, (n_small & 1) != 0
    lane = lax.broadcasted_iota(I32, (1, Q_TILE), 1)
    qis, ws, key_limits = [], [], []
    for tq in tiles:
        qiT = qiT_ref[0, :, tq * Q_TILE:(tq + 1) * Q_TILE]
        qi = jnp.concatenate([qiT[hh * IDX_DIM:(hh + 1) * IDX_DIM] for hh in range(IDX_HEADS)], axis=1)
        qis.append(jnp.concatenate([qi, jnp.zeros_like(qi)], axis=0))
        ws.append(iwT_ref[0, :, tq * Q_TILE:(tq + 1) * Q_TILE])
        i = pair * B_TILES_PER_STEP + tq
        key_limits.append(i * Q_TILE + CHUNK + jnp.where(lane >= CHUNK, CHUNK, 0))

    def over_key_tiles(span, carry):
        wide, small = 2 * tk, tk // 2
        carry = lax.fori_loop(0, n_wide, lambda u, c: span(pl.multiple_of(u * wide, wide), wide, c), carry)
        r_mid = pl.multiple_of(n_wide * wide, wide)
        carry = lax.cond(has_mid, lambda c: span(r_mid, tk, c), lambda c: c, carry)
        r_small = pl.multiple_of(r_mid + jnp.where(has_mid, tk, 0), small)
        return lax.cond(has_small, lambda c: span(r_small, small, c), lambda c: c, carry)

    def score_span(r0, rows, c):
        ki = ki_ref[0, pl.ds(r0, rows), :]
        row_iota = lax.broadcasted_iota(I32, (rows, Q_TILE), 0)
        for tq in tiles:
            dots = jnp.dot(ki, qis[tq], preferred_element_type=F32)
            sc = jnp.maximum(dots[:, 0:Q_TILE], 0.0) * ws[tq][0:1, :]
            for hh in range(1, IDX_HEADS):
                sc = sc + jnp.maximum(dots[:, hh * Q_TILE:(hh + 1) * Q_TILE], 0.0) * ws[tq][hh:hh + 1, :]
            sc = jnp.where(row_iota < key_limits[tq] - r0, sc, -jnp.inf)
            scores_ref[tq, pl.ds(r0, rows), :] = sc
            bits = lax.bitcast_convert_type(sc, U32)
            for g0 in range(0, rows, PLANE_GROUP):
                words = [bits[g0 + j * SUBLANES:g0 + (j + 1) * SUBLANES] for j in range(32)]
                planes = _bit_transpose32(words)
                magnitude = planes[1]
                for p in planes[2:]:
                    magnitude = magnitude | p
                negative = planes[0] & magnitude
                planes = [~negative] + [p ^ negative for p in planes[1:]]
                planes_ref[tq, pl.ds(r0 + g0, PLANE_GROUP), :] = jnp.concatenate(planes, axis=0)
        return c

    @pl.when(pair == 0)
    def _():
        planes_ref[...] = jnp.zeros_like(planes_ref)

    over_key_tiles(score_span, 0)

    max_groups = planes_ref.shape[1] // PLANE_GROUP
    full = jnp.full((SUBLANES, Q_TILE), 0xFFFFFFFF, U32)
    zero = jnp.zeros((1, Q_TILE), I32)

    def members(masks):
        parts = [lax.population_count(m) for m in masks]
        while len(parts) > 1:
            parts = [parts[j] + parts[j + 1] for j in range(0, len(parts) - 1, 2)] + parts[len(parts) & ~1:]
        return jnp.sum(parts[0].astype(I32), axis=0, keepdims=True)

    def search(n_groups):
        live0 = tuple(jnp.where(g < n_small, full, jnp.zeros_like(full)) for g in range(n_groups))

        def two_bits_tile(tq, row, bit_hi, state):
            live, n_gt, thr_bits = state
            p_hi = [planes_ref[tq, pl.ds(g * PLANE_GROUP + row, SUBLANES), :] for g in range(n_groups)]
            p_lo = [planes_ref[tq, pl.ds(g * PLANE_GROUP + row + SUBLANES, SUBLANES), :] for g in range(n_groups)]
            set_hi = [live[g] & p_hi[g] for g in range(n_groups)]
            clr_hi = [live[g] ^ set_hi[g] for g in range(n_groups)]
            set_both = [set_hi[g] & p_lo[g] for g in range(n_groups)]
            clr_set = [clr_hi[g] & p_lo[g] for g in range(n_groups)]
            n_hi = n_gt + members(set_hi)
            take_hi = n_hi >= topk
            n_gt = jnp.where(take_hi, n_gt, n_hi)
            n_lo = n_gt + jnp.where(take_hi, members(set_both), members(clr_set))
            take_lo = n_lo >= topk
            new_live = []
            for g in range(n_groups):
                kept = jnp.where(take_hi, set_hi[g], clr_hi[g])
                low = jnp.where(take_hi, set_both[g], clr_set[g])
                new_live.append(jnp.where(take_lo, low, kept ^ low))
            bit_lo = lax.shift_right_logical(bit_hi, 1)
            thr_bits = thr_bits | jnp.where(take_hi, bit_hi, 0) | jnp.where(take_lo, bit_lo, 0)
            return tuple(new_live), jnp.where(take_lo, n_gt, n_lo), thr_bits

        def two_bits(it, states):
            row = pl.multiple_of(it * (2 * SUBLANES), 2 * SUBLANES)
            bit_hi = jnp.int32(1) << (31 - 2 * it)
            return tuple(two_bits_tile(tq, row, bit_hi, states[tq]) for tq in tiles)

        states = lax.fori_loop(0, 16, two_bits, tuple((live0, zero, zero) for _ in tiles))
        return tuple((n_gt, thr_bits) for _, n_gt, thr_bits in states)

    if max_groups % 2 == 0:
        found = lax.cond(n_small <= max_groups // 2, lambda: search(max_groups // 2), lambda: search(max_groups))
    else:
        found = search(max_groups)
    thrs, needs = [], []
    for n_gt, thr_bits in found:
        thr = lax.bitcast_convert_type(jnp.where(thr_bits < 0, thr_bits ^ jnp.int32(INT_MIN), ~thr_bits), F32)
        thrs.append(thr)
        needs.append(jnp.where(thr == -jnp.inf, 0, topk - n_gt).astype(F32))
    hk = tk // 2
    tril = (lax.broadcasted_iota(I32, (hk, hk), 0) >= lax.broadcasted_iota(I32, (hk, hk), 1)).astype(BF16)

    qbds = [_block_diag_q(qbT_ref[0, :, tq * Q_TILE:(tq + 1) * Q_TILE]) for tq in tiles]

    def attn_span_tile(tq, r0, rows, kb, carry):
        ms, ls, accs, seen = carry
        key = scores_ref[tq, pl.ds(r0, rows), :]
        tied = key == thrs[tq]
        tied_b = jnp.where(tied, 1.0, 0.0).astype(BF16)
        ranks = []
        for c0 in range(0, rows, hk):
            ranks.append(jnp.dot(tril, tied_b[c0:c0 + hk], preferred_element_type=F32) + seen)
            seen = ranks[-1][hk - 1:hk, :]
        sel = jnp.logical_or(key > thrs[tq], jnp.logical_and(tied, jnp.concatenate(ranks, axis=0) <= needs[tq]))
        s_all = jnp.dot(kb, qbds[tq], preferred_element_type=F32)
        new_ms, new_ls, new_accs = [], [], []
        for hh in range(B_HEADS):
            s = jnp.where(sel, s_all[:, hh * Q_TILE:(hh + 1) * Q_TILE], -jnp.inf)
            m_new = jnp.maximum(ms[hh], jnp.max(s, axis=0, keepdims=True))
            alpha = jnp.exp2(ms[hh] - m_new)
            p = jnp.exp2(s - m_new)
            new_ls.append(alpha * ls[hh] + jnp.sum(p, axis=0, keepdims=True))
            pv = jnp.dot(vbT_ref[0, hh * HEAD_DIM:(hh + 1) * HEAD_DIM, pl.ds(r0, rows)], p.astype(BF16),
                         preferred_element_type=F32)
            new_accs.append(alpha * accs[hh] + pv)
            new_ms.append(m_new)
        return tuple(new_ms), tuple(new_ls), tuple(new_accs), seen

    def attn_span(r0, rows, carries):
        kb = kb_ref[0, pl.ds(r0, rows), :]
        return tuple(attn_span_tile(tq, r0, rows, kb, carries[tq]) for tq in tiles)

    init = (tuple(jnp.full((1, Q_TILE), -1e30, F32) for _ in range(B_HEADS)),
            tuple(jnp.zeros((1, Q_TILE), F32) for _ in range(B_HEADS)),
            tuple(jnp.zeros((HEAD_DIM, Q_TILE), F32) for _ in range(B_HEADS)),
            jnp.zeros((1, Q_TILE), F32))
    finals = over_key_tiles(attn_span, tuple(init for _ in tiles))
    for tq in tiles:
        _, ls, accs, _ = finals[tq]
        out = jnp.concatenate([accs[hh] / ls[hh] for hh in range(B_HEADS)], axis=0)
        o_ref[0, tq * Q_TILE:(tq + 1) * Q_TILE, :] = out.T.astype(BF16)


def _sparse_attention(qiT, iwT, qbT, ki, kb, vbT):
    b, _, s = qbT.shape
    s_pad = -(-s // B_KEY_TILE) * B_KEY_TILE
    topk = min(TOPK_MAX, s // 4)
    step = B_TILES_PER_STEP * Q_TILE
    grid = (b, s // step)
    qspec = lambda r: pl.BlockSpec((1, r, step), lambda bi, i: (bi, 0, i))
    kernel = functools.partial(_dsa_kernel, topk=topk)
    return pl.pallas_call(
        kernel, grid=grid,
        in_specs=[qspec(256), qspec(8), qspec(256),
                  pl.BlockSpec((1, s, 128), lambda bi, i: (bi, 0, 0)),
                  pl.BlockSpec((1, s, 256), lambda bi, i: (bi, 0, 0)),
                  pl.BlockSpec((1, 256, s), lambda bi, i: (bi, 0, 0))],
        out_specs=pl.BlockSpec((1, step, 256), lambda bi, i: (bi, i, 0)),
        out_shape=jax.ShapeDtypeStruct((b, s, 256), BF16),
        scratch_shapes=[pltpu.VMEM((B_TILES_PER_STEP, s_pad, Q_TILE), F32),
                        pltpu.VMEM((B_TILES_PER_STEP, s_pad, Q_TILE), U32)],
        compiler_params=_cparams("parallel", "arbitrary"), name="sparse_attn")(qiT, iwT, qbT, ki, kb, vbT)


def _rglru_kernel(cx_ref, cy_ref, cw_ref, cb_ref, wax_ref, ba_ref, bx_ref, lam_ref, o_ref,
                  tail_ref, h_ref, a_ref, g_ref):
    ts = cx_ref.shape[1]

    @pl.when(pl.program_id(1) == 0)
    def _():
        tail_ref[...] = jnp.zeros_like(tail_ref)
        h_ref[...] = jnp.zeros_like(h_ref)

    x = cx_ref[0]
    tail = tail_ref[...]
    row8 = lax.broadcasted_iota(I32, (SUBLANES, C_WIDTH), 0)
    u = cb_ref[...] + x * cw_ref[C_CONV - 1:C_CONV, :]
    for k in range(1, C_CONV):
        xr = pltpu.roll(x, k, axis=0)
        head = jnp.where(row8 < k, pltpu.roll(tail, k, axis=0), xr[0:SUBLANES])
        xk = jnp.concatenate([head, xr[SUBLANES:]], axis=0)
        u = u + xk * cw_ref[C_CONV - 1 - k:C_CONV - k, :]
    tail_ref[...] = x[ts - SUBLANES:ts]

    gates = jnp.dot(u.astype(BF16), wax_ref[...], preferred_element_type=F32)
    r = _sigmoid(gates[:, :C_WIDTH] + ba_ref[...])
    ig = _sigmoid(gates[:, C_WIDTH:] + bx_ref[...])
    nlam = -lam_ref[...]
    softplus = jnp.maximum(nlam, 0.0) + jnp.log1p(jnp.exp(-jnp.abs(nlam)))
    a = jnp.exp2(r * (softplus * (-LRU_C * LOG2_E)))
    a_ref[...] = a
    y = 1.0 - a * a
    root = jnp.where(y > 0.0, y * lax.rsqrt(y), 0.0)
    g_ref[...] = root * (ig * u)

    def group(gi, hprev):
        r0 = pl.multiple_of(gi * SUBLANES, SUBLANES)
        av = a_ref[pl.ds(r0, SUBLANES), :]
        bv = g_ref[pl.ds(r0, SUBLANES), :]
        for d in (1, 2, 4):
            keep = row8 >= d
            a_sh = jnp.where(keep, pltpu.roll(av, d, axis=0), 1.0)
            b_sh = jnp.where(keep, pltpu.roll(bv, d, axis=0), 0.0)
            bv = av * b_sh + bv
            av = av * a_sh
        hs = av * hprev + bv
        g_ref[pl.ds(r0, SUBLANES), :] = hs
        return jnp.broadcast_to(hs[SUBLANES - 1:SUBLANES, :], hs.shape)

    h_ref[...] = lax.fori_loop(0, ts // SUBLANES, group, h_ref[...], unroll=8)
    o_ref[0] = (g_ref[...] * jax.nn.gelu(cy_ref[0])).astype(BF16)


def _rglru(layer, cx, cy, cw, cb, wax, ba, bx, lam, ts):
    b, s, c = cx.shape
    blk = pl.BlockSpec((1, ts, c), lambda bi, si: (bi, si, 0))
    return pl.pallas_call(
        _rglru_kernel, grid=(b, s // ts),
        in_specs=[blk, blk] + [_layer_spec(a, layer) for a in (cw, cb, wax, ba, bx, lam)],
        out_specs=blk, out_shape=jax.ShapeDtypeStruct((b, s, c), BF16),
        scratch_shapes=[pltpu.VMEM((SUBLANES, c), F32), pltpu.VMEM((SUBLANES, c), F32),
                        pltpu.VMEM((ts, c), F32), pltpu.VMEM((ts, c), F32)],
        compiler_params=_cparams("parallel", "arbitrary"), name="rglru")(cx, cy, cw, cb, wax, ba, bx, lam)


def _merge_kernel(x_ref, ya_ref, yb_ref, yc_ref, g_ref, wg_ref, bg_ref, wb_ref, wo_ref, o_ref):
    x = x_ref[...]
    d = x.shape[1]
    h = _rms(x, g_ref[...]).astype(BF16)
    ys = (ya_ref[...], yb_ref[...], yc_ref[...])
    offs = (0, A_WIDTH, A_WIDTH + B_WIDTH, A_WIDTH + B_WIDTH + C_WIDTH)
    halves = []
    for c0 in range(0, d, d // 2):
        c1 = c0 + d // 2
        merged = None
        for j in range(N_BRANCH):
            gate = _sigmoid(jnp.dot(h, wg_ref[:, j * d + c0:j * d + c1], preferred_element_type=F32)
                            + bg_ref[:, j * d + c0:j * d + c1])
            term = gate * jnp.dot(ys[j], wb_ref[offs[j]:offs[j + 1], c0:c1], preferred_element_type=F32)
            merged = term if merged is None else merged + term
        halves.append(merged.astype(BF16))
    o_ref[...] = x + jnp.dot(jnp.concatenate(halves, axis=1), wo_ref[...], preferred_element_type=F32)


def _merge(layer, x2, ya, yb, yc, g, wg, bg, wb, wo, tm):
    n, d = x2.shape
    row = lambda w: pl.BlockSpec((tm, w), lambda i: (i, 0))
    return pl.pallas_call(
        _merge_kernel, grid=(n // tm,),
        in_specs=[row(d), row(A_WIDTH), row(B_WIDTH), row(C_WIDTH)]
        + [_layer_spec(a, layer) for a in (g, wg, bg, wb, wo)],
        out_specs=row(d), out_shape=jax.ShapeDtypeStruct((n, d), F32),
        compiler_params=_cparams("parallel"), name="merge")(x2, ya, yb, yc, g, wg, bg, wb, wo)


def _ffn_kernel(x_ref, g_ref, wi_ref, wo_ref, o_ref, *, d_ff, chunk):
    x = x_ref[...]
    h = _rms(x, g_ref[...]).astype(BF16)
    acc = x
    for c0 in range(0, d_ff, chunk):
        c1 = min(c0 + chunk, d_ff)
        gp = jnp.dot(h, wi_ref[:, c0:c1], preferred_element_type=F32)
        up = jnp.dot(h, wi_ref[:, d_ff + c0:d_ff + c1], preferred_element_type=F32)
        act = (gp * _sigmoid(gp) * up).astype(BF16)
        acc = acc + jnp.dot(act, wo_ref[c0:c1, :], preferred_element_type=F32)
    o_ref[...] = acc


def _ffn(layer, x2, g, wi, wo, tm):
    n, d = x2.shape
    d_ff = wo.shape[1]
    row = pl.BlockSpec((tm, d), lambda i: (i, 0))
    kernel = functools.partial(_ffn_kernel, d_ff=d_ff, chunk=1024)
    return pl.pallas_call(
        kernel, grid=(n // tm,),
        in_specs=[row] + [_layer_spec(a, layer) for a in (g, wi, wo)],
        out_specs=row, out_shape=jax.ShapeDtypeStruct((n, d), F32),
        compiler_params=_cparams("parallel"), name="ffn")(x2, g, wi, wo)


def _rope_tables(s):
    pos = jnp.arange(s, dtype=F32)
    inv = 1.0 / (ROPE_THETA ** (jnp.arange(0, HEAD_DIM, 2, dtype=F32) / HEAD_DIM))
    ang = pos[:, None] * inv[None, :]
    cos, sin = jnp.cos(ang), jnp.sin(ang)
    return jnp.concatenate([cos, cos], axis=-1), jnp.concatenate([sin, sin], axis=-1)


def _bias_table_kernel(vec_ref, o_ref):
    nk, q_tile = o_ref.shape
    rolled = pltpu.roll(jnp.broadcast_to(vec_ref[...], (nk, nk + q_tile)), 0, axis=1, stride=1, stride_axis=0)
    j = lax.broadcasted_iota(I32, (nk, q_tile), 0)
    q = lax.broadcasted_iota(I32, (nk, q_tile), 1)
    shift = CHUNK.bit_length() - 1
    dchunk = A_LEFT_CHUNKS + (q >> shift) - (j >> shift)
    valid = jnp.logical_and(dchunk >= 0, dchunk <= A_LEFT_CHUNKS)
    o_ref[...] = jnp.where(valid, rolled[:, nk:] * LOG2_E, -jnp.inf)


def _rel_bias_tables(rel_bias):
    depth = rel_bias.shape[0]
    nk = A_KEY_TILES * Q_TILE
    assert REL_CLIP == Q_TILE and CHUNK & (CHUNK - 1) == 0
    rb = rel_bias.astype(F32)
    vec = jnp.concatenate([rb, jnp.tile(rb[..., -1:], (1, 1, nk + Q_TILE - rb.shape[-1]))], axis=-1)
    return pl.pallas_call(
        _bias_table_kernel, grid=(depth, A_HEADS),
        in_specs=[pl.BlockSpec((None, None, 1, nk + Q_TILE), lambda l, h: (l, h, 0, 0))],
        out_specs=pl.BlockSpec((None, nk, Q_TILE), lambda l, h: (l, 0, h)),
        out_shape=jax.ShapeDtypeStruct((depth, nk, A_HEADS * Q_TILE), F32),
        compiler_params=_cparams("parallel", "parallel"), name="bias_table")(vec[:, :, None, :])


def kernel(x, g_mix, w_in, qk_gain_a, rel_bias, qk_gain_b, g_idx_k, conv_w, conv_b, lru_wa, lru_ba, lru_wx,
           lru_bx, lru_lambda, b_gate, w_branch, w_out, g_ffn, w_ffn_in, w_ffn_out):
    b, s, d = x.shape
    depth = g_mix.shape[0]
    assert s % PROJ_ROWS == 0 and d % LANES == 0
    tm = DENSE_ROWS
    cos, sin = _rope_tables(s)
    half = (jnp.arange(HEAD_DIM) < HEAD_DIM // 2).astype(F32)
    cos_rm = jnp.tile(cos, (1, LANES // HEAD_DIM))
    sina_rm = jnp.tile(-sin * half[None, :], (1, LANES // HEAD_DIM))
    sinb_rm = jnp.tile(sin * (1.0 - half)[None, :], (1, LANES // HEAD_DIM))
    cosT, sinT = cos.T, sin.T
    rows = lambda v: v.reshape(depth, 1, -1).astype(F32)
    blocks = jnp.arange(C_WIDTH) // (C_WIDTH // C_BLOCKS)
    bmask = (blocks[:, None] == blocks[None, :])

    def dense_blocks(wblk):
        return jnp.where(bmask, jnp.tile(wblk.reshape(depth, C_WIDTH, C_WIDTH // C_BLOCKS), (1, 1, C_BLOCKS)), 0.0)

    wrm, w_gate, wt = _weight_prep(w_in)
    head_gainT = lambda gain: jnp.tile(gain, (1, 4)).reshape(depth, 256, 1)
    gkaT, gkbT = head_gainT(qk_gain_a[:, 1]), head_gainT(qk_gain_b[:, 1])
    gqaT, gqbT = head_gainT(qk_gain_a[:, 0]), head_gainT(qk_gain_b[:, 0])
    gki = rows(jnp.concatenate([g_idx_k, jnp.zeros((depth, 64), F32)], axis=1))
    bias_tabs = _rel_bias_tables(rel_bias)
    wax = jnp.concatenate([dense_blocks(lru_wa), dense_blocks(lru_wx)], axis=2).astype(BF16)
    g_mix3, g_ffn3, b_gate3 = rows(g_mix), rows(g_ffn), rows(b_gate)
    cb3, ba3, bx3, lam3 = rows(conv_b), rows(lru_ba), rows(lru_bx), rows(lru_lambda)
    wb, wo = w_branch.astype(BF16), w_out.astype(BF16)
    wfi, wfo = w_ffn_in.astype(BF16), w_ffn_out.astype(BF16)

    n = b * s
    for l in range(depth):
        ka, kb, ki, cx, cy, qaT, vaT, qbT, vbT, qiT, iwT = _project(
            l, x, g_mix3, wrm, wt, gkaT, gkbT, gki, gqaT, gqbT, cos_rm, sina_rm, sinb_rm, cosT, sinT, PROJ_ROWS)
        ya = _chunk_attention(l, qaT, ka, vaT, bias_tabs)
        yb = _sparse_attention(qiT, iwT, qbT, ki, kb, vbT)
        yc = _rglru(l, cx, cy, conv_w, cb3, wax, ba3, bx3, lam3, 512)
        x1 = _merge(l, x.reshape(n, d), ya.reshape(n, A_WIDTH), yb.reshape(n, B_WIDTH), yc.reshape(n, C_WIDTH),
                    g_mix3, w_gate, b_gate3, wb, wo, tm)
        x = _ffn(l, x1, g_ffn3, wfi, wfo, tm).reshape(b, s, d)
    return x
```
